```python
import jax, jax.numpy as jnp
from jax import lax
import numpy as np

D_MODEL = 2048
BATCH = 8
SEQ = 2048
DEPTH = 2

HEAD_DIM = 64
D_FF = 256 * ((8 * D_MODEL // 3 + 255) // 256)
ATTN_BLOCK = 128
SWA_HEADS = D_MODEL // (4 * HEAD_DIM)
SWA_KV_HEADS = 2
SWA_WINDOW = 128
NSA_HEADS = D_MODEL // (4 * HEAD_DIM)
NSA_KV_HEADS = 2
NSA_CMP_LEN = 32
NSA_CMP_STRIDE = 16
NSA_CMP_HIDDEN = 256
NSA_SEL_BLOCK = 64
NSA_TOP_N = 16
NSA_WINDOW = 512
NSA_Q_CHUNK = 64
RNN_WIDTH = D_MODEL // 2
RNN_BLOCKS = 16
RNN_BLOCK_WIDTH = RNN_WIDTH // RNN_BLOCKS
CONV_WIDTH = 4
LRU_C = 8.0
D_SWA = SWA_HEADS * HEAD_DIM
D_NSA = NSA_HEADS * HEAD_DIM
D_MIX = D_SWA + D_NSA + RNN_WIDTH
SWA_KVD = SWA_KV_HEADS * HEAD_DIM
NSA_KVD = NSA_KV_HEADS * HEAD_DIM
IN_SPLITS = (D_SWA, SWA_KVD, SWA_KVD,
             D_NSA, NSA_KVD, NSA_KVD, NSA_KVD, NSA_KVD, NSA_KVD, NSA_KVD, 3 * NSA_HEADS,
             RNN_WIDTH, RNN_WIDTH)
D_IN = sum(IN_SPLITS)
RMS_EPS = 1e-6
NEG_INF = -1e30
FORCED_SCORE = 1e4

kernel_name = "hybrid_swa_nsa_rglru_macaron"


def rmsnorm(x, g):
    xf = x.astype(jnp.float32)
    y = xf * lax.rsqrt(jnp.mean(xf * xf, axis=-1, keepdims=True) + RMS_EPS)
    return (y * g.astype(jnp.float32)).astype(x.dtype)


def swiglu(x, wg, wu, wd):
    return (jax.nn.silu(x @ wg) * (x @ wu)) @ wd


def banded_window_attention(q, k, v, window, sink=None):
    b, s, kvh, g, dh = q.shape
    blk = ATTN_BLOCK
    nb = s // blk
    n_prev = -(-(window - 1) // blk)
    span = (n_prev + 1) * blk

    def band(t):
        tp = jnp.pad(t, ((0, 0), (n_prev * blk, 0), (0, 0), (0, 0)))
        tp = tp.reshape(b, nb + n_prev, blk, kvh, dh)
        return jnp.concatenate([tp[:, j:j + nb] for j in range(n_prev + 1)], axis=2)

    kb, vb = band(k), band(v)
    qb = q.reshape(b, nb, blk, kvh, g, dh)
    scores = jnp.einsum('bnqhgd,bnkhd->bnhgqk', qb, kb).astype(jnp.float32) * (dh ** -0.5)
    qpos = jnp.arange(nb)[:, None, None] * blk + jnp.arange(blk)[None, :, None]
    kpos = (jnp.arange(nb)[:, None, None] - n_prev) * blk + jnp.arange(span)[None, None, :]
    diff = qpos - kpos
    mask = (diff >= 0) & (diff < window) & (kpos >= 0)
    scores = jnp.where(mask[None, :, None, None], scores, NEG_INF)
    m = jnp.max(scores, axis=-1, keepdims=True)
    if sink is not None:
        sk = sink.astype(jnp.float32).reshape(1, 1, kvh, g, 1, 1)
        m = jnp.maximum(m, sk)
        p = jnp.exp(scores - m)
        denom = jnp.sum(p, axis=-1, keepdims=True) + jnp.exp(sk - m)
    else:
        p = jnp.exp(scores - m)
        denom = jnp.sum(p, axis=-1, keepdims=True)
    p = p / denom
    out = jnp.einsum('bnhgqk,bnkhd->bnqhgd', p, vb.astype(jnp.float32))
    return out.reshape(b, s, kvh, g, dh)


def nsa_attention(q, k_cmp, v_cmp, k_slc, v_slc, k_win, v_win, gate_logits,
                  cmp_pos, cmp_w1, cmp_b1, cmp_w2):
    b, s, kvh, g, dh = q.shape
    scale = dh ** -0.5
    pos = jnp.arange(s)

    n_c = (s - NSA_CMP_LEN) // NSA_CMP_STRIDE + 1
    tok_idx = np.arange(n_c)[:, None] * NSA_CMP_STRIDE + np.arange(NSA_CMP_LEN)[None, :]

    def compress(t, i):
        blocks = t[:, tok_idx] + cmp_pos[i][None, None, :, None, :]
        flat = blocks.transpose(0, 1, 3, 2, 4).reshape(b, n_c, kvh, NSA_CMP_LEN * dh)
        return jax.nn.gelu(flat @ cmp_w1[i] + cmp_b1[i]) @ cmp_w2[i]

    kc = compress(k_cmp, 0)
    vc = compress(v_cmp, 1)
    cmp_end = jnp.arange(n_c) * NSA_CMP_STRIDE + NSA_CMP_LEN - 1
    cmask = (cmp_end[None, :] <= pos[:, None])[None, :, None, None, :]
    sc = jnp.einsum('bshgd,bchd->bshgc', q, kc).astype(jnp.float32) * scale
    sc = jnp.where(cmask, sc, NEG_INF)
    p_cmp = jax.nn.softmax(sc, axis=-1) * cmask
    o_cmp = jnp.einsum('bshgc,bchd->bshgd', p_cmp, vc.astype(jnp.float32))

    n_sel = s // NSA_SEL_BLOCK
    cs = np.arange(n_c)[:, None] * NSA_CMP_STRIDE
    ss = np.arange(n_sel)[None, :] * NSA_SEL_BLOCK
    overlap = np.clip(np.minimum(cs + NSA_CMP_LEN, ss + NSA_SEL_BLOCK) - np.maximum(cs, ss), 0, None)
    overlap = jnp.asarray(overlap / NSA_CMP_LEN, jnp.float32)
    imp = jnp.einsum('bshc,cj->bshj', jnp.sum(p_cmp, axis=3), overlap)
    blk_id = jnp.arange(n_sel)[None, :]
    cur = (pos // NSA_SEL_BLOCK)[:, None]
    forced = (blk_id == 0) | (blk_id == cur) | (blk_id == cur - 1)
    valid = blk_id * NSA_SEL_BLOCK <= pos[:, None]
    imp = jnp.where(forced[None, :, None, :], FORCED_SCORE, imp)
    imp = jnp.where(valid[None, :, None, :], imp, NEG_INF)
    n_top = min(NSA_TOP_N, n_sel)
    _, sel_idx = lax.top_k(imp, n_top)

    kb = k_slc.reshape(b, n_sel, NSA_SEL_BLOCK, kvh, dh).transpose(0, 3, 1, 2, 4)
    vb = v_slc.reshape(b, n_sel, NSA_SEL_BLOCK, kvh, dh).transpose(0, 3, 1, 2, 4)
    bi = jnp.arange(b)[:, None, None, None]
    hi = jnp.arange(kvh)[None, None, :, None]
    nq = s // NSA_Q_CHUNK

    def sel_chunk(args):
        qc, ic, pc = args
        kg = kb[bi, hi, ic]
        vg = vb[bi, hi, ic]
        c = qc.shape[1]
        scs = jnp.einsum('bchgd,bchnld->bchgnl', qc, kg).astype(jnp.float32) * scale
        kpos = ic[..., None] * NSA_SEL_BLOCK + jnp.arange(NSA_SEL_BLOCK)
        km = (kpos <= pc[None, :, None, None, None])[:, :, :, None]
        scs = jnp.where(km, scs, NEG_INF).reshape(b, c, kvh, g, n_top * NSA_SEL_BLOCK)
        ps = jax.nn.softmax(scs, axis=-1).reshape(b, c, kvh, g, n_top, NSA_SEL_BLOCK)
        return jnp.einsum('bchgnl,bchnld->bchgd', ps, vg.astype(jnp.float32))

    def chunks(t):
        return t.reshape(b, nq, NSA_Q_CHUNK, *t.shape[2:]).swapaxes(0, 1)

    o_slc = lax.map(sel_chunk, (chunks(q), chunks(sel_idx), pos.reshape(nq, NSA_Q_CHUNK)))
    o_slc = o_slc.swapaxes(0, 1).reshape(b, s, kvh, g, dh)

    o_win = banded_window_attention(q, k_win, v_win, NSA_WINDOW)

    gw = jax.nn.sigmoid(gate_logits.astype(jnp.float32)).reshape(b, s, 3, kvh, g)[..., None]
    return gw[:, :, 0] * o_cmp + gw[:, :, 1] * o_slc + gw[:, :, 2] * o_win


def rglru_block(xr, xg, conv_w, conv_b, wa, ba, wx, bx, lam):
    b, s, c = xr.shape
    xc = lax.conv_general_dilated(xr, conv_w[:, None, :], (1,), [(CONV_WIDTH - 1, 0)],
                                  dimension_numbers=('NWC', 'WIO', 'NWC'),
                                  feature_group_count=c) + conv_b
    xblk = xc.reshape(b, s, RNN_BLOCKS, RNN_BLOCK_WIDTH)
    r = jax.nn.sigmoid((jnp.einsum('bsnc,ncd->bsnd', xblk, wa).reshape(b, s, c) + ba).astype(jnp.float32))
    i = jax.nn.sigmoid((jnp.einsum('bsnc,ncd->bsnd', xblk, wx).reshape(b, s, c) + bx).astype(jnp.float32))
    log_a = -LRU_C * r * jax.nn.softplus(-lam.astype(jnp.float32))
    a = jnp.exp(log_a)
    u = jnp.sqrt(-jnp.expm1(2.0 * log_a)) * (i * xc.astype(jnp.float32))

    def combine(left, right):
        a1, b1 = left
        a2, b2 = right
        return a1 * a2, a2 * b1 + b2

    _, h = lax.associative_scan(combine, (a, u), axis=1)
    return (jax.nn.gelu(xg.astype(jnp.float32)) * h).astype(xr.dtype)


def setup_inputs(seed: int = 0) -> dict:
    key = jax.random.key(seed)
    ks = iter(jax.random.split(key, 40))

    def nrm(shape, scale):
        return jax.random.normal(next(ks), shape, jnp.float32) * scale

    def gain(shape):
        return 1.0 + 0.01 * jax.random.normal(next(ks), shape, jnp.float32)

    a0 = jax.random.uniform(next(ks), (DEPTH, RNN_WIDTH), jnp.float32, 0.9, 0.999)
    base = a0 ** (1.0 / LRU_C)
    lru_lambda = jnp.log(base) - jnp.log1p(-base)
    return {
        "x": jax.random.normal(next(ks), (BATCH, SEQ, D_MODEL), jnp.float32),
        "ffn1_norm": gain((DEPTH, D_MODEL)),
        "ffn1_w_gate": nrm((DEPTH, D_MODEL, D_FF), D_MODEL ** -0.5),
        "ffn1_w_up": nrm((DEPTH, D_MODEL, D_FF), D_MODEL ** -0.5),
        "ffn1_w_down": nrm((DEPTH, D_FF, D_MODEL), D_FF ** -0.5),
        "mix_norm": gain((DEPTH, D_MODEL)),
        "w_in": nrm((DEPTH, D_MODEL, D_IN), D_MODEL ** -0.5),
        "swa_sinks": nrm((DEPTH, SWA_HEADS), 0.5),
        "cmp_pos": nrm((DEPTH, 2, NSA_CMP_LEN, HEAD_DIM), 0.5),
        "cmp_w1": nrm((DEPTH, 2, NSA_CMP_LEN * HEAD_DIM, NSA_CMP_HIDDEN), (NSA_CMP_LEN * HEAD_DIM) ** -0.5),
        "cmp_b1": nrm((DEPTH, 2, NSA_CMP_HIDDEN), 0.01),
        "cmp_w2": nrm((DEPTH, 2, NSA_CMP_HIDDEN, HEAD_DIM), NSA_CMP_HIDDEN ** -0.5),
        "conv_w": nrm((DEPTH, CONV_WIDTH, RNN_WIDTH), CONV_WIDTH ** -0.5),
        "conv_b": nrm((DEPTH, RNN_WIDTH), 0.01),
        "lru_wa": nrm((DEPTH, RNN_BLOCKS, RNN_BLOCK_WIDTH, RNN_BLOCK_WIDTH), RNN_BLOCK_WIDTH ** -0.5),
        "lru_ba": nrm((DEPTH, RNN_WIDTH), 0.01),
        "lru_wx": nrm((DEPTH, RNN_BLOCKS, RNN_BLOCK_WIDTH, RNN_BLOCK_WIDTH), RNN_BLOCK_WIDTH ** -0.5),
        "lru_bx": nrm((DEPTH, RNN_WIDTH), 0.01),
        "lru_lambda": lru_lambda,
        "group_norm": gain((DEPTH, D_MIX)),
        "w_out": nrm((DEPTH, D_MIX, D_MODEL), D_MIX ** -0.5),
        "ffn2_norm": gain((DEPTH, D_MODEL)),
        "ffn2_w_gate": nrm((DEPTH, D_MODEL, D_FF), D_MODEL ** -0.5),
        "ffn2_w_up": nrm((DEPTH, D_MODEL, D_FF), D_MODEL ** -0.5),
        "ffn2_w_down": nrm((DEPTH, D_FF, D_MODEL), D_FF ** -0.5),
        "final_norm": gain((D_MODEL,)),
    }


def reference(x, ffn1_norm, ffn1_w_gate, ffn1_w_up, ffn1_w_down, mix_norm, w_in, swa_sinks,
              cmp_pos, cmp_w1, cmp_b1, cmp_w2, conv_w, conv_b, lru_wa, lru_ba, lru_wx, lru_bx,
              lru_lambda, group_norm, w_out, ffn2_norm, ffn2_w_gate, ffn2_w_up, ffn2_w_down,
              final_norm):
    b, s, _ = x.shape
    split_at = [int(v) for v in np.cumsum(IN_SPLITS)[:-1]]
    ga = SWA_HEADS // SWA_KV_HEADS
    gb = NSA_HEADS // NSA_KV_HEADS
    for l in range(DEPTH):
        x = x + 0.5 * swiglu(rmsnorm(x, ffn1_norm[l]), ffn1_w_gate[l], ffn1_w_up[l], ffn1_w_down[l])

        h = rmsnorm(x, mix_norm[l])
        proj = h @ w_in[l]
        (qa, ka, va, qb, kcb, vcb, ksb, vsb, kwb, vwb, gl, xr, xg) = jnp.split(proj, split_at, axis=-1)

        def kvh(t, n):
            return t.reshape(b, s, n, HEAD_DIM)

        ya = banded_window_attention(qa.reshape(b, s, SWA_KV_HEADS, ga, HEAD_DIM),
                                     kvh(ka, SWA_KV_HEADS), kvh(va, SWA_KV_HEADS),
                                     SWA_WINDOW, swa_sinks[l]).reshape(b, s, D_SWA)
        yb = nsa_attention(qb.reshape(b, s, NSA_KV_HEADS, gb, HEAD_DIM),
                           kvh(kcb, NSA_KV_HEADS), kvh(vcb, NSA_KV_HEADS),
                           kvh(ksb, NSA_KV_HEADS), kvh(vsb, NSA_KV_HEADS),
                           kvh(kwb, NSA_KV_HEADS), kvh(vwb, NSA_KV_HEADS), gl,
                           cmp_pos[l], cmp_w1[l], cmp_b1[l], cmp_w2[l]).reshape(b, s, D_NSA)
        yc = rglru_block(xr, xg, conv_w[l], conv_b[l], lru_wa[l], lru_ba[l], lru_wx[l], lru_bx[l],
                         lru_lambda[l])

        gn = group_norm[l]
        y = jnp.concatenate([
            rmsnorm(ya.astype(x.dtype), gn[:D_SWA]),
            rmsnorm(yb.astype(x.dtype), gn[D_SWA:D_SWA + D_NSA]),
            rmsnorm(yc, gn[D_SWA + D_NSA:]),
        ], axis=-1)
        x = x + y @ w_out[l]

        x = x + 0.5 * swiglu(rmsnorm(x, ffn2_norm[l]), ffn2_w_gate[l], ffn2_w_up[l], ffn2_w_down[l])
    return rmsnorm(x, final_norm)
```

```python
import functools

import jax
import jax.numpy as jnp
import numpy as np
from jax import lax
from jax.experimental import pallas as pl
from jax.experimental.pallas import tpu as pltpu

F32 = jnp.float32
BF16 = jnp.bfloat16

D_MODEL = 2048
HEAD_DIM = 64
D_FF = 5632
N_HEADS = 8
N_KV = 2
GROUP = N_HEADS // N_KV
D_ATT = N_HEADS * HEAD_DIM
D_KV = N_KV * HEAD_DIM
SWA_WINDOW = 128
NSA_WINDOW = 512
ATTN_BLOCK = 128
CMP_LEN = 32
CMP_STRIDE = 16
CMP_HIDDEN = 256
SEL_BLOCK = 64
TOP_N = 16
RNN_WIDTH = 1024
RNN_BLOCK_WIDTH = 64
CONV_WIDTH = 4
LRU_C = 8.0
RMS_EPS = 1e-6
NEG_INF = -1e30
FORCED_SCORE = 1e4
SCALE = HEAD_DIM ** -0.5

LANES = 128
SUBLANES = 8
MXU_DIM = 256
VMEM_LIMIT = 56 * 1024 * 1024

COL_XR = 0
COL_XG = 1024
COL_QA = 2048
COL_QB = 2560
COL_KA = 3072
COL_VA = 3200
COL_KC = 3328
COL_VC = 3456
COL_KS = 3584
COL_VS = 3712
COL_KW = 3840
COL_VW = 3968
COL_GL = 4096
D_PROJ = 4224

_REF_SPLITS = (D_ATT, D_KV, D_KV, D_ATT, D_KV, D_KV, D_KV, D_KV, D_KV, D_KV, 3 * N_HEADS,
               RNN_WIDTH, RNN_WIDTH)


def _params(sem):
    return pltpu.CompilerParams(dimension_semantics=sem, vmem_limit_bytes=VMEM_LIMIT)


def _rms(x, g):
    return x * lax.rsqrt(jnp.mean(x * x, axis=-1, keepdims=True) + RMS_EPS) * g


def _ffn_kernel(x_ref, g_ref, wg_ref, wu_ref, wd_ref, *rest, final_norm):
    if final_norm:
        fg_ref, o_ref, h_ref, acc_ref = rest
    else:
        o_ref, h_ref, acc_ref = rest
    j = pl.program_id(1)

    @pl.when(j == 0)
    def _():
        h_ref[...] = _rms(x_ref[...], g_ref[...]).astype(BF16)
        acc_ref[...] = jnp.zeros_like(acc_ref)

    h = h_ref[...]
    gate = jnp.dot(h, wg_ref[...], preferred_element_type=F32)
    up = jnp.dot(h, wu_ref[...], preferred_element_type=F32)
    act = (gate * jax.nn.sigmoid(gate) * up).astype(BF16)
    acc_ref[...] += jnp.dot(act, wd_ref[...], preferred_element_type=F32)

    @pl.when(j == pl.num_programs(1) - 1)
    def _():
        y = x_ref[...] + 0.5 * acc_ref[...]
        if final_norm:
            y = _rms(y, fg_ref[...])
        o_ref[...] = y


def _ffn(x, g, wg, wu, wd, final_g=None, *, tm=512, tf=512):
    t, d = x.shape
    f = wg.shape[1]
    final_norm = final_g is not None
    in_specs = [
        pl.BlockSpec((tm, d), lambda i, j: (i, 0)),
        pl.BlockSpec((1, d), lambda i, j: (0, 0)),
        pl.BlockSpec((d, tf), lambda i, j: (0, j)),
        pl.BlockSpec((d, tf), lambda i, j: (0, j)),
        pl.BlockSpec((tf, d), lambda i, j: (j, 0)),
    ]
    args = [x, g, wg, wu, wd]
    if final_norm:
        in_specs.append(pl.BlockSpec((1, d), lambda i, j: (0, 0)))
        args.append(final_g)
    return pl.pallas_call(
        functools.partial(_ffn_kernel, final_norm=final_norm),
        grid=(t // tm, f // tf),
        in_specs=in_specs,
        out_specs=pl.BlockSpec((tm, d), lambda i, j: (i, 0)),
        out_shape=jax.ShapeDtypeStruct((t, d), F32),
        scratch_shapes=[pltpu.VMEM((tm, d), BF16), pltpu.VMEM((tm, d), F32)],
        compiler_params=_params(("parallel", "arbitrary")),
        name="ffn",
    )(*args)


def _inproj_kernel(x_ref, g_ref, w_ref, o_ref, h_ref):
    @pl.when(pl.program_id(1) == 0)
    def _():
        h_ref[...] = _rms(x_ref[...], g_ref[...]).astype(BF16)

    o_ref[...] = jnp.dot(h_ref[...], w_ref[...], preferred_element_type=F32)


def _inproj(x, g, w, *, tm=512, tn=1408):
    t, d = x.shape
    n = w.shape[1]
    return pl.pallas_call(
        _inproj_kernel,
        grid=(t // tm, n // tn),
        in_specs=[
            pl.BlockSpec((tm, d), lambda i, j: (i, 0)),
            pl.BlockSpec((1, d), lambda i, j: (0, 0)),
            pl.BlockSpec((d, tn), lambda i, j: (0, j)),
        ],
        out_specs=pl.BlockSpec((tm, tn), lambda i, j: (i, j)),
        out_shape=jax.ShapeDtypeStruct((t, n), F32),
        scratch_shapes=[pltpu.VMEM((tm, d), BF16)],
        compiler_params=_params(("parallel", "arbitrary")),
        name="inproj",
    )(x, g, w)


def _window_kernel(*refs, window, n_prev, use_sink, gate_col):
    refs = list(refs)
    q_ref, k_ref, v_ref = refs[:3]
    rest = refs[3:]
    sink_ref = rest.pop(0) if use_sink else None
    gl_ref = rest.pop(0) if gate_col is not None else None
    (o_ref,) = rest

    blk = ATTN_BLOCK
    span = (n_prev + 1) * blk
    n = pl.program_id(1)
    start = pl.multiple_of(jnp.maximum(n - n_prev, 0) * blk, blk)
    qpos = n * blk + lax.broadcasted_iota(jnp.int32, (blk, span), 0)
    kpos = start + lax.broadcasted_iota(jnp.int32, (blk, span), 1)
    diff = qpos - kpos
    mask = (diff >= 0) & (diff < window)

    for kv in range(N_KV):
        k = k_ref[0, pl.ds(start, span), kv * HEAD_DIM:(kv + 1) * HEAD_DIM].astype(BF16)
        v = v_ref[0, pl.ds(start, span), kv * HEAD_DIM:(kv + 1) * HEAD_DIM].astype(BF16)
        for g in range(GROUP):
            h = kv * GROUP + g
            q = q_ref[0, :, h * HEAD_DIM:(h + 1) * HEAD_DIM].astype(BF16)
            s = lax.dot_general(q, k, (((1,), (1,)), ((), ())),
                                preferred_element_type=F32) * SCALE
            s = jnp.where(mask, s, NEG_INF)
            m = jnp.max(s, axis=-1, keepdims=True)
            if use_sink:
                sk = sink_ref[h]
                m = jnp.maximum(m, sk)
                p = jnp.exp(s - m)
                denom = jnp.sum(p, axis=-1, keepdims=True) + jnp.exp(sk - m)
            else:
                p = jnp.exp(s - m)
                denom = jnp.sum(p, axis=-1, keepdims=True)
            p = p / denom
            o = jnp.dot(p.astype(BF16), v, preferred_element_type=F32)
            if gate_col is not None:
                o = o * jax.nn.sigmoid(gl_ref[0, :, gate_col + h:gate_col + h + 1])
            o_ref[0, :, h * HEAD_DIM:(h + 1) * HEAD_DIM] = o


def _window_attention(proj, q_col, k_col, v_col, window, sinks=None, gate_col=None):
    b, s, _ = proj.shape
    blk = ATTN_BLOCK
    n_prev = -(-(window - 1) // blk)
    in_specs = [
        pl.BlockSpec((1, blk, D_ATT), lambda i, n: (i, n, q_col // D_ATT)),
        pl.BlockSpec((1, s, D_KV), lambda i, n: (i, 0, k_col // D_KV)),
        pl.BlockSpec((1, s, D_KV), lambda i, n: (i, 0, v_col // D_KV)),
    ]
    args = [proj, proj, proj]
    if sinks is not None:
        in_specs.append(pl.BlockSpec(memory_space=pltpu.SMEM))
        args.append(sinks)
    if gate_col is not None:
        in_specs.append(pl.BlockSpec((1, blk, LANES), lambda i, n: (i, n, COL_GL // LANES)))
        args.append(proj)
    return pl.pallas_call(
        functools.partial(_window_kernel, window=window, n_prev=n_prev,
                          use_sink=sinks is not None, gate_col=gate_col),
        grid=(b, s // blk),
        in_specs=in_specs,
        out_specs=pl.BlockSpec((1, blk, D_ATT), lambda i, n: (i, n, 0)),
        out_shape=jax.ShapeDtypeStruct((b, s, D_ATT), F32),
        compiler_params=_params(("parallel", "arbitrary")),
        name=f"window{window}",
    )(*args)


def _compress_kernel(k16_ref, v16_ref, pos_ref, w1_ref, b1_ref, w2_ref, kc_ref, vc_ref):
    half = CMP_STRIDE * HEAD_DIM
    for i, (src_ref, dst_ref) in enumerate(((k16_ref, kc_ref), (v16_ref, vc_ref))):
        for kv in range(N_KV):
            x = src_ref[0, kv]
            lo = (x + pos_ref[i, 0:1, :]).astype(BF16)
            hi = (x + pos_ref[i, 1:2, :]).astype(BF16)
            a = jnp.dot(lo, w1_ref[i, 0:half, :], preferred_element_type=F32)
            bm = jnp.dot(hi, w1_ref[i, half:2 * half, :], preferred_element_type=F32)
            rows = bm.shape[0]
            hid = a + pltpu.roll(bm, rows - 1, 0) + b1_ref[i]
            act = jax.nn.gelu(hid, approximate=True).astype(BF16)
            dst_ref[0, kv] = jnp.dot(act, w2_ref[i], preferred_element_type=F32)


def _compress(k16, v16, pos, w1, b1, w2):
    b = k16.shape[0]
    nrow = k16.shape[2]
    spec16 = pl.BlockSpec((1, N_KV, nrow, CMP_STRIDE * HEAD_DIM), lambda i: (i, 0, 0, 0))
    out_spec = pl.BlockSpec((1, N_KV, nrow, HEAD_DIM), lambda i: (i, 0, 0, 0))
    out_shape = jax.ShapeDtypeStruct((b, N_KV, nrow, HEAD_DIM), F32)
    return pl.pallas_call(
        _compress_kernel,
        grid=(b,),
        in_specs=[
            spec16, spec16,
            pl.BlockSpec(pos.shape, lambda i: (0, 0, 0)),
            pl.BlockSpec(w1.shape, lambda i: (0, 0, 0)),
            pl.BlockSpec(b1.shape, lambda i: (0, 0, 0)),
            pl.BlockSpec(w2.shape, lambda i: (0, 0, 0)),
        ],
        out_specs=[out_spec, out_spec],
        out_shape=[out_shape, out_shape],
        compiler_params=_params(("parallel",)),
        name="compress",
    )(k16, v16, pos, w1, b1, w2)


def _cmp_attn_kernel(q_ref, kc_ref, vc_ref, gl_ref, ov_ref, o_ref, sel_ref, *, tq, n_sel):
    n = pl.program_id(1)
    ncp = kc_ref.shape[2]
    pos_c = n * tq + lax.broadcasted_iota(jnp.int32, (tq, ncp), 0)
    cend = lax.broadcasted_iota(jnp.int32, (tq, ncp), 1) * CMP_STRIDE + (CMP_LEN - 1)
    cmask = cend <= pos_c

    pos_s = n * tq + lax.broadcasted_iota(jnp.int32, (tq, n_sel), 0)
    bid = lax.broadcasted_iota(jnp.int32, (tq, n_sel), 1)
    cur = pos_s // SEL_BLOCK
    forced = (bid == 0) | (bid == cur) | (bid == cur - 1)
    valid = bid * SEL_BLOCK <= pos_s

    for kv in range(N_KV):
        kc = kc_ref[0, kv].astype(BF16)
        vc = vc_ref[0, kv].astype(BF16)
        psum = jnp.zeros((tq, ncp), F32)
        for g in range(GROUP):
            h = kv * GROUP + g
            q = q_ref[0, :, h * HEAD_DIM:(h + 1) * HEAD_DIM].astype(BF16)
            s = lax.dot_general(q, kc, (((1,), (1,)), ((), ())),
                                preferred_element_type=F32) * SCALE
            s = jnp.where(cmask, s, NEG_INF)
            m = jnp.max(s, axis=-1, keepdims=True)
            p = jnp.exp(s - m)
            p = p / jnp.sum(p, axis=-1, keepdims=True)
            p = jnp.where(cmask, p, 0.0)
            psum = psum + p
            o = jnp.dot(p.astype(BF16), vc, preferred_element_type=F32)
            o = o * jax.nn.sigmoid(gl_ref[0, :, h:h + 1])
            o_ref[0, :, h * HEAD_DIM:(h + 1) * HEAD_DIM] = o

        imp = jnp.dot(psum, ov_ref[...], preferred_element_type=F32,
                      precision=lax.Precision.HIGHEST)
        imp = jnp.where(forced, FORCED_SCORE, imp)
        imp = jnp.where(valid, imp, NEG_INF)
        rank = jnp.zeros((tq, n_sel), jnp.int32)
        for i in range(n_sel):
            col = imp[:, i:i + 1]
            ahead = (col > imp) | ((col == imp) & (bid > i))
            rank = rank + ahead.astype(jnp.int32)
        sel_ref[0, :, kv * n_sel:(kv + 1) * n_sel] = (rank < TOP_N).astype(F32)


def _cmp_attention(proj, kc, vc, overlap, *, tq=256):
    b, s, _ = proj.shape
    n_sel = overlap.shape[1]
    ncp = kc.shape[2]
    return pl.pallas_call(
        functools.partial(_cmp_attn_kernel, tq=tq, n_sel=n_sel),
        grid=(b, s // tq),
        in_specs=[
            pl.BlockSpec((1, tq, D_ATT), lambda i, n: (i, n, COL_QB // D_ATT)),
            pl.BlockSpec((1, N_KV, ncp, HEAD_DIM), lambda i, n: (i, 0, 0, 0)),
            pl.BlockSpec((1, N_KV, ncp, HEAD_DIM), lambda i, n: (i, 0, 0, 0)),
            pl.BlockSpec((1, tq, LANES), lambda i, n: (i, n, COL_GL // LANES)),
            pl.BlockSpec(overlap.shape, lambda i, n: (0, 0)),
        ],
        out_specs=[
            pl.BlockSpec((1, tq, D_ATT), lambda i, n: (i, n, 0)),
            pl.BlockSpec((1, tq, N_KV * n_sel), lambda i, n: (i, n, 0)),
        ],
        out_shape=[
            jax.ShapeDtypeStruct((b, s, D_ATT), F32),
            jax.ShapeDtypeStruct((b, s, N_KV * n_sel), F32),
        ],
        compiler_params=_params(("parallel", "arbitrary")),
        name="cmp_attn",
    )(proj, kc, vc, proj, overlap)


def _sel_attn_kernel(q_ref, k_ref, v_ref, sel_ref, gl_ref, o_ref, *, tq, tk, n_sel, gate_col):
    n = pl.program_id(1)
    qpos = n * tq + lax.broadcasted_iota(jnp.int32, (tq, tk), 0)
    kcol = lax.broadcasted_iota(jnp.int32, (tq, tk), 1)
    e_row = lax.broadcasted_iota(jnp.int32, (n_sel, tk), 0)
    e_col = lax.broadcasted_iota(jnp.int32, (n_sel, tk), 1)
    n_tiles = (n * tq) // tk + 1

    for kv in range(N_KV):
        sel = sel_ref[0, :, kv * n_sel:(kv + 1) * n_sel].astype(BF16)
        for g in range(GROUP):
            h = kv * GROUP + g
            q = q_ref[0, :, h * HEAD_DIM:(h + 1) * HEAD_DIM].astype(BF16)

            def body(kt, carry):
                m, l, acc = carry
                k0 = pl.multiple_of(kt * tk, tk)
                k = k_ref[0, pl.ds(k0, tk), kv * HEAD_DIM:(kv + 1) * HEAD_DIM].astype(BF16)
                v = v_ref[0, pl.ds(k0, tk), kv * HEAD_DIM:(kv + 1) * HEAD_DIM].astype(BF16)
                expand = ((k0 + e_col) // SEL_BLOCK == e_row).astype(BF16)
                chosen = jnp.dot(sel, expand, preferred_element_type=F32) > 0.5
                mask = chosen & (k0 + kcol <= qpos)
                s = lax.dot_general(q, k, (((1,), (1,)), ((), ())),
                                    preferred_element_type=F32) * SCALE
                s = jnp.where(mask, s, NEG_INF)
                m_new = jnp.maximum(m, jnp.max(s, axis=-1, keepdims=True))
                alpha = jnp.exp(m - m_new)
                p = jnp.exp(s - m_new)
                l = alpha * l + jnp.sum(p, axis=-1, keepdims=True)
                acc = alpha * acc + jnp.dot(p.astype(BF16), v, preferred_element_type=F32)
                return m_new, l, acc

            init = (jnp.full((tq, 1), NEG_INF, F32), jnp.zeros((tq, 1), F32),
                    jnp.zeros((tq, HEAD_DIM), F32))
            _, l, acc = lax.fori_loop(0, n_tiles, body, init)
            o = acc / l
            o = o * jax.nn.sigmoid(gl_ref[0, :, gate_col + h:gate_col + h + 1])
            o_ref[0, :, h * HEAD_DIM:(h + 1) * HEAD_DIM] = o


def _sel_attention(proj, sel, *, tq=128, tk=128):
    b, s, _ = proj.shape
    n_sel = sel.shape[2] // N_KV
    return pl.pallas_call(
        functools.partial(_sel_attn_kernel, tq=tq, tk=tk, n_sel=n_sel, gate_col=N_HEADS),
        grid=(b, s // tq),
        in_specs=[
            pl.BlockSpec((1, tq, D_ATT), lambda i, n: (i, n, COL_QB // D_ATT)),
            pl.BlockSpec((1, s, D_KV), lambda i, n: (i, 0, COL_KS // D_KV)),
            pl.BlockSpec((1, s, D_KV), lambda i, n: (i, 0, COL_VS // D_KV)),
            pl.BlockSpec((1, tq, N_KV * n_sel), lambda i, n: (i, n, 0)),
            pl.BlockSpec((1, tq, LANES), lambda i, n: (i, n, COL_GL // LANES)),
        ],
        out_specs=pl.BlockSpec((1, tq, D_ATT), lambda i, n: (i, n, 0)),
        out_shape=jax.ShapeDtypeStruct((b, s, D_ATT), F32),
        compiler_params=_params(("parallel", "arbitrary")),
        name="sel_attn",
    )(proj, proj, proj, sel, proj)


def _rglru_kernel(xr_ref, xg_ref, cw_ref, cb_ref, wa_ref, ba_ref, wx_ref, bx_ref, lam_ref,
                  o_ref, xbuf_ref, h_ref, *, ts):
    pad = SUBLANES

    @pl.when(pl.program_id(1) == 0)
    def _():
        xbuf_ref[0:pad, :] = jnp.zeros((pad, RNN_WIDTH), F32)
        h_ref[...] = jnp.zeros_like(h_ref)

    xr = xr_ref[0]
    xbuf_ref[pad:pad + ts, :] = xr
    xc = cb_ref[...] + cw_ref[CONV_WIDTH - 1:CONV_WIDTH, :] * xr
    for w in range(CONV_WIDTH - 1):
        shift = CONV_WIDTH - 1 - w
        xc = xc + cw_ref[w:w + 1, :] * xbuf_ref[pl.ds(pad - shift, ts), :]
    xbuf_ref[0:pad, :] = xr[ts - pad:ts, :]

    ra, rx = [], []
    for blk in range(RNN_WIDTH // MXU_DIM):
        xs = xc[:, blk * MXU_DIM:(blk + 1) * MXU_DIM].astype(BF16)
        ra.append(jnp.dot(xs, wa_ref[blk], preferred_element_type=F32))
        rx.append(jnp.dot(xs, wx_ref[blk], preferred_element_type=F32))
    r = jax.nn.sigmoid(jnp.concatenate(ra, axis=1) + ba_ref[...])
    gi = jax.nn.sigmoid(jnp.concatenate(rx, axis=1) + bx_ref[...])
    nl = -lam_ref[...]
    softplus = jnp.maximum(nl, 0.0) + jnp.log1p(jnp.exp(-jnp.abs(nl)))
    log_a = -LRU_C * r * softplus
    a = jnp.exp(log_a)
    u = jnp.sqrt(-jnp.tanh(log_a) * (a * a + 1.0)) * (gi * xc)

    row = lax.broadcasted_iota(jnp.int32, (ts, RNN_WIDTH), 0)
    d = 1
    while d < ts:
        keep = row >= d
        a_prev = jnp.where(keep, pltpu.roll(a, d, 0), 1.0)
        u_prev = jnp.where(keep, pltpu.roll(u, d, 0), 0.0)
        u = a * u_prev + u
        a = a * a_prev
        d *= 2
    h = a * h_ref[0:1, :] + u
    h_ref[0:1, :] = h[ts - 1:ts, :]
    o_ref[0] = jax.nn.gelu(xg_ref[0], approximate=True) * h


def _rglru(proj, conv_w, conv_b, wa, ba, wx, bx, lam, *, ts=256):
    b, s, _ = proj.shape
    c = RNN_WIDTH
    vec = pl.BlockSpec((1, c), lambda i, n: (0, 0))
    wspec = pl.BlockSpec(wa.shape, lambda i, n: (0, 0, 0))
    return pl.pallas_call(
        functools.partial(_rglru_kernel, ts=ts),
        grid=(b, s // ts),
        in_specs=[
            pl.BlockSpec((1, ts, c), lambda i, n: (i, n, COL_XR // c)),
            pl.BlockSpec((1, ts, c), lambda i, n: (i, n, COL_XG // c)),
            pl.BlockSpec((CONV_WIDTH, c), lambda i, n: (0, 0)),
            vec, wspec, vec, wspec, vec, vec,
        ],
        out_specs=pl.BlockSpec((1, ts, c), lambda i, n: (i, n, 0)),
        out_shape=jax.ShapeDtypeStruct((b, s, c), F32),
        scratch_shapes=[pltpu.VMEM((ts + SUBLANES, c), F32), pltpu.VMEM((SUBLANES, c), F32)],
        compiler_params=_params(("parallel", "arbitrary")),
        name="rglru",
    )(proj, proj, conv_w, conv_b, wa, ba, wx, bx, lam)


def _outproj_kernel(x_ref, ya_ref, oc_ref, os_ref, ow_ref, yc_ref, gn_ref, w_ref, o_ref):
    ya = _rms(ya_ref[...], gn_ref[:, 0:D_ATT]).astype(BF16)
    yb = oc_ref[...] + os_ref[...] + ow_ref[...]
    yb = _rms(yb, gn_ref[:, D_ATT:2 * D_ATT]).astype(BF16)
    yc = _rms(yc_ref[...], gn_ref[:, 2 * D_ATT:]).astype(BF16)
    y = jnp.dot(ya, w_ref[0:D_ATT, :], preferred_element_type=F32)
    y = y + jnp.dot(yb, w_ref[D_ATT:2 * D_ATT, :], preferred_element_type=F32)
    y = y + jnp.dot(yc, w_ref[2 * D_ATT:, :], preferred_element_type=F32)
    o_ref[...] = x_ref[...] + y


def _outproj(x, ya, oc, osl, ow, yc, gn, w, *, tm=256):
    t, d = x.shape
    att = pl.BlockSpec((tm, D_ATT), lambda i: (i, 0))
    return pl.pallas_call(
        _outproj_kernel,
        grid=(t // tm,),
        in_specs=[
            pl.BlockSpec((tm, d), lambda i: (i, 0)),
            att, att, att, att,
            pl.BlockSpec((tm, RNN_WIDTH), lambda i: (i, 0)),
            pl.BlockSpec((1, gn.shape[1]), lambda i: (0, 0)),
            pl.BlockSpec(w.shape, lambda i: (0, 0)),
        ],
        out_specs=pl.BlockSpec((tm, d), lambda i: (i, 0)),
        out_shape=jax.ShapeDtypeStruct((t, d), F32),
        compiler_params=_params(("parallel",)),
        name="outproj",
    )(x, ya, oc, osl, ow, yc, gn, w)


def _reorder_w_in(w):
    offs = np.concatenate([[0], np.cumsum(_REF_SPLITS)])
    qa, ka, va, qb, kc, vc, ks, vs, kw, vw, gl, xr, xg = [
        w[:, int(offs[i]):int(offs[i + 1])] for i in range(len(_REF_SPLITS))]
    gl = jnp.pad(gl, ((0, 0), (0, LANES - gl.shape[1])))
    return jnp.concatenate([xr, xg, qa, qb, ka, va, kc, vc, ks, vs, kw, vw, gl], axis=1)


def _block_diag(w):
    per = MXU_DIM // RNN_BLOCK_WIDTH
    nt = w.shape[0] // per
    w = w.reshape(nt, per, RNN_BLOCK_WIDTH, RNN_BLOCK_WIDTH)
    eye = jnp.eye(per, dtype=w.dtype)
    out = jnp.einsum('tpcd,pq->tpcqd', w, eye)
    return out.reshape(nt, MXU_DIM, MXU_DIM)


def _overlap_matrix(seq, n_cmp_padded):
    n_c = (seq - CMP_LEN) // CMP_STRIDE + 1
    n_sel = seq // SEL_BLOCK
    cs = np.arange(n_c)[:, None] * CMP_STRIDE
    ss = np.arange(n_sel)[None, :] * SEL_BLOCK
    ov = np.clip(np.minimum(cs + CMP_LEN, ss + SEL_BLOCK) - np.maximum(cs, ss), 0, None)
    out = np.zeros((n_cmp_padded, n_sel), np.float32)
    out[:n_c] = ov / CMP_LEN
    return jnp.asarray(out)


def _to_cmp_rows(t, b, s):
    t = t.reshape(b, s // CMP_STRIDE, CMP_STRIDE, N_KV, HEAD_DIM)
    return t.transpose(0, 3, 1, 2, 4).reshape(b, N_KV, s // CMP_STRIDE, CMP_STRIDE * HEAD_DIM)


def kernel(x, ffn1_norm, ffn1_w_gate, ffn1_w_up, ffn1_w_down, mix_norm, w_in, swa_sinks, cmp_pos, cmp_w1, cmp_b1, cmp_w2, conv_w, conv_b, lru_wa, lru_ba, lru_wx, lru_bx, lru_lambda, group_norm, w_out, ffn2_norm, ffn2_w_gate, ffn2_w_up, ffn2_w_down, final_norm):
    b, s, d = x.shape
    depth = w_in.shape[0]
    t = b * s
    overlap = _overlap_matrix(s, s // CMP_STRIDE)
    xt = x.reshape(t, d)
    for l in range(depth):
        xt = _ffn(xt, ffn1_norm[l][None], ffn1_w_gate[l].astype(BF16),
                  ffn1_w_up[l].astype(BF16), ffn1_w_down[l].astype(BF16))

        proj = _inproj(xt, mix_norm[l][None], _reorder_w_in(w_in[l]).astype(BF16))
        proj = proj.reshape(b, s, D_PROJ)

        ya = _window_attention(proj, COL_QA, COL_KA, COL_VA, SWA_WINDOW, sinks=swa_sinks[l])

        k16 = _to_cmp_rows(proj[:, :, COL_KC:COL_KC + D_KV], b, s)
        v16 = _to_cmp_rows(proj[:, :, COL_VC:COL_VC + D_KV], b, s)
        pos = cmp_pos[l].reshape(2, 2, CMP_STRIDE * HEAD_DIM)
        kc, vc = _compress(k16, v16, pos, cmp_w1[l].astype(BF16), cmp_b1[l][:, None, :],
                           cmp_w2[l].astype(BF16))
        o_cmp, sel = _cmp_attention(proj, kc, vc, overlap)
        o_slc = _sel_attention(proj, sel)
        o_win = _window_attention(proj, COL_QB, COL_KW, COL_VW, NSA_WINDOW,
                                  gate_col=2 * N_HEADS)

        yc = _rglru(proj, conv_w[l], conv_b[l][None], _block_diag(lru_wa[l]).astype(BF16),
                    lru_ba[l][None], _block_diag(lru_wx[l]).astype(BF16), lru_bx[l][None],
                    lru_lambda[l][None])

        xt = _outproj(xt, ya.reshape(t, D_ATT), o_cmp.reshape(t, D_ATT),
                      o_slc.reshape(t, D_ATT), o_win.reshape(t, D_ATT),
                      yc.reshape(t, RNN_WIDTH), group_norm[l][None], w_out[l].astype(BF16))

        last = l == depth - 1
        xt = _ffn(xt, ffn2_norm[l][None], ffn2_w_gate[l].astype(BF16),
                  ffn2_w_up[l].astype(BF16), ffn2_w_down[l].astype(BF16),
                  final_g=final_norm[None] if last else None)
    return xt.reshape(b, s, d)
```

```python
import functools

import jax
import jax.numpy as jnp
import numpy as np
from jax import lax
from jax.experimental import pallas as pl
from jax.experimental.pallas import tpu as pltpu

F32 = jnp.float32
BF16 = jnp.bfloat16

D_MODEL = 2048
HEAD_DIM = 64
D_FF = 5632
N_HEADS = 8
N_KV = 2
GROUP = N_HEADS // N_KV
D_ATT = N_HEADS * HEAD_DIM
D_KV = N_KV * HEAD_DIM
SWA_WINDOW = 128
NSA_WINDOW = 512
ATTN_BLOCK = 128
CMP_LEN = 32
CMP_STRIDE = 16
CMP_HIDDEN = 256
SEL_BLOCK = 64
TOP_N = 16
RNN_WIDTH = 1024
RNN_BLOCK_WIDTH = 64
CONV_WIDTH = 4
LRU_C = 8.0
RMS_EPS = 1e-6
NEG_INF = -1e30
FORCED_SCORE = 1e4
SCALE = HEAD_DIM ** -0.5

LANES = 128
SUBLANES = 8
MXU_DIM = 256
VMEM_LIMIT = 56 * 1024 * 1024

COL_XR = 0
COL_XG = 1024
COL_QA = 2048
COL_QB = 2560
COL_KA = 3072
COL_VA = 3200
COL_KC = 3328
COL_VC = 3456
COL_KS = 3584
COL_VS = 3712
COL_KW = 3840
COL_VW = 3968
COL_GL = 4096
D_PROJ = 4224

_REF_SPLITS = (D_ATT, D_KV, D_KV, D_ATT, D_KV, D_KV, D_KV, D_KV, D_KV, D_KV, 3 * N_HEADS,
               RNN_WIDTH, RNN_WIDTH)


def _params(sem):
    return pltpu.CompilerParams(dimension_semantics=sem, vmem_limit_bytes=VMEM_LIMIT)


def _rms(x, g):
    return x * lax.rsqrt(jnp.mean(x * x, axis=-1, keepdims=True) + RMS_EPS) * g


def _ffn_kernel(x_ref, g_ref, wg_ref, wu_ref, wd_ref, *rest, final_norm):
    if final_norm:
        fg_ref, o_ref, h_ref, acc_ref = rest
    else:
        o_ref, h_ref, acc_ref = rest
    j = pl.program_id(1)

    @pl.when(j == 0)
    def _():
        h_ref[...] = _rms(x_ref[...], g_ref[...]).astype(BF16)
        acc_ref[...] = jnp.zeros_like(acc_ref)

    h = h_ref[...]
    gate = jnp.dot(h, wg_ref[...], preferred_element_type=F32)
    up = jnp.dot(h, wu_ref[...], preferred_element_type=F32)
    act = (gate * jax.nn.sigmoid(gate) * up).astype(BF16)
    acc_ref[...] += jnp.dot(act, wd_ref[...], preferred_element_type=F32)

    @pl.when(j == pl.num_programs(1) - 1)
    def _():
        y = x_ref[...] + 0.5 * acc_ref[...]
        if final_norm:
            y = _rms(y, fg_ref[...])
        o_ref[...] = y


def _ffn(x, g, wg, wu, wd, final_g=None, *, tm=512, tf=512):
    t, d = x.shape
    f = wg.shape[1]
    final_norm = final_g is not None
    in_specs = [
        pl.BlockSpec((tm, d), lambda i, j: (i, 0)),
        pl.BlockSpec((1, d), lambda i, j: (0, 0)),
        pl.BlockSpec((d, tf), lambda i, j: (0, j)),
        pl.BlockSpec((d, tf), lambda i, j: (0, j)),
        pl.BlockSpec((tf, d), lambda i, j: (j, 0)),
    ]
    args = [x, g, wg, wu, wd]
    if final_norm:
        in_specs.append(pl.BlockSpec((1, d), lambda i, j: (0, 0)))
        args.append(final_g)
    return pl.pallas_call(
        functools.partial(_ffn_kernel, final_norm=final_norm),
        grid=(t // tm, f // tf),
        in_specs=in_specs,
        out_specs=pl.BlockSpec((tm, d), lambda i, j: (i, 0)),
        out_shape=jax.ShapeDtypeStruct((t, d), F32),
        scratch_shapes=[pltpu.VMEM((tm, d), BF16), pltpu.VMEM((tm, d), F32)],
        compiler_params=_params(("parallel", "arbitrary")),
        name="ffn",
    )(*args)


def _inproj_kernel(x_ref, g_ref, w_ref, o_ref, h_ref):
    @pl.when(pl.program_id(1) == 0)
    def _():
        h_ref[...] = _rms(x_ref[...], g_ref[...]).astype(BF16)

    o_ref[...] = jnp.dot(h_ref[...], w_ref[...], preferred_element_type=F32)


def _inproj(x, g, w, *, tm=512, tn=1408):
    t, d = x.shape
    n = w.shape[1]
    return pl.pallas_call(
        _inproj_kernel,
        grid=(t // tm, n // tn),
        in_specs=[
            pl.BlockSpec((tm, d), lambda i, j: (i, 0)),
            pl.BlockSpec((1, d), lambda i, j: (0, 0)),
            pl.BlockSpec((d, tn), lambda i, j: (0, j)),
        ],
        out_specs=pl.BlockSpec((tm, tn), lambda i, j: (i, j)),
        out_shape=jax.ShapeDtypeStruct((t, n), F32),
        scratch_shapes=[pltpu.VMEM((tm, d), BF16)],
        compiler_params=_params(("parallel", "arbitrary")),
        name="inproj",
    )(x, g, w)


def _stack_heads(q_ref, kv):
    heads = [q_ref[0, :, (kv * GROUP + g) * HEAD_DIM:(kv * GROUP + g + 1) * HEAD_DIM]
             for g in range(GROUP)]
    return (jnp.concatenate(heads, axis=0) * SCALE).astype(BF16)


def _window_kernel(*refs, window, n_prev, use_sink, gate_col):
    refs = list(refs)
    q_ref, k_ref, v_ref = refs[:3]
    rest = refs[3:]
    sink_ref = rest.pop(0) if use_sink else None
    gl_ref = rest.pop(0) if gate_col is not None else None
    o_ref, kb_ref, vb_ref = rest

    blk = ATTN_BLOCK
    rows = GROUP * blk
    span = (n_prev + 1) * blk
    n = pl.program_id(1)

    @pl.when(n == 0)
    def _():
        for kv in range(N_KV):
            kb_ref[kv] = k_ref[0, :, kv * HEAD_DIM:(kv + 1) * HEAD_DIM].astype(BF16)
            vb_ref[kv] = v_ref[0, :, kv * HEAD_DIM:(kv + 1) * HEAD_DIM].astype(BF16)

    start = pl.multiple_of(jnp.maximum(n - n_prev, 0) * blk, blk)
    qpos = n * blk + (lax.broadcasted_iota(jnp.int32, (rows, span), 0) & (blk - 1))
    kpos = start + lax.broadcasted_iota(jnp.int32, (rows, span), 1)
    diff = qpos - kpos
    mask = (diff >= 0) & (diff < window)

    for kv in range(N_KV):
        q = _stack_heads(q_ref, kv)
        k = kb_ref[kv, pl.ds(start, span), :]
        v = vb_ref[kv, pl.ds(start, span), :]
        s = lax.dot_general(q, k, (((1,), (1,)), ((), ())), preferred_element_type=F32)
        s = jnp.where(mask, s, NEG_INF)
        m = jnp.max(s, axis=-1, keepdims=True)
        if use_sink:
            sk = jnp.concatenate([jnp.full((blk, 1), sink_ref[kv * GROUP + g], F32)
                                  for g in range(GROUP)], axis=0)
            m = jnp.maximum(m, sk)
            p = jnp.exp(s - m)
            denom = jnp.sum(p, axis=-1, keepdims=True) + jnp.exp(sk - m)
        else:
            p = jnp.exp(s - m)
            denom = jnp.sum(p, axis=-1, keepdims=True)
        o = jnp.dot(p.astype(BF16), v, preferred_element_type=F32) * (1.0 / denom)
        for g in range(GROUP):
            h = kv * GROUP + g
            oh = o[g * blk:(g + 1) * blk, :]
            if gate_col is not None:
                oh = oh * jax.nn.sigmoid(gl_ref[0, :, gate_col + h:gate_col + h + 1])
            o_ref[0, :, h * HEAD_DIM:(h + 1) * HEAD_DIM] = oh


def _window_attention(proj, q_col, k_col, v_col, window, sinks=None, gate_col=None):
    b, s, _ = proj.shape
    blk = ATTN_BLOCK
    n_prev = -(-(window - 1) // blk)
    in_specs = [
        pl.BlockSpec((1, blk, D_ATT), lambda i, n: (i, n, q_col // D_ATT)),
        pl.BlockSpec((1, s, D_KV), lambda i, n: (i, 0, k_col // D_KV)),
        pl.BlockSpec((1, s, D_KV), lambda i, n: (i, 0, v_col // D_KV)),
    ]
    args = [proj, proj, proj]
    if sinks is not None:
        in_specs.append(pl.BlockSpec(memory_space=pltpu.SMEM))
        args.append(sinks)
    if gate_col is not None:
        in_specs.append(pl.BlockSpec((1, blk, LANES), lambda i, n: (i, n, COL_GL // LANES)))
        args.append(proj)
    return pl.pallas_call(
        functools.partial(_window_kernel, window=window, n_prev=n_prev,
                          use_sink=sinks is not None, gate_col=gate_col),
        grid=(b, s // blk),
        in_specs=in_specs,
        out_specs=pl.BlockSpec((1, blk, D_ATT), lambda i, n: (i, n, 0)),
        out_shape=jax.ShapeDtypeStruct((b, s, D_ATT), F32),
        scratch_shapes=[pltpu.VMEM((N_KV, s, HEAD_DIM), BF16),
                        pltpu.VMEM((N_KV, s, HEAD_DIM), BF16)],
        compiler_params=_params(("parallel", "arbitrary")),
        name=f"window{window}",
    )(*args)


def _compress_kernel(k16_ref, v16_ref, pos_ref, w1_ref, b1_ref, w2_ref, kc_ref, vc_ref):
    half = CMP_STRIDE * HEAD_DIM
    for i, (src_ref, dst_ref) in enumerate(((k16_ref, kc_ref), (v16_ref, vc_ref))):
        for kv in range(N_KV):
            x = src_ref[0, kv]
            lo = (x + pos_ref[i, 0:1, :]).astype(BF16)
            hi = (x + pos_ref[i, 1:2, :]).astype(BF16)
            a = jnp.dot(lo, w1_ref[i, 0:half, :], preferred_element_type=F32)
            bm = jnp.dot(hi, w1_ref[i, half:2 * half, :], preferred_element_type=F32)
            rows = bm.shape[0]
            hid = a + pltpu.roll(bm, rows - 1, 0) + b1_ref[i]
            act = jax.nn.gelu(hid, approximate=True).astype(BF16)
            dst_ref[0, kv] = jnp.dot(act, w2_ref[i], preferred_element_type=F32)


def _compress(k16, v16, pos, w1, b1, w2):
    b = k16.shape[0]
    nrow = k16.shape[2]
    spec16 = pl.BlockSpec((1, N_KV, nrow, CMP_STRIDE * HEAD_DIM), lambda i: (i, 0, 0, 0))
    out_spec = pl.BlockSpec((1, N_KV, nrow, HEAD_DIM), lambda i: (i, 0, 0, 0))
    out_shape = jax.ShapeDtypeStruct((b, N_KV, nrow, HEAD_DIM), F32)
    return pl.pallas_call(
        _compress_kernel,
        grid=(b,),
        in_specs=[
            spec16, spec16,
            pl.BlockSpec(pos.shape, lambda i: (0, 0, 0)),
            pl.BlockSpec(w1.shape, lambda i: (0, 0, 0)),
            pl.BlockSpec(b1.shape, lambda i: (0, 0, 0)),
            pl.BlockSpec(w2.shape, lambda i: (0, 0, 0)),
        ],
        out_specs=[out_spec, out_spec],
        out_shape=[out_shape, out_shape],
        compiler_params=_params(("parallel",)),
        name="compress",
    )(k16, v16, pos, w1, b1, w2)


def _cmp_attn_kernel(q_ref, kc_ref, vc_ref, gl_ref, ov_ref, o_ref, sel_ref, *, tq, n_sel):
    n = pl.program_id(1)
    ncp = kc_ref.shape[2]
    rows = GROUP * tq
    pos_c = n * tq + (lax.broadcasted_iota(jnp.int32, (rows, ncp), 0) & (tq - 1))
    cend = lax.broadcasted_iota(jnp.int32, (rows, ncp), 1) * CMP_STRIDE + (CMP_LEN - 1)
    cmask = cend <= pos_c

    pos_s = n * tq + lax.broadcasted_iota(jnp.int32, (tq, n_sel), 0)
    bid = lax.broadcasted_iota(jnp.int32, (tq, n_sel), 1)
    cur = pos_s // SEL_BLOCK
    forced = (bid == 0) | (bid == cur) | (bid == cur - 1)
    valid = bid * SEL_BLOCK <= pos_s

    for kv in range(N_KV):
        kc = kc_ref[0, kv].astype(BF16)
        vc = vc_ref[0, kv].astype(BF16)
        q = _stack_heads(q_ref, kv)
        s = lax.dot_general(q, kc, (((1,), (1,)), ((), ())), preferred_element_type=F32)
        s = jnp.where(cmask, s, NEG_INF)
        m = jnp.max(s, axis=-1, keepdims=True)
        p = jnp.exp(s - m)
        p = p * (1.0 / jnp.sum(p, axis=-1, keepdims=True))
        p = jnp.where(cmask, p, 0.0)
        o = jnp.dot(p.astype(BF16), vc, preferred_element_type=F32)
        psum = p[0:tq]
        for g in range(GROUP):
            h = kv * GROUP + g
            if g:
                psum = psum + p[g * tq:(g + 1) * tq]
            oh = o[g * tq:(g + 1) * tq, :] * jax.nn.sigmoid(gl_ref[0, :, h:h + 1])
            o_ref[0, :, h * HEAD_DIM:(h + 1) * HEAD_DIM] = oh

        imp = jnp.dot(psum, ov_ref[...], preferred_element_type=F32,
                      precision=lax.Precision.HIGHEST)
        imp = jnp.where(forced, FORCED_SCORE, imp)
        imp = jnp.where(valid, imp, NEG_INF)
        rank = jnp.zeros((tq, n_sel), jnp.int32)
        for i in range(n_sel):
            col = imp[:, i:i + 1]
            ahead = (col > imp) | ((col == imp) & (bid > i))
            rank = rank + ahead.astype(jnp.int32)
        sel_ref[0, :, kv * n_sel:(kv + 1) * n_sel] = jnp.where(rank < TOP_N, 0.0, NEG_INF)


def _cmp_attention(proj, kc, vc, overlap, *, tq=256):
    b, s, _ = proj.shape
    n_sel = overlap.shape[1]
    ncp = kc.shape[2]
    return pl.pallas_call(
        functools.partial(_cmp_attn_kernel, tq=tq, n_sel=n_sel),
        grid=(b, s // tq),
        in_specs=[
            pl.BlockSpec((1, tq, D_ATT), lambda i, n: (i, n, COL_QB // D_ATT)),
            pl.BlockSpec((1, N_KV, ncp, HEAD_DIM), lambda i, n: (i, 0, 0, 0)),
            pl.BlockSpec((1, N_KV, ncp, HEAD_DIM), lambda i, n: (i, 0, 0, 0)),
            pl.BlockSpec((1, tq, LANES), lambda i, n: (i, n, COL_GL // LANES)),
            pl.BlockSpec(overlap.shape, lambda i, n: (0, 0)),
        ],
        out_specs=[
            pl.BlockSpec((1, tq, D_ATT), lambda i, n: (i, n, 0)),
            pl.BlockSpec((1, tq, N_KV * n_sel), lambda i, n: (i, n, 0)),
        ],
        out_shape=[
            jax.ShapeDtypeStruct((b, s, D_ATT), F32),
            jax.ShapeDtypeStruct((b, s, N_KV * n_sel), F32),
        ],
        compiler_params=_params(("parallel", "arbitrary")),
        name="cmp_attn",
    )(proj, kc, vc, proj, overlap)


def _sel_attn_kernel(q_ref, k_ref, v_ref, bias_ref, gl_ref, o_ref,
                     qa_ref, ka_ref, vb_ref, m_ref, l_ref, acc_ref, *, tq, tk, n_sel, gate_col):
    n = pl.program_id(1)
    s_len = k_ref.shape[1]
    rows = GROUP * tq
    n_pad = LANES - HEAD_DIM - n_sel

    @pl.when(n == 0)
    def _():
        kpos = lax.broadcasted_iota(jnp.int32, (s_len, n_sel), 0)
        blk = lax.broadcasted_iota(jnp.int32, (s_len, n_sel), 1)
        onehot = (kpos // SEL_BLOCK == blk).astype(F32)
        zeros = jnp.zeros((s_len, n_pad), F32)
        for kv in range(N_KV):
            k = k_ref[0, :, kv * HEAD_DIM:(kv + 1) * HEAD_DIM]
            ka_ref[kv] = jnp.concatenate([k, onehot, zeros], axis=1).astype(BF16)
            vb_ref[kv] = v_ref[0, :, kv * HEAD_DIM:(kv + 1) * HEAD_DIM].astype(BF16)

    zq = jnp.zeros((tq, n_pad), F32)
    for kv in range(N_KV):
        bias = bias_ref[0, :, kv * n_sel:(kv + 1) * n_sel]
        for g in range(GROUP):
            h = kv * GROUP + g
            q = q_ref[0, :, h * HEAD_DIM:(h + 1) * HEAD_DIM] * SCALE
            qa_ref[h * tq:(h + 1) * tq, :] = jnp.concatenate([q, bias, zq], axis=1).astype(BF16)
    m_ref[...] = jnp.full(m_ref.shape, NEG_INF, F32)
    l_ref[...] = jnp.zeros(l_ref.shape, F32)
    acc_ref[...] = jnp.zeros(acc_ref.shape, F32)

    qpos = n * tq + (lax.broadcasted_iota(jnp.int32, (rows, tk), 0) & (tq - 1))
    kcol = lax.broadcasted_iota(jnp.int32, (rows, tk), 1)

    def step(kt, causal):
        k0 = pl.multiple_of(kt * tk, tk)
        for kv in range(N_KV):
            r = slice(kv * rows, (kv + 1) * rows)
            s = lax.dot_general(qa_ref[r, :], ka_ref[kv, pl.ds(k0, tk), :],
                                (((1,), (1,)), ((), ())), preferred_element_type=F32)
            if causal:
                s = jnp.where(k0 + kcol <= qpos, s, NEG_INF)
            m_old = m_ref[r, :]
            m_new = jnp.maximum(m_old, jnp.max(s, axis=-1, keepdims=True))
            alpha = jnp.exp(m_old - m_new)
            p = jnp.exp(s - pltpu.repeat(m_new, tk // LANES, 1))
            l_ref[r, :] = alpha * l_ref[r, :] + jnp.sum(p, axis=-1, keepdims=True)
            acc_ref[r, :] = alpha[:, 0:HEAD_DIM] * acc_ref[r, :] + jnp.dot(
                p.astype(BF16), vb_ref[kv, pl.ds(k0, tk), :], preferred_element_type=F32)
            m_ref[r, :] = m_new

    n_full = (n * tq) // tk

    def body(kt, carry):
        step(kt, causal=False)
        return carry

    lax.fori_loop(0, n_full, body, 0)
    step(n_full, causal=True)

    for h in range(N_HEADS):
        r = slice(h * tq, (h + 1) * tq)
        o = acc_ref[r, :] * (1.0 / l_ref[r, 0:HEAD_DIM])
        o = o * jax.nn.sigmoid(gl_ref[0, :, gate_col + h:gate_col + h + 1])
        o_ref[0, :, h * HEAD_DIM:(h + 1) * HEAD_DIM] = o


def _sel_attention(proj, bias, *, tq=128, tk=256):
    b, s, _ = proj.shape
    n_sel = bias.shape[2] // N_KV
    rows = N_HEADS * tq
    return pl.pallas_call(
        functools.partial(_sel_attn_kernel, tq=tq, tk=tk, n_sel=n_sel, gate_col=N_HEADS),
        grid=(b, s // tq),
        in_specs=[
            pl.BlockSpec((1, tq, D_ATT), lambda i, n: (i, n, COL_QB // D_ATT)),
            pl.BlockSpec((1, s, D_KV), lambda i, n: (i, 0, COL_KS // D_KV)),
            pl.BlockSpec((1, s, D_KV), lambda i, n: (i, 0, COL_VS // D_KV)),
            pl.BlockSpec((1, tq, N_KV * n_sel), lambda i, n: (i, n, 0)),
            pl.BlockSpec((1, tq, LANES), lambda i, n: (i, n, COL_GL // LANES)),
        ],
        out_specs=pl.BlockSpec((1, tq, D_ATT), lambda i, n: (i, n, 0)),
        out_shape=jax.ShapeDtypeStruct((b, s, D_ATT), F32),
        scratch_shapes=[
            pltpu.VMEM((rows, LANES), BF16),
            pltpu.VMEM((N_KV, s, LANES), BF16),
            pltpu.VMEM((N_KV, s, HEAD_DIM), BF16),
            pltpu.VMEM((rows, LANES), F32),
            pltpu.VMEM((rows, LANES), F32),
            pltpu.VMEM((rows, HEAD_DIM), F32),
        ],
        compiler_params=_params(("parallel", "arbitrary")),
        name="sel_attn",
    )(proj, proj, proj, bias, proj)


def _rglru_kernel(xr_ref, xg_ref, cw_ref, cb_ref, wa_ref, ba_ref, wx_ref, bx_ref, lam_ref,
                  o_ref, xbuf_ref, h_ref, *, ts):
    pad = SUBLANES

    @pl.when(pl.program_id(1) == 0)
    def _():
        xbuf_ref[0:pad, :] = jnp.zeros((pad, RNN_WIDTH), F32)
        h_ref[...] = jnp.zeros_like(h_ref)

    xr = xr_ref[0]
    xbuf_ref[pad:pad + ts, :] = xr
    xc = cb_ref[...] + cw_ref[CONV_WIDTH - 1:CONV_WIDTH, :] * xr
    for w in range(CONV_WIDTH - 1):
        shift = CONV_WIDTH - 1 - w
        xc = xc + cw_ref[w:w + 1, :] * xbuf_ref[pl.ds(pad - shift, ts), :]
    xbuf_ref[0:pad, :] = xr[ts - pad:ts, :]

    ra, rx = [], []
    for blk in range(RNN_WIDTH // MXU_DIM):
        xs = xc[:, blk * MXU_DIM:(blk + 1) * MXU_DIM].astype(BF16)
        ra.append(jnp.dot(xs, wa_ref[blk], preferred_element_type=F32))
        rx.append(jnp.dot(xs, wx_ref[blk], preferred_element_type=F32))
    r = jax.nn.sigmoid(jnp.concatenate(ra, axis=1) + ba_ref[...])
    gi = jax.nn.sigmoid(jnp.concatenate(rx, axis=1) + bx_ref[...])
    nl = -lam_ref[...]
    softplus = jnp.maximum(nl, 0.0) + jnp.log1p(jnp.exp(-jnp.abs(nl)))
    log_a = -LRU_C * r * softplus
    a = jnp.exp(log_a)
    u = jnp.sqrt(-jnp.tanh(log_a) * (a * a + 1.0)) * (gi * xc)

    row = lax.broadcasted_iota(jnp.int32, (ts, RNN_WIDTH), 0)
    d = 1
    while d < ts:
        keep = row >= d
        a_prev = jnp.where(keep, pltpu.roll(a, d, 0), 1.0)
        u_prev = jnp.where(keep, pltpu.roll(u, d, 0), 0.0)
        u = a * u_prev + u
        a = a * a_prev
        d *= 2
    h = a * h_ref[0:1, :] + u
    h_ref[0:1, :] = h[ts - 1:ts, :]
    o_ref[0] = jax.nn.gelu(xg_ref[0], approximate=True) * h


def _rglru(proj, conv_w, conv_b, wa, ba, wx, bx, lam, *, ts=256):
    b, s, _ = proj.shape
    c = RNN_WIDTH
    vec = pl.BlockSpec((1, c), lambda i, n: (0, 0))
    wspec = pl.BlockSpec(wa.shape, lambda i, n: (0, 0, 0))
    return pl.pallas_call(
        functools.partial(_rglru_kernel, ts=ts),
        grid=(b, s // ts),
        in_specs=[
            pl.BlockSpec((1, ts, c), lambda i, n: (i, n, COL_XR // c)),
            pl.BlockSpec((1, ts, c), lambda i, n: (i, n, COL_XG // c)),
            pl.BlockSpec((CONV_WIDTH, c), lambda i, n: (0, 0)),
            vec, wspec, vec, wspec, vec, vec,
        ],
        out_specs=pl.BlockSpec((1, ts, c), lambda i, n: (i, n, 0)),
        out_shape=jax.ShapeDtypeStruct((b, s, c), F32),
        scratch_shapes=[pltpu.VMEM((ts + SUBLANES, c), F32), pltpu.VMEM((SUBLANES, c), F32)],
        compiler_params=_params(("parallel", "arbitrary")),
        name="rglru",
    )(proj, proj, conv_w, conv_b, wa, ba, wx, bx, lam)


def _outproj_kernel(x_ref, ya_ref, oc_ref, os_ref, ow_ref, yc_ref, gn_ref, w_ref, o_ref):
    ya = _rms(ya_ref[...], gn_ref[:, 0:D_ATT]).astype(BF16)
    yb = oc_ref[...] + os_ref[...] + ow_ref[...]
    yb = _rms(yb, gn_ref[:, D_ATT:2 * D_ATT]).astype(BF16)
    yc = _rms(yc_ref[...], gn_ref[:, 2 * D_ATT:]).astype(BF16)
    y = jnp.dot(ya, w_ref[0:D_ATT, :], preferred_element_type=F32)
    y = y + jnp.dot(yb, w_ref[D_ATT:2 * D_ATT, :], preferred_element_type=F32)
    y = y + jnp.dot(yc, w_ref[2 * D_ATT:, :], preferred_element_type=F32)
    o_ref[...] = x_ref[...] + y


def _outproj(x, ya, oc, osl, ow, yc, gn, w, *, tm=256):
    t, d = x.shape
    att = pl.BlockSpec((tm, D_ATT), lambda i: (i, 0))
    return pl.pallas_call(
        _outproj_kernel,
        grid=(t // tm,),
        in_specs=[
            pl.BlockSpec((tm, d), lambda i: (i, 0)),
            att, att, att, att,
            pl.BlockSpec((tm, RNN_WIDTH), lambda i: (i, 0)),
            pl.BlockSpec((1, gn.shape[1]), lambda i: (0, 0)),
            pl.BlockSpec(w.shape, lambda i: (0, 0)),
        ],
        out_specs=pl.BlockSpec((tm, d), lambda i: (i, 0)),
        out_shape=jax.ShapeDtypeStruct((t, d), F32),
        compiler_params=_params(("parallel",)),
        name="outproj",
    )(x, ya, oc, osl, ow, yc, gn, w)


def _reorder_w_in(w):
    offs = np.concatenate([[0], np.cumsum(_REF_SPLITS)])
    qa, ka, va, qb, kc, vc, ks, vs, kw, vw, gl, xr, xg = [
        w[:, int(offs[i]):int(offs[i + 1])] for i in range(len(_REF_SPLITS))]
    gl = jnp.pad(gl, ((0, 0), (0, LANES - gl.shape[1])))
    return jnp.concatenate([xr, xg, qa, qb, ka, va, kc, vc, ks, vs, kw, vw, gl], axis=1)


def _block_diag(w):
    per = MXU_DIM // RNN_BLOCK_WIDTH
    nt = w.shape[0] // per
    w = w.reshape(nt, per, RNN_BLOCK_WIDTH, RNN_BLOCK_WIDTH)
    eye = jnp.eye(per, dtype=w.dtype)
    out = jnp.einsum('tpcd,pq->tpcqd', w, eye)
    return out.reshape(nt, MXU_DIM, MXU_DIM)


def _overlap_matrix(seq, n_cmp_padded):
    n_c = (seq - CMP_LEN) // CMP_STRIDE + 1
    n_sel = seq // SEL_BLOCK
    cs = np.arange(n_c)[:, None] * CMP_STRIDE
    ss = np.arange(n_sel)[None, :] * SEL_BLOCK
    ov = np.clip(np.minimum(cs + CMP_LEN, ss + SEL_BLOCK) - np.maximum(cs, ss), 0, None)
    out = np.zeros((n_cmp_padded, n_sel), np.float32)
    out[:n_c] = ov / CMP_LEN
    return jnp.asarray(out)


def _to_cmp_rows(t, b, s):
    t = t.reshape(b, s // CMP_STRIDE, CMP_STRIDE, N_KV, HEAD_DIM)
    return t.transpose(0, 3, 1, 2, 4).reshape(b, N_KV, s // CMP_STRIDE, CMP_STRIDE * HEAD_DIM)


def kernel(x, ffn1_norm, ffn1_w_gate, ffn1_w_up, ffn1_w_down, mix_norm, w_in, swa_sinks, cmp_pos, cmp_w1, cmp_b1, cmp_w2, conv_w, conv_b, lru_wa, lru_ba, lru_wx, lru_bx, lru_lambda, group_norm, w_out, ffn2_norm, ffn2_w_gate, ffn2_w_up, ffn2_w_down, final_norm):
    b, s, d = x.shape
    depth = w_in.shape[0]
    t = b * s
    overlap = _overlap_matrix(s, s // CMP_STRIDE)
    xt = x.reshape(t, d)
    for l in range(depth):
        xt = _ffn(xt, ffn1_norm[l][None], ffn1_w_gate[l].astype(BF16),
                  ffn1_w_up[l].astype(BF16), ffn1_w_down[l].astype(BF16))

        proj = _inproj(xt, mix_norm[l][None], _reorder_w_in(w_in[l]).astype(BF16))
        proj = proj.reshape(b, s, D_PROJ)

        ya = _window_attention(proj, COL_QA, COL_KA, COL_VA, SWA_WINDOW, sinks=swa_sinks[l])

        k16 = _to_cmp_rows(proj[:, :, COL_KC:COL_KC + D_KV], b, s)
        v16 = _to_cmp_rows(proj[:, :, COL_VC:COL_VC + D_KV], b, s)
        pos = cmp_pos[l].reshape(2, 2, CMP_STRIDE * HEAD_DIM)
        kc, vc = _compress(k16, v16, pos, cmp_w1[l].astype(BF16), cmp_b1[l][:, None, :],
                           cmp_w2[l].astype(BF16))
        o_cmp, sel = _cmp_attention(proj, kc, vc, overlap)
        o_slc = _sel_attention(proj, sel)
        o_win = _window_attention(proj, COL_QB, COL_KW, COL_VW, NSA_WINDOW,
                                  gate_col=2 * N_HEADS)

        yc = _rglru(proj, conv_w[l], conv_b[l][None], _block_diag(lru_wa[l]).astype(BF16),
                    lru_ba[l][None], _block_diag(lru_wx[l]).astype(BF16), lru_bx[l][None],
                    lru_lambda[l][None])

        xt = _outproj(xt, ya.reshape(t, D_ATT), o_cmp.reshape(t, D_ATT),
                      o_slc.reshape(t, D_ATT), o_win.reshape(t, D_ATT),
                      yc.reshape(t, RNN_WIDTH), group_norm[l][None], w_out[l].astype(BF16))

        last = l == depth - 1
        xt = _ffn(xt, ffn2_norm[l][None], ffn2_w_gate[l].astype(BF16),
                  ffn2_w_up[l].astype(BF16), ffn2_w_down[l].astype(BF16),
                  final_g=final_norm[None] if last else None)
    return xt.reshape(b, s, d)
```

```python
import functools

import jax
import jax.numpy as jnp
import numpy as np
from jax import lax
from jax.experimental import pallas as pl
from jax.experimental.pallas import tpu as pltpu

F32 = jnp.float32
BF16 = jnp.bfloat16

D_MODEL = 2048
HEAD_DIM = 64
D_FF = 5632
N_HEADS = 8
N_KV = 2
GROUP = N_HEADS // N_KV
D_ATT = N_HEADS * HEAD_DIM
D_KV = N_KV * HEAD_DIM
SWA_WINDOW = 128
NSA_WINDOW = 512
ATTN_BLOCK = 128
CMP_LEN = 32
CMP_STRIDE = 16
CMP_HIDDEN = 256
SEL_BLOCK = 64
TOP_N = 16
RNN_WIDTH = 1024
RNN_BLOCK_WIDTH = 64
CONV_WIDTH = 4
LRU_C = 8.0
RMS_EPS = 1e-6
NEG_INF = -1e30
FORCED_SCORE = 1e4
SCALE = HEAD_DIM ** -0.5

LANES = 128
SUBLANES = 8
MXU_DIM = 256
VMEM_LIMIT = 56 * 1024 * 1024

COL_XR = 0
COL_XG = 1024
COL_QA = 2048
COL_QB = 2560
COL_KA = 3072
COL_VA = 3200
COL_KC = 3328
COL_VC = 3456
COL_KS = 3584
COL_VS = 3712
COL_KW = 3840
COL_VW = 3968
COL_GL = 4096
D_PROJ = 4224

_REF_SPLITS = (D_ATT, D_KV, D_KV, D_ATT, D_KV, D_KV, D_KV, D_KV, D_KV, D_KV, 3 * N_HEADS,
               RNN_WIDTH, RNN_WIDTH)


def _params(sem):
    return pltpu.CompilerParams(dimension_semantics=sem, vmem_limit_bytes=VMEM_LIMIT)


def _rms(x, g):
    return x * lax.rsqrt(jnp.mean(x * x, axis=-1, keepdims=True) + RMS_EPS) * g


def _ffn_kernel(x_ref, g_ref, wg_ref, wu_ref, wd_ref, *rest, final_norm):
    if final_norm:
        fg_ref, o_ref, h_ref, acc_ref = rest
    else:
        o_ref, h_ref, acc_ref = rest
    j = pl.program_id(1)

    @pl.when(j == 0)
    def _():
        h_ref[...] = _rms(x_ref[...], g_ref[...]).astype(BF16)
        acc_ref[...] = jnp.zeros_like(acc_ref)

    h = h_ref[...]
    gate = jnp.dot(h, wg_ref[...], preferred_element_type=F32)
    up = jnp.dot(h, wu_ref[...], preferred_element_type=F32)
    act = (gate * jax.nn.sigmoid(gate) * up).astype(BF16)
    acc_ref[...] += jnp.dot(act, wd_ref[...], preferred_element_type=F32)

    @pl.when(j == pl.num_programs(1) - 1)
    def _():
        y = x_ref[...] + 0.5 * acc_ref[...]
        if final_norm:
            y = _rms(y, fg_ref[...])
        o_ref[...] = y


def _ffn(x, g, wg, wu, wd, final_g=None, *, tm=512, tf=512):
    t, d = x.shape
    f = wg.shape[1]
    final_norm = final_g is not None
    in_specs = [
        pl.BlockSpec((tm, d), lambda i, j: (i, 0)),
        pl.BlockSpec((1, d), lambda i, j: (0, 0)),
        pl.BlockSpec((d, tf), lambda i, j: (0, j)),
        pl.BlockSpec((d, tf), lambda i, j: (0, j)),
        pl.BlockSpec((tf, d), lambda i, j: (j, 0)),
    ]
    args = [x, g, wg, wu, wd]
    if final_norm:
        in_specs.append(pl.BlockSpec((1, d), lambda i, j: (0, 0)))
        args.append(final_g)
    return pl.pallas_call(
        functools.partial(_ffn_kernel, final_norm=final_norm),
        grid=(t // tm, f // tf),
        in_specs=in_specs,
        out_specs=pl.BlockSpec((tm, d), lambda i, j: (i, 0)),
        out_shape=jax.ShapeDtypeStruct((t, d), F32),
        scratch_shapes=[pltpu.VMEM((tm, d), BF16), pltpu.VMEM((tm, d), F32)],
        compiler_params=_params(("parallel", "arbitrary")),
        name="ffn",
    )(*args)


def _inproj_kernel(x_ref, g_ref, w_ref, o_ref, h_ref):
    @pl.when(pl.program_id(1) == 0)
    def _():
        h_ref[...] = _rms(x_ref[...], g_ref[...]).astype(BF16)

    o_ref[...] = jnp.dot(h_ref[...], w_ref[...], preferred_element_type=F32)


def _inproj(x, g, w, *, tm=512, tn=1408):
    t, d = x.shape
    n = w.shape[1]
    return pl.pallas_call(
        _inproj_kernel,
        grid=(t // tm, n // tn),
        in_specs=[
            pl.BlockSpec((tm, d), lambda i, j: (i, 0)),
            pl.BlockSpec((1, d), lambda i, j: (0, 0)),
            pl.BlockSpec((d, tn), lambda i, j: (0, j)),
        ],
        out_specs=pl.BlockSpec((tm, tn), lambda i, j: (i, j)),
        out_shape=jax.ShapeDtypeStruct((t, n), F32),
        scratch_shapes=[pltpu.VMEM((tm, d), BF16)],
        compiler_params=_params(("parallel", "arbitrary")),
        name="inproj",
    )(x, g, w)


def _stack_heads(q_ref, kv, tok=slice(None)):
    heads = [q_ref[0, tok, (kv * GROUP + g) * HEAD_DIM:(kv * GROUP + g + 1) * HEAD_DIM]
             for g in range(GROUP)]
    return (jnp.concatenate(heads, axis=0) * SCALE).astype(BF16)


def _window_kernel(*refs, window, n_prev, q_blocks, use_sink, gate_col):
    refs = list(refs)
    q_ref, k_ref, v_ref = refs[:3]
    rest = refs[3:]
    sink_ref = rest.pop(0) if use_sink else None
    gl_ref = rest.pop(0) if gate_col is not None else None
    o_ref, kb_ref, vb_ref = rest

    blk = ATTN_BLOCK
    rows = GROUP * blk
    span = (n_prev + 1) * blk
    n = pl.program_id(1)

    @pl.when(n == 0)
    def _():
        for kv in range(N_KV):
            kb_ref[kv] = k_ref[0, :, kv * HEAD_DIM:(kv + 1) * HEAD_DIM].astype(BF16)
            vb_ref[kv] = v_ref[0, :, kv * HEAD_DIM:(kv + 1) * HEAD_DIM].astype(BF16)

    row = lax.broadcasted_iota(jnp.int32, (rows, span), 0) & (blk - 1)
    col = lax.broadcasted_iota(jnp.int32, (rows, span), 1)
    for j in range(q_blocks):
        nb = n * q_blocks + j
        tok = slice(j * blk, (j + 1) * blk)
        start = pl.multiple_of(jnp.maximum(nb - n_prev, 0) * blk, blk)
        diff = (nb * blk + row) - (start + col)
        mask = (diff >= 0) & (diff < window)
        for kv in range(N_KV):
            q = _stack_heads(q_ref, kv, tok)
            k = kb_ref[kv, pl.ds(start, span), :]
            v = vb_ref[kv, pl.ds(start, span), :]
            s = lax.dot_general(q, k, (((1,), (1,)), ((), ())), preferred_element_type=F32)
            s = jnp.where(mask, s, NEG_INF)
            m = jnp.max(s, axis=-1, keepdims=True)
            if use_sink:
                sk = jnp.concatenate([jnp.full((blk, 1), sink_ref[kv * GROUP + g], F32)
                                      for g in range(GROUP)], axis=0)
                m = jnp.maximum(m, sk)
                p = jnp.exp(s - m)
                denom = jnp.sum(p, axis=-1, keepdims=True) + jnp.exp(sk - m)
            else:
                p = jnp.exp(s - m)
                denom = jnp.sum(p, axis=-1, keepdims=True)
            o = jnp.dot(p.astype(BF16), v, preferred_element_type=F32) * (1.0 / denom)
            for g in range(GROUP):
                h = kv * GROUP + g
                oh = o[g * blk:(g + 1) * blk, :]
                if gate_col is not None:
                    oh = oh * jax.nn.sigmoid(gl_ref[0, tok, gate_col + h:gate_col + h + 1])
                o_ref[0, tok, h * HEAD_DIM:(h + 1) * HEAD_DIM] = oh


def _window_attention(proj, q_col, k_col, v_col, window, sinks=None, gate_col=None, *,
                      q_blocks=4):
    b, s, _ = proj.shape
    n_prev = -(-(window - 1) // ATTN_BLOCK)
    blk = q_blocks * ATTN_BLOCK
    in_specs = [
        pl.BlockSpec((1, blk, D_ATT), lambda i, n: (i, n, q_col // D_ATT)),
        pl.BlockSpec((1, s, D_KV), lambda i, n: (i, 0, k_col // D_KV)),
        pl.BlockSpec((1, s, D_KV), lambda i, n: (i, 0, v_col // D_KV)),
    ]
    args = [proj, proj, proj]
    if sinks is not None:
        in_specs.append(pl.BlockSpec(memory_space=pltpu.SMEM))
        args.append(sinks)
    if gate_col is not None:
        in_specs.append(pl.BlockSpec((1, blk, LANES), lambda i, n: (i, n, COL_GL // LANES)))
        args.append(proj)
    return pl.pallas_call(
        functools.partial(_window_kernel, window=window, n_prev=n_prev, q_blocks=q_blocks,
                          use_sink=sinks is not None, gate_col=gate_col),
        grid=(b, s // blk),
        in_specs=in_specs,
        out_specs=pl.BlockSpec((1, blk, D_ATT), lambda i, n: (i, n, 0)),
        out_shape=jax.ShapeDtypeStruct((b, s, D_ATT), F32),
        scratch_shapes=[pltpu.VMEM((N_KV, s, HEAD_DIM), BF16),
                        pltpu.VMEM((N_KV, s, HEAD_DIM), BF16)],
        compiler_params=_params(("parallel", "arbitrary")),
        name=f"window{window}",
    )(*args)


def _compress_kernel(k16_ref, v16_ref, pos_ref, w1_ref, b1_ref, w2_ref, kc_ref, vc_ref):
    half = CMP_STRIDE * HEAD_DIM
    for i, (src_ref, dst_ref) in enumerate(((k16_ref, kc_ref), (v16_ref, vc_ref))):
        for kv in range(N_KV):
            x = src_ref[0, kv]
            lo = (x + pos_ref[i, 0:1, :]).astype(BF16)
            hi = (x + pos_ref[i, 1:2, :]).astype(BF16)
            a = jnp.dot(lo, w1_ref[i, 0:half, :], preferred_element_type=F32)
            bm = jnp.dot(hi, w1_ref[i, half:2 * half, :], preferred_element_type=F32)
            rows = bm.shape[0]
            hid = a + pltpu.roll(bm, rows - 1, 0) + b1_ref[i]
            act = jax.nn.gelu(hid, approximate=True).astype(BF16)
            dst_ref[0, kv] = jnp.dot(act, w2_ref[i], preferred_element_type=F32)


def _compress(k16, v16, pos, w1, b1, w2):
    b = k16.shape[0]
    nrow = k16.shape[2]
    spec16 = pl.BlockSpec((1, N_KV, nrow, CMP_STRIDE * HEAD_DIM), lambda i: (i, 0, 0, 0))
    out_spec = pl.BlockSpec((1, N_KV, nrow, HEAD_DIM), lambda i: (i, 0, 0, 0))
    out_shape = jax.ShapeDtypeStruct((b, N_KV, nrow, HEAD_DIM), F32)
    return pl.pallas_call(
        _compress_kernel,
        grid=(b,),
        in_specs=[
            spec16, spec16,
            pl.BlockSpec(pos.shape, lambda i: (0, 0, 0)),
            pl.BlockSpec(w1.shape, lambda i: (0, 0, 0)),
            pl.BlockSpec(b1.shape, lambda i: (0, 0, 0)),
            pl.BlockSpec(w2.shape, lambda i: (0, 0, 0)),
        ],
        out_specs=[out_spec, out_spec],
        out_shape=[out_shape, out_shape],
        compiler_params=_params(("parallel",)),
        name="compress",
    )(k16, v16, pos, w1, b1, w2)


def _cmp_attn_kernel(q_ref, kc_ref, vc_ref, gl_ref, ov_ref, o_ref, sel_ref, *, tq, n_sel):
    n = pl.program_id(1)
    ncp = kc_ref.shape[2]
    rows = GROUP * tq
    pos_c = n * tq + (lax.broadcasted_iota(jnp.int32, (rows, ncp), 0) & (tq - 1))
    cend = lax.broadcasted_iota(jnp.int32, (rows, ncp), 1) * CMP_STRIDE + (CMP_LEN - 1)
    cmask = cend <= pos_c

    pos_s = n * tq + lax.broadcasted_iota(jnp.int32, (n_sel, tq), 1)
    bid = lax.broadcasted_iota(jnp.int32, (n_sel, tq), 0)
    cur = pos_s // SEL_BLOCK
    forced = (bid == 0) | (bid == cur) | (bid == cur - 1)
    valid = bid * SEL_BLOCK <= pos_s

    biases = []
    for kv in range(N_KV):
        kc = kc_ref[0, kv].astype(BF16)
        vc = vc_ref[0, kv].astype(BF16)
        q = _stack_heads(q_ref, kv)
        s = lax.dot_general(q, kc, (((1,), (1,)), ((), ())), preferred_element_type=F32)
        s = jnp.where(cmask, s, NEG_INF)
        m = jnp.max(s, axis=-1, keepdims=True)
        p = jnp.exp(s - m)
        p = p * (1.0 / jnp.sum(p, axis=-1, keepdims=True))
        p = jnp.where(cmask, p, 0.0)
        o = jnp.dot(p.astype(BF16), vc, preferred_element_type=F32)
        psum = p[0:tq]
        for g in range(GROUP):
            h = kv * GROUP + g
            if g:
                psum = psum + p[g * tq:(g + 1) * tq]
            oh = o[g * tq:(g + 1) * tq, :] * jax.nn.sigmoid(gl_ref[0, :, h:h + 1])
            o_ref[0, :, h * HEAD_DIM:(h + 1) * HEAD_DIM] = oh

        imp = lax.dot_general(ov_ref[...], psum, (((1,), (1,)), ((), ())),
                              preferred_element_type=F32,
                              precision=lax.Precision.HIGHEST)
        imp = jnp.where(forced, FORCED_SCORE, imp)
        imp = jnp.where(valid, imp, NEG_INF)
        rank = jnp.zeros((n_sel, tq), jnp.int32)
        for i in range(n_sel):
            row = imp[i:i + 1, :]
            ahead = (row > imp) | ((row == imp) & (bid > i))
            rank = rank + ahead.astype(jnp.int32)
        biases.append(jnp.where(rank < TOP_N, 0.0, NEG_INF))
    sel_ref[0] = jnp.concatenate(biases, axis=0).T


def _cmp_attention(proj, kc, vc, overlap, *, tq=256):
    b, s, _ = proj.shape
    n_sel = overlap.shape[0]
    ncp = kc.shape[2]
    return pl.pallas_call(
        functools.partial(_cmp_attn_kernel, tq=tq, n_sel=n_sel),
        grid=(b, s // tq),
        in_specs=[
            pl.BlockSpec((1, tq, D_ATT), lambda i, n: (i, n, COL_QB // D_ATT)),
            pl.BlockSpec((1, N_KV, ncp, HEAD_DIM), lambda i, n: (i, 0, 0, 0)),
            pl.BlockSpec((1, N_KV, ncp, HEAD_DIM), lambda i, n: (i, 0, 0, 0)),
            pl.BlockSpec((1, tq, LANES), lambda i, n: (i, n, COL_GL // LANES)),
            pl.BlockSpec(overlap.shape, lambda i, n: (0, 0)),
        ],
        out_specs=[
            pl.BlockSpec((1, tq, D_ATT), lambda i, n: (i, n, 0)),
            pl.BlockSpec((1, tq, N_KV * n_sel), lambda i, n: (i, n, 0)),
        ],
        out_shape=[
            jax.ShapeDtypeStruct((b, s, D_ATT), F32),
            jax.ShapeDtypeStruct((b, s, N_KV * n_sel), F32),
        ],
        compiler_params=_params(("parallel", "arbitrary")),
        name="cmp_attn",
    )(proj, kc, vc, proj, overlap)


def _sel_attn_kernel(q_ref, k_ref, v_ref, bias_ref, gl_ref, o_ref,
                     qa_ref, ka_ref, vb_ref, m_ref, l_ref, acc_ref, *,
                     tq, tk, chains, n_sel, gate_col):
    n = pl.program_id(1)
    s_len = k_ref.shape[1]
    rows = GROUP * tq
    n_pad = LANES - HEAD_DIM - n_sel

    @pl.when(n == 0)
    def _():
        kpos = lax.broadcasted_iota(jnp.int32, (s_len, n_sel), 0)
        blk = lax.broadcasted_iota(jnp.int32, (s_len, n_sel), 1)
        onehot = (kpos // SEL_BLOCK == blk).astype(F32)
        zeros = jnp.zeros((s_len, n_pad), F32)
        for kv in range(N_KV):
            k = k_ref[0, :, kv * HEAD_DIM:(kv + 1) * HEAD_DIM]
            ka_ref[kv] = jnp.concatenate([k, onehot, zeros], axis=1).astype(BF16)
            vb_ref[kv] = v_ref[0, :, kv * HEAD_DIM:(kv + 1) * HEAD_DIM].astype(BF16)

    zq = jnp.zeros((tq, n_pad), F32)
    for kv in range(N_KV):
        bias = bias_ref[0, :, kv * n_sel:(kv + 1) * n_sel]
        for g in range(GROUP):
            h = kv * GROUP + g
            q = q_ref[0, :, h * HEAD_DIM:(h + 1) * HEAD_DIM] * SCALE
            qa_ref[h * tq:(h + 1) * tq, :] = jnp.concatenate([q, bias, zq], axis=1).astype(BF16)
    m_ref[...] = jnp.full(m_ref.shape, NEG_INF, F32)
    l_ref[...] = jnp.zeros(l_ref.shape, F32)
    acc_ref[...] = jnp.zeros(acc_ref.shape, F32)

    crows = rows // chains
    qpos = n * tq + (lax.broadcasted_iota(jnp.int32, (crows, tk), 0) & (tq - 1))
    kcol = lax.broadcasted_iota(jnp.int32, (crows, tk), 1)

    def step(kt, causal):
        k0 = pl.multiple_of(kt * tk, tk)
        for c in range(N_KV * chains):
            kv = c // chains
            r = slice(c * crows, (c + 1) * crows)
            s = lax.dot_general(qa_ref[r, :], ka_ref[kv, pl.ds(k0, tk), :],
                                (((1,), (1,)), ((), ())), preferred_element_type=F32)
            if causal:
                s = jnp.where(k0 + kcol <= qpos, s, NEG_INF)
            m_old = m_ref[r, :]
            m_new = jnp.maximum(m_old, jnp.max(s, axis=-1, keepdims=True))
            alpha = jnp.exp(m_old - m_new)
            p = jnp.exp(s - jnp.tile(m_new, (1, tk // LANES)))
            l_ref[r, :] = alpha * l_ref[r, :] + jnp.sum(p, axis=-1, keepdims=True)
            acc_ref[r, :] = alpha[:, 0:HEAD_DIM] * acc_ref[r, :] + jnp.dot(
                p.astype(BF16), vb_ref[kv, pl.ds(k0, tk), :], preferred_element_type=F32)
            m_ref[r, :] = m_new

    n_full = (n * tq) // tk

    def body(kt, carry):
        step(kt, causal=False)
        return carry

    lax.fori_loop(0, n_full, body, 0)
    step(n_full, causal=True)

    for h in range(N_HEADS):
        r = slice(h * tq, (h + 1) * tq)
        o = acc_ref[r, :] * (1.0 / l_ref[r, 0:HEAD_DIM])
        o = o * jax.nn.sigmoid(gl_ref[0, :, gate_col + h:gate_col + h + 1])
        o_ref[0, :, h * HEAD_DIM:(h + 1) * HEAD_DIM] = o


def _sel_attention(proj, bias, *, tq=256, tk=256, chains=2):
    b, s, _ = proj.shape
    n_sel = bias.shape[2] // N_KV
    rows = N_HEADS * tq
    return pl.pallas_call(
        functools.partial(_sel_attn_kernel, tq=tq, tk=tk, chains=chains, n_sel=n_sel,
                          gate_col=N_HEADS),
        grid=(b, s // tq),
        in_specs=[
            pl.BlockSpec((1, tq, D_ATT), lambda i, n: (i, n, COL_QB // D_ATT)),
            pl.BlockSpec((1, s, D_KV), lambda i, n: (i, 0, COL_KS // D_KV)),
            pl.BlockSpec((1, s, D_KV), lambda i, n: (i, 0, COL_VS // D_KV)),
            pl.BlockSpec((1, tq, N_KV * n_sel), lambda i, n: (i, n, 0)),
            pl.BlockSpec((1, tq, LANES), lambda i, n: (i, n, COL_GL // LANES)),
        ],
        out_specs=pl.BlockSpec((1, tq, D_ATT), lambda i, n: (i, n, 0)),
        out_shape=jax.ShapeDtypeStruct((b, s, D_ATT), F32),
        scratch_shapes=[
            pltpu.VMEM((rows, LANES), BF16),
            pltpu.VMEM((N_KV, s, LANES), BF16),
            pltpu.VMEM((N_KV, s, HEAD_DIM), BF16),
            pltpu.VMEM((rows, LANES), F32),
            pltpu.VMEM((rows, LANES), F32),
            pltpu.VMEM((rows, HEAD_DIM), F32),
        ],
        compiler_params=_params(("parallel", "arbitrary")),
        name="sel_attn",
    )(proj, proj, proj, bias, proj)


def _rglru_kernel(xr_ref, xg_ref, cw_ref, cb_ref, wa_ref, ba_ref, wx_ref, bx_ref, lam_ref,
                  o_ref, xbuf_ref, h_ref, *, ts):
    pad = SUBLANES

    @pl.when(pl.program_id(1) == 0)
    def _():
        xbuf_ref[0:pad, :] = jnp.zeros((pad, RNN_WIDTH), F32)
        h_ref[...] = jnp.zeros_like(h_ref)

    xr = xr_ref[0]
    xbuf_ref[pad:pad + ts, :] = xr
    xc = cb_ref[...] + cw_ref[CONV_WIDTH - 1:CONV_WIDTH, :] * xr
    for w in range(CONV_WIDTH - 1):
        shift = CONV_WIDTH - 1 - w
        xc = xc + cw_ref[w:w + 1, :] * xbuf_ref[pl.ds(pad - shift, ts), :]
    xbuf_ref[0:pad, :] = xr[ts - pad:ts, :]

    ra, rx = [], []
    for blk in range(RNN_WIDTH // MXU_DIM):
        xs = xc[:, blk * MXU_DIM:(blk + 1) * MXU_DIM].astype(BF16)
        ra.append(jnp.dot(xs, wa_ref[blk], preferred_element_type=F32))
        rx.append(jnp.dot(xs, wx_ref[blk], preferred_element_type=F32))
    r = jax.nn.sigmoid(jnp.concatenate(ra, axis=1) + ba_ref[...])
    gi = jax.nn.sigmoid(jnp.concatenate(rx, axis=1) + bx_ref[...])
    nl = -lam_ref[...]
    softplus = jnp.maximum(nl, 0.0) + jnp.log1p(jnp.exp(-jnp.abs(nl)))
    log_a = -LRU_C * r * softplus
    a = jnp.exp(log_a)
    u = jnp.sqrt(-jnp.tanh(log_a) * (a * a + 1.0)) * (gi * xc)

    row = lax.broadcasted_iota(jnp.int32, (ts, RNN_WIDTH), 0)
    d = 1
    while d < ts:
        keep = row >= d
        a_prev = jnp.where(keep, pltpu.roll(a, d, 0), 1.0)
        u_prev = jnp.where(keep, pltpu.roll(u, d, 0), 0.0)
        u = a * u_prev + u
        a = a * a_prev
        d *= 2
    h = a * h_ref[0:1, :] + u
    h_ref[0:1, :] = h[ts - 1:ts, :]
    o_ref[0] = jax.nn.gelu(xg_ref[0], approximate=True) * h


def _rglru(proj, conv_w, conv_b, wa, ba, wx, bx, lam, *, ts=256):
    b, s, _ = proj.shape
    c = RNN_WIDTH
    vec = pl.BlockSpec((1, c), lambda i, n: (0, 0))
    wspec = pl.BlockSpec(wa.shape, lambda i, n: (0, 0, 0))
    return pl.pallas_call(
        functools.partial(_rglru_kernel, ts=ts),
        grid=(b, s // ts),
        in_specs=[
            pl.BlockSpec((1, ts, c), lambda i, n: (i, n, COL_XR // c)),
            pl.BlockSpec((1, ts, c), lambda i, n: (i, n, COL_XG // c)),
            pl.BlockSpec((CONV_WIDTH, c), lambda i, n: (0, 0)),
            vec, wspec, vec, wspec, vec, vec,
        ],
        out_specs=pl.BlockSpec((1, ts, c), lambda i, n: (i, n, 0)),
        out_shape=jax.ShapeDtypeStruct((b, s, c), F32),
        scratch_shapes=[pltpu.VMEM((ts + SUBLANES, c), F32), pltpu.VMEM((SUBLANES, c), F32)],
        compiler_params=_params(("parallel", "arbitrary")),
        name="rglru",
    )(proj, proj, conv_w, conv_b, wa, ba, wx, bx, lam)


def _outproj_kernel(x_ref, ya_ref, oc_ref, os_ref, ow_ref, yc_ref, gn_ref, w_ref, o_ref):
    ya = _rms(ya_ref[...], gn_ref[:, 0:D_ATT]).astype(BF16)
    yb = oc_ref[...] + os_ref[...] + ow_ref[...]
    yb = _rms(yb, gn_ref[:, D_ATT:2 * D_ATT]).astype(BF16)
    yc = _rms(yc_ref[...], gn_ref[:, 2 * D_ATT:]).astype(BF16)
    y = jnp.dot(ya, w_ref[0:D_ATT, :], preferred_element_type=F32)
    y = y + jnp.dot(yb, w_ref[D_ATT:2 * D_ATT, :], preferred_element_type=F32)
    y = y + jnp.dot(yc, w_ref[2 * D_ATT:, :], preferred_element_type=F32)
    o_ref[...] = x_ref[...] + y


def _outproj(x, ya, oc, osl, ow, yc, gn, w, *, tm=256):
    t, d = x.shape
    att = pl.BlockSpec((tm, D_ATT), lambda i: (i, 0))
    return pl.pallas_call(
        _outproj_kernel,
        grid=(t // tm,),
        in_specs=[
            pl.BlockSpec((tm, d), lambda i: (i, 0)),
            att, att, att, att,
            pl.BlockSpec((tm, RNN_WIDTH), lambda i: (i, 0)),
            pl.BlockSpec((1, gn.shape[1]), lambda i: (0, 0)),
            pl.BlockSpec(w.shape, lambda i: (0, 0)),
        ],
        out_specs=pl.BlockSpec((tm, d), lambda i: (i, 0)),
        out_shape=jax.ShapeDtypeStruct((t, d), F32),
        compiler_params=_params(("parallel",)),
        name="outproj",
    )(x, ya, oc, osl, ow, yc, gn, w)


def _reorder_w_in(w):
    offs = np.concatenate([[0], np.cumsum(_REF_SPLITS)])
    qa, ka, va, qb, kc, vc, ks, vs, kw, vw, gl, xr, xg = [
        w[:, int(offs[i]):int(offs[i + 1])] for i in range(len(_REF_SPLITS))]
    gl = jnp.pad(gl, ((0, 0), (0, LANES - gl.shape[1])))
    return jnp.concatenate([xr, xg, qa, qb, ka, va, kc, vc, ks, vs, kw, vw, gl], axis=1)


def _block_diag(w):
    per = MXU_DIM // RNN_BLOCK_WIDTH
    nt = w.shape[0] // per
    w = w.reshape(nt, per, RNN_BLOCK_WIDTH, RNN_BLOCK_WIDTH)
    eye = jnp.eye(per, dtype=w.dtype)
    out = jnp.einsum('tpcd,pq->tpcqd', w, eye)
    return out.reshape(nt, MXU_DIM, MXU_DIM)


def _overlap_matrix(seq, n_cmp_padded):
    n_c = (seq - CMP_LEN) // CMP_STRIDE + 1
    n_sel = seq // SEL_BLOCK
    cs = np.arange(n_c)[:, None] * CMP_STRIDE
    ss = np.arange(n_sel)[None, :] * SEL_BLOCK
    ov = np.clip(np.minimum(cs + CMP_LEN, ss + SEL_BLOCK) - np.maximum(cs, ss), 0, None)
    out = np.zeros((n_cmp_padded, n_sel), np.float32)
    out[:n_c] = ov / CMP_LEN
    return jnp.asarray(out.T)


def _to_cmp_rows(t, b, s):
    t = t.reshape(b, s // CMP_STRIDE, CMP_STRIDE, N_KV, HEAD_DIM)
    return t.transpose(0, 3, 1, 2, 4).reshape(b, N_KV, s // CMP_STRIDE, CMP_STRIDE * HEAD_DIM)


def kernel(x, ffn1_norm, ffn1_w_gate, ffn1_w_up, ffn1_w_down, mix_norm, w_in, swa_sinks, cmp_pos, cmp_w1, cmp_b1, cmp_w2, conv_w, conv_b, lru_wa, lru_ba, lru_wx, lru_bx, lru_lambda, group_norm, w_out, ffn2_norm, ffn2_w_gate, ffn2_w_up, ffn2_w_down, final_norm):
    b, s, d = x.shape
    depth = w_in.shape[0]
    t = b * s
    overlap = _overlap_matrix(s, s // CMP_STRIDE)
    xt = x.reshape(t, d)
    for l in range(depth):
        xt = _ffn(xt, ffn1_norm[l][None], ffn1_w_gate[l].astype(BF16),
                  ffn1_w_up[l].astype(BF16), ffn1_w_down[l].astype(BF16))

        proj = _inproj(xt, mix_norm[l][None], _reorder_w_in(w_in[l]).astype(BF16))
        proj = proj.reshape(b, s, D_PROJ)

        ya = _window_attention(proj, COL_QA, COL_KA, COL_VA, SWA_WINDOW, sinks=swa_sinks[l])

        k16 = _to_cmp_rows(proj[:, :, COL_KC:COL_KC + D_KV], b, s)
        v16 = _to_cmp_rows(proj[:, :, COL_VC:COL_VC + D_KV], b, s)
        pos = cmp_pos[l].reshape(2, 2, CMP_STRIDE * HEAD_DIM)
        kc, vc = _compress(k16, v16, pos, cmp_w1[l].astype(BF16), cmp_b1[l][:, None, :],
                           cmp_w2[l].astype(BF16))
        o_cmp, sel = _cmp_attention(proj, kc, vc, overlap)
        o_slc = _sel_attention(proj, sel)
        o_win = _window_attention(proj, COL_QB, COL_KW, COL_VW, NSA_WINDOW,
                                  gate_col=2 * N_HEADS, q_blocks=2)

        yc = _rglru(proj, conv_w[l], conv_b[l][None], _block_diag(lru_wa[l]).astype(BF16),
                    lru_ba[l][None], _block_diag(lru_wx[l]).astype(BF16), lru_bx[l][None],
                    lru_lambda[l][None])

        xt = _outproj(xt, ya.reshape(t, D_ATT), o_cmp.reshape(t, D_ATT),
                      o_slc.reshape(t, D_ATT), o_win.reshape(t, D_ATT),
                      yc.reshape(t, RNN_WIDTH), group_norm[l][None], w_out[l].astype(BF16))

        last = l == depth - 1
        xt = _ffn(xt, ffn2_norm[l][None], ffn2_w_gate[l].astype(BF16),
                  ffn2_w_up[l].astype(BF16), ffn2_w_down[l].astype(BF16),
                  final_g=final_norm[None] if last else None)
    return xt.reshape(b, s, d)
```

```python
import functools

import jax
import jax.numpy as jnp
import numpy as np
from jax import lax
from jax.experimental import pallas as pl
from jax.experimental.pallas import tpu as pltpu

F32 = jnp.float32
BF16 = jnp.bfloat16

D_MODEL = 2048
HEAD_DIM = 64
D_FF = 5632
N_HEADS = 8
N_KV = 2
GROUP = N_HEADS // N_KV
D_ATT = N_HEADS * HEAD_DIM
D_KV = N_KV * HEAD_DIM
SWA_WINDOW = 128
NSA_WINDOW = 512
ATTN_BLOCK = 128
CMP_LEN = 32
CMP_STRIDE = 16
CMP_HIDDEN = 256
SEL_BLOCK = 64
TOP_N = 16
RNN_WIDTH = 1024
RNN_BLOCK_WIDTH = 64
CONV_WIDTH = 4
LRU_C = 8.0
RMS_EPS = 1e-6
NEG_INF = -1e30
FORCED_SCORE = 1e4
SCALE = HEAD_DIM ** -0.5

LANES = 128
SUBLANES = 8
MXU_DIM = 256
VMEM_LIMIT = 56 * 1024 * 1024

COL_XR = 0
COL_XG = 1024
COL_QA = 2048
COL_QB = 2560
COL_KA = 3072
COL_VA = 3200
COL_KC = 3328
COL_VC = 3456
COL_KS = 3584
COL_VS = 3712
COL_KW = 3840
COL_VW = 3968
COL_GL = 4096
D_PROJ = 4224

_REF_SPLITS = (D_ATT, D_KV, D_KV, D_ATT, D_KV, D_KV, D_KV, D_KV, D_KV, D_KV, 3 * N_HEADS,
               RNN_WIDTH, RNN_WIDTH)


def _params(sem):
    return pltpu.CompilerParams(dimension_semantics=sem, vmem_limit_bytes=VMEM_LIMIT)


def _rms(x, g):
    return x * lax.rsqrt(jnp.mean(x * x, axis=-1, keepdims=True) + RMS_EPS) * g


def _ffn_kernel(x_ref, g_ref, wg_ref, wu_ref, wd_ref, *rest, final_norm):
    if final_norm:
        fg_ref, o_ref, h_ref, acc_ref = rest
    else:
        o_ref, h_ref, acc_ref = rest
    j = pl.program_id(1)

    @pl.when(j == 0)
    def _():
        h_ref[...] = _rms(x_ref[...], g_ref[...]).astype(BF16)
        acc_ref[...] = jnp.zeros_like(acc_ref)

    h = h_ref[...]
    gate = jnp.dot(h, wg_ref[...], preferred_element_type=F32)
    up = jnp.dot(h, wu_ref[...], preferred_element_type=F32)
    act = (gate * jax.nn.sigmoid(gate) * up).astype(BF16)
    acc_ref[...] += jnp.dot(act, wd_ref[...], preferred_element_type=F32)

    @pl.when(j == pl.num_programs(1) - 1)
    def _():
        y = x_ref[...] + 0.5 * acc_ref[...]
        if final_norm:
            y = _rms(y, fg_ref[...])
        o_ref[...] = y


def _ffn(x, g, wg, wu, wd, final_g=None, *, tm=512, tf=512):
    t, d = x.shape
    f = wg.shape[1]
    final_norm = final_g is not None
    in_specs = [
        pl.BlockSpec((tm, d), lambda i, j: (i, 0)),
        pl.BlockSpec((1, d), lambda i, j: (0, 0)),
        pl.BlockSpec((d, tf), lambda i, j: (0, j)),
        pl.BlockSpec((d, tf), lambda i, j: (0, j)),
        pl.BlockSpec((tf, d), lambda i, j: (j, 0)),
    ]
    args = [x, g, wg, wu, wd]
    if final_norm:
        in_specs.append(pl.BlockSpec((1, d), lambda i, j: (0, 0)))
        args.append(final_g)
    return pl.pallas_call(
        functools.partial(_ffn_kernel, final_norm=final_norm),
        grid=(t // tm, f // tf),
        in_specs=in_specs,
        out_specs=pl.BlockSpec((tm, d), lambda i, j: (i, 0)),
        out_shape=jax.ShapeDtypeStruct((t, d), F32),
        scratch_shapes=[pltpu.VMEM((tm, d), BF16), pltpu.VMEM((tm, d), F32)],
        compiler_params=_params(("parallel", "arbitrary")),
        name="ffn",
    )(*args)


def _inproj_kernel(x_ref, g_ref, w_ref, o_ref, h_ref):
    @pl.when(pl.program_id(1) == 0)
    def _():
        h_ref[...] = _rms(x_ref[...], g_ref[...]).astype(BF16)

    o_ref[...] = jnp.dot(h_ref[...], w_ref[...], preferred_element_type=F32)


def _inproj(x, g, w, *, tm=512, tn=1408):
    t, d = x.shape
    n = w.shape[1]
    return pl.pallas_call(
        _inproj_kernel,
        grid=(t // tm, n // tn),
        in_specs=[
            pl.BlockSpec((tm, d), lambda i, j: (i, 0)),
            pl.BlockSpec((1, d), lambda i, j: (0, 0)),
            pl.BlockSpec((d, tn), lambda i, j: (0, j)),
        ],
        out_specs=pl.BlockSpec((tm, tn), lambda i, j: (i, j)),
        out_shape=jax.ShapeDtypeStruct((t, n), F32),
        scratch_shapes=[pltpu.VMEM((tm, d), BF16)],
        compiler_params=_params(("parallel", "arbitrary")),
        name="inproj",
    )(x, g, w)


def _queries_t(q_ref, tok):
    return q_ref[0, tok, :].T * SCALE


def _group_t(qt, kv, extra=()):
    cols = []
    for g in range(GROUP):
        h = kv * GROUP + g
        parts = [qt[h * HEAD_DIM:(h + 1) * HEAD_DIM, :], *extra]
        cols.append(parts[0] if len(parts) == 1 else jnp.concatenate(parts, axis=0))
    return jnp.concatenate(cols, axis=1).astype(BF16)


def _store_heads(o_ref, tok, outs_t, gates_t, gate_row, tq):
    pieces = []
    for kv in range(N_KV):
        for g in range(GROUP):
            h = kv * GROUP + g
            piece = outs_t[kv][:, g * tq:(g + 1) * tq]
            if gates_t is not None:
                piece = piece * gates_t[gate_row + h:gate_row + h + 1, :]
            pieces.append(piece)
    o_ref[0, tok, :] = jnp.concatenate(pieces, axis=0).T


def _gates_t(gl_ref, tok):
    return jax.nn.sigmoid(gl_ref[0, tok, :].T)


def _window_kernel(*refs, window, n_prev, q_blocks, use_sink, gate_row):
    refs = list(refs)
    q_ref, k_ref, v_ref = refs[:3]
    rest = refs[3:]
    sink_ref = rest.pop(0) if use_sink else None
    gl_ref = rest.pop(0) if gate_row is not None else None
    o_ref, kb_ref, vt_ref = rest

    blk = ATTN_BLOCK
    cols = GROUP * blk
    span = (n_prev + 1) * blk
    n = pl.program_id(1)

    @pl.when(n == 0)
    def _():
        vt = v_ref[0].T
        for kv in range(N_KV):
            kb_ref[kv] = k_ref[0, :, kv * HEAD_DIM:(kv + 1) * HEAD_DIM].astype(BF16)
            vt_ref[kv] = vt[kv * HEAD_DIM:(kv + 1) * HEAD_DIM, :].astype(BF16)

    krow = lax.broadcasted_iota(jnp.int32, (span, cols), 0)
    qcol = lax.broadcasted_iota(jnp.int32, (span, cols), 1) & (blk - 1)
    for j in range(q_blocks):
        nb = n * q_blocks + j
        tok = slice(j * blk, (j + 1) * blk)
        start = pl.multiple_of(jnp.maximum(nb - n_prev, 0) * blk, blk)
        diff = (nb * blk + qcol) - (start + krow)
        mask = (diff >= 0) & (diff < window)
        qt = _queries_t(q_ref, tok)
        outs = []
        for kv in range(N_KV):
            s = jnp.dot(kb_ref[kv, pl.ds(start, span), :], _group_t(qt, kv),
                        preferred_element_type=F32)
            s = jnp.where(mask, s, NEG_INF)
            m = jnp.max(s, axis=0, keepdims=True)
            if use_sink:
                sk = jnp.concatenate([jnp.full((1, blk), sink_ref[kv * GROUP + g], F32)
                                      for g in range(GROUP)], axis=1)
                m = jnp.maximum(m, sk)
                p = jnp.exp(s - m)
                denom = jnp.sum(p, axis=0, keepdims=True) + jnp.exp(sk - m)
            else:
                p = jnp.exp(s - m)
                denom = jnp.sum(p, axis=0, keepdims=True)
            o = jnp.dot(vt_ref[kv, :, pl.ds(start, span)], p.astype(BF16),
                        preferred_element_type=F32)
            outs.append(o * (1.0 / denom))
        gates = _gates_t(gl_ref, tok) if gate_row is not None else None
        _store_heads(o_ref, tok, outs, gates, gate_row, blk)


def _window_attention(proj, q_col, k_col, v_col, window, sinks=None, gate_row=None, *,
                      q_blocks=4):
    b, s, _ = proj.shape
    n_prev = -(-(window - 1) // ATTN_BLOCK)
    blk = q_blocks * ATTN_BLOCK
    in_specs = [
        pl.BlockSpec((1, blk, D_ATT), lambda i, n: (i, n, q_col // D_ATT)),
        pl.BlockSpec((1, s, D_KV), lambda i, n: (i, 0, k_col // D_KV)),
        pl.BlockSpec((1, s, D_KV), lambda i, n: (i, 0, v_col // D_KV)),
    ]
    args = [proj, proj, proj]
    if sinks is not None:
        in_specs.append(pl.BlockSpec(memory_space=pltpu.SMEM))
        args.append(sinks)
    if gate_row is not None:
        in_specs.append(pl.BlockSpec((1, blk, LANES), lambda i, n: (i, n, COL_GL // LANES)))
        args.append(proj)
    return pl.pallas_call(
        functools.partial(_window_kernel, window=window, n_prev=n_prev, q_blocks=q_blocks,
                          use_sink=sinks is not None, gate_row=gate_row),
        grid=(b, s // blk),
        in_specs=in_specs,
        out_specs=pl.BlockSpec((1, blk, D_ATT), lambda i, n: (i, n, 0)),
        out_shape=jax.ShapeDtypeStruct((b, s, D_ATT), F32),
        scratch_shapes=[pltpu.VMEM((N_KV, s, HEAD_DIM), BF16),
                        pltpu.VMEM((N_KV, HEAD_DIM, s), BF16)],
        compiler_params=_params(("parallel", "arbitrary")),
        name=f"window{window}",
    )(*args)


def _compress_kernel(k_ref, v_ref, pos_ref, w1_ref, b1_ref, w2_ref, kc_ref, vc_ref):
    n_rows = kc_ref.shape[1]
    for i, (src_ref, dst_ref) in enumerate(((k_ref, kc_ref), (v_ref, vc_ref))):
        lo = jnp.zeros((n_rows, N_KV * CMP_HIDDEN), F32)
        hi = jnp.zeros((n_rows, N_KV * CMP_HIDDEN), F32)
        for l in range(CMP_STRIDE):
            x = src_ref[0, pl.ds(l, n_rows, stride=CMP_STRIDE), :]
            lo = lo + jnp.dot((x + pos_ref[i, l:l + 1, :]).astype(BF16), w1_ref[i, l],
                              preferred_element_type=F32)
            hi = hi + jnp.dot((x + pos_ref[i, CMP_STRIDE + l:CMP_STRIDE + l + 1, :]).astype(BF16),
                              w1_ref[i, CMP_STRIDE + l], preferred_element_type=F32)
        hid = lo + pltpu.roll(hi, n_rows - 1, 0) + b1_ref[i]
        act = jax.nn.gelu(hid, approximate=True).astype(BF16)
        dst_ref[0] = jnp.dot(act, w2_ref[i], preferred_element_type=F32)


def _compress(proj, pos, w1, b1, w2):
    b, s, _ = proj.shape
    n_rows = s // CMP_STRIDE
    out_spec = pl.BlockSpec((1, n_rows, D_KV), lambda i: (i, 0, 0))
    out_shape = jax.ShapeDtypeStruct((b, n_rows, D_KV), F32)
    return pl.pallas_call(
        _compress_kernel,
        grid=(b,),
        in_specs=[
            pl.BlockSpec((1, s, D_KV), lambda i: (i, 0, COL_KC // D_KV)),
            pl.BlockSpec((1, s, D_KV), lambda i: (i, 0, COL_VC // D_KV)),
            pl.BlockSpec(pos.shape, lambda i: (0, 0, 0)),
            pl.BlockSpec(w1.shape, lambda i: (0, 0, 0, 0)),
            pl.BlockSpec(b1.shape, lambda i: (0, 0, 0)),
            pl.BlockSpec(w2.shape, lambda i: (0, 0, 0)),
        ],
        out_specs=[out_spec, out_spec],
        out_shape=[out_shape, out_shape],
        compiler_params=_params(("parallel",)),
        name="compress",
    )(proj, proj, pos, w1, b1, w2)


def _cmp_attn_kernel(q_ref, kc_ref, vc_ref, gl_ref, ov_ref, o_ref, bias_ref, *, tq, n_sel):
    n = pl.program_id(1)
    ncp = kc_ref.shape[1]
    cols = GROUP * tq
    pos_c = n * tq + (lax.broadcasted_iota(jnp.int32, (ncp, cols), 1) & (tq - 1))
    cend = lax.broadcasted_iota(jnp.int32, (ncp, cols), 0) * CMP_STRIDE + (CMP_LEN - 1)
    cmask = cend <= pos_c

    pos_s = n * tq + lax.broadcasted_iota(jnp.int32, (n_sel, tq), 1)
    bid = lax.broadcasted_iota(jnp.int32, (n_sel, tq), 0)
    cur = pos_s // SEL_BLOCK
    forced = (bid == 0) | (bid == cur) | (bid == cur - 1)
    valid = bid * SEL_BLOCK <= pos_s

    tok = slice(None)
    qt = _queries_t(q_ref, tok)
    vct = vc_ref[0].T
    outs = []
    for kv in range(N_KV):
        kc = kc_ref[0, :, kv * HEAD_DIM:(kv + 1) * HEAD_DIM].astype(BF16)
        s = jnp.dot(kc, _group_t(qt, kv), preferred_element_type=F32)
        s = jnp.where(cmask, s, NEG_INF)
        m = jnp.max(s, axis=0, keepdims=True)
        p = jnp.exp(s - m)
        p = p * (1.0 / jnp.sum(p, axis=0, keepdims=True))
        p = jnp.where(cmask, p, 0.0)
        outs.append(jnp.dot(vct[kv * HEAD_DIM:(kv + 1) * HEAD_DIM, :].astype(BF16),
                            p.astype(BF16), preferred_element_type=F32))
        psum = p[:, 0:tq]
        for g in range(1, GROUP):
            psum = psum + p[:, g * tq:(g + 1) * tq]

        imp = jnp.dot(ov_ref[...], psum, preferred_element_type=F32,
                      precision=lax.Precision.HIGHEST)
        imp = jnp.where(forced, FORCED_SCORE, imp)
        imp = jnp.where(valid, imp, NEG_INF)
        rank = jnp.zeros((n_sel, tq), jnp.int32)
        for i in range(n_sel):
            row = imp[i:i + 1, :]
            ahead = (row > imp) | ((row == imp) & (bid > i))
            rank = rank + ahead.astype(jnp.int32)
        bias_ref[0, kv * n_sel:(kv + 1) * n_sel, :] = jnp.where(rank < TOP_N, 0.0, NEG_INF)
    _store_heads(o_ref, tok, outs, _gates_t(gl_ref, tok), 0, tq)


def _cmp_attention(proj, kc, vc, overlap, *, tq=256):
    b, s, _ = proj.shape
    n_sel = overlap.shape[0]
    ncp = kc.shape[1]
    return pl.pallas_call(
        functools.partial(_cmp_attn_kernel, tq=tq, n_sel=n_sel),
        grid=(b, s // tq),
        in_specs=[
            pl.BlockSpec((1, tq, D_ATT), lambda i, n: (i, n, COL_QB // D_ATT)),
            pl.BlockSpec((1, ncp, D_KV), lambda i, n: (i, 0, 0)),
            pl.BlockSpec((1, ncp, D_KV), lambda i, n: (i, 0, 0)),
            pl.BlockSpec((1, tq, LANES), lambda i, n: (i, n, COL_GL // LANES)),
            pl.BlockSpec(overlap.shape, lambda i, n: (0, 0)),
        ],
        out_specs=[
            pl.BlockSpec((1, tq, D_ATT), lambda i, n: (i, n, 0)),
            pl.BlockSpec((1, N_KV * n_sel, tq), lambda i, n: (i, 0, n)),
        ],
        out_shape=[
            jax.ShapeDtypeStruct((b, s, D_ATT), F32),
            jax.ShapeDtypeStruct((b, N_KV * n_sel, s), F32),
        ],
        compiler_params=_params(("parallel", "arbitrary")),
        name="cmp_attn",
    )(proj, kc, vc, proj, overlap)


def _sel_attn_kernel(q_ref, k_ref, v_ref, bias_ref, gl_ref, o_ref,
                     qa_ref, ka_ref, vt_ref, *, tq, tk, chains, n_sel, gate_row):
    n = pl.program_id(1)
    s_len = k_ref.shape[1]
    cols = GROUP * tq
    n_pad = LANES - HEAD_DIM - n_sel

    @pl.when(n == 0)
    def _():
        kpos = lax.broadcasted_iota(jnp.int32, (s_len, n_sel), 0)
        blk = lax.broadcasted_iota(jnp.int32, (s_len, n_sel), 1)
        onehot = (kpos // SEL_BLOCK == blk).astype(F32)
        zeros = jnp.zeros((s_len, n_pad), F32)
        vt = v_ref[0].T
        for kv in range(N_KV):
            k = k_ref[0, :, kv * HEAD_DIM:(kv + 1) * HEAD_DIM]
            ka_ref[kv] = jnp.concatenate([k, onehot, zeros], axis=1).astype(BF16)
            for t in range(s_len // tk):
                vt_ref[kv, t] = vt[kv * HEAD_DIM:(kv + 1) * HEAD_DIM,
                                   t * tk:(t + 1) * tk].astype(BF16)

    tok = slice(None)
    qt = _queries_t(q_ref, tok)
    zq = jnp.zeros((n_pad, tq), F32)
    for kv in range(N_KV):
        bias = bias_ref[0, kv * n_sel:(kv + 1) * n_sel, :]
        qa_ref[kv] = _group_t(qt, kv, extra=(bias, zq))
    ccols = cols // chains
    qpos = n * tq + (lax.broadcasted_iota(jnp.int32, (tk, ccols), 1) & (tq - 1))
    krow = lax.broadcasted_iota(jnp.int32, (tk, ccols), 0)

    def step(kt, state, causal):
        k0 = pl.multiple_of(kt * tk, tk)
        new_state = []
        for c, (m_old, l_old, acc_old) in enumerate(state):
            kv, part = divmod(c, chains)
            cs = slice(part * ccols, (part + 1) * ccols)
            s = jnp.dot(ka_ref[kv, pl.ds(k0, tk), :], qa_ref[kv, :, cs],
                        preferred_element_type=F32)
            if causal:
                s = jnp.where(k0 + krow <= qpos, s, NEG_INF)
            m_new = jnp.maximum(m_old, jnp.max(s, axis=0, keepdims=True))
            alpha = jnp.exp(m_old - m_new)
            p = jnp.exp(s - m_new)
            l_new = alpha * l_old + jnp.sum(p, axis=0, keepdims=True)
            acc_new = alpha * acc_old + jnp.dot(vt_ref[kv, kt], p.astype(BF16),
                                                preferred_element_type=F32)
            new_state.append((m_new, l_new, acc_new))
        return tuple(new_state)

    init = tuple((jnp.full((1, ccols), NEG_INF, F32), jnp.zeros((1, ccols), F32),
                  jnp.zeros((HEAD_DIM, ccols), F32)) for _ in range(N_KV * chains))
    n_full = (n * tq) // tk
    state = lax.fori_loop(0, n_full, functools.partial(step, causal=False), init)
    state = step(n_full, state, causal=True)

    outs = []
    for kv in range(N_KV):
        parts = [acc * (1.0 / l) for _, l, acc in state[kv * chains:(kv + 1) * chains]]
        outs.append(jnp.concatenate(parts, axis=1))
    _store_heads(o_ref, tok, outs, _gates_t(gl_ref, tok), gate_row, tq)


def _sel_attention(proj, bias, *, tq=256, tk=512, chains=2):
    b, s, _ = proj.shape
    n_sel = bias.shape[1] // N_KV
    cols = GROUP * tq
    return pl.pallas_call(
        functools.partial(_sel_attn_kernel, tq=tq, tk=tk, chains=chains, n_sel=n_sel,
                          gate_row=N_HEADS),
        grid=(b, s // tq),
        in_specs=[
            pl.BlockSpec((1, tq, D_ATT), lambda i, n: (i, n, COL_QB // D_ATT)),
            pl.BlockSpec((1, s, D_KV), lambda i, n: (i, 0, COL_KS // D_KV)),
            pl.BlockSpec((1, s, D_KV), lambda i, n: (i, 0, COL_VS // D_KV)),
            pl.BlockSpec((1, N_KV * n_sel, tq), lambda i, n: (i, 0, n)),
            pl.BlockSpec((1, tq, LANES), lambda i, n: (i, n, COL_GL // LANES)),
        ],
        out_specs=pl.BlockSpec((1, tq, D_ATT), lambda i, n: (i, n, 0)),
        out_shape=jax.ShapeDtypeStruct((b, s, D_ATT), F32),
        scratch_shapes=[
            pltpu.VMEM((N_KV, LANES, cols), BF16),
            pltpu.VMEM((N_KV, s, LANES), BF16),
            pltpu.VMEM((N_KV, s // tk, HEAD_DIM, tk), BF16),
        ],
        compiler_params=_params(("parallel", "arbitrary")),
        name="sel_attn",
    )(proj, proj, proj, bias, proj)


def _rglru_kernel(xr_ref, xg_ref, cw_ref, cb_ref, wa_ref, ba_ref, wx_ref, bx_ref, lam_ref,
                  o_ref, xbuf_ref, h_ref, *, ts):
    pad = SUBLANES

    @pl.when(pl.program_id(1) == 0)
    def _():
        xbuf_ref[0:pad, :] = jnp.zeros((pad, RNN_WIDTH), F32)
        h_ref[...] = jnp.zeros_like(h_ref)

    xr = xr_ref[0]
    xbuf_ref[pad:pad + ts, :] = xr
    xc = cb_ref[...] + cw_ref[CONV_WIDTH - 1:CONV_WIDTH, :] * xr
    for w in range(CONV_WIDTH - 1):
        shift = CONV_WIDTH - 1 - w
        xc = xc + cw_ref[w:w + 1, :] * xbuf_ref[pl.ds(pad - shift, ts), :]
    xbuf_ref[0:pad, :] = xr[ts - pad:ts, :]

    ra, rx = [], []
    for blk in range(RNN_WIDTH // MXU_DIM):
        xs = xc[:, blk * MXU_DIM:(blk + 1) * MXU_DIM].astype(BF16)
        ra.append(jnp.dot(xs, wa_ref[blk], preferred_element_type=F32))
        rx.append(jnp.dot(xs, wx_ref[blk], preferred_element_type=F32))
    r = jax.nn.sigmoid(jnp.concatenate(ra, axis=1) + ba_ref[...])
    gi = jax.nn.sigmoid(jnp.concatenate(rx, axis=1) + bx_ref[...])
    nl = -lam_ref[...]
    softplus = jnp.maximum(nl, 0.0) + jnp.log1p(jnp.exp(-jnp.abs(nl)))
    log_a = -LRU_C * r * softplus
    a = jnp.exp(log_a)
    u = jnp.sqrt(-jnp.tanh(log_a) * (a * a + 1.0)) * (gi * xc)

    row = lax.broadcasted_iota(jnp.int32, (ts, RNN_WIDTH), 0)
    d = 1
    while d < ts:
        keep = row >= d
        a_prev = jnp.where(keep, pltpu.roll(a, d, 0), 1.0)
        u_prev = jnp.where(keep, pltpu.roll(u, d, 0), 0.0)
        u = a * u_prev + u
        a = a * a_prev
        d *= 2
    h = a * h_ref[0:1, :] + u
    h_ref[0:1, :] = h[ts - 1:ts, :]
    o_ref[0] = jax.nn.gelu(xg_ref[0], approximate=True) * h


def _rglru(proj, conv_w, conv_b, wa, ba, wx, bx, lam, *, ts=256):
    b, s, _ = proj.shape
    c = RNN_WIDTH
    vec = pl.BlockSpec((1, c), lambda i, n: (0, 0))
    wspec = pl.BlockSpec(wa.shape, lambda i, n: (0, 0, 0))
    return pl.pallas_call(
        functools.partial(_rglru_kernel, ts=ts),
        grid=(b, s // ts),
        in_specs=[
            pl.BlockSpec((1, ts, c), lambda i, n: (i, n, COL_XR // c)),
            pl.BlockSpec((1, ts, c), lambda i, n: (i, n, COL_XG // c)),
            pl.BlockSpec((CONV_WIDTH, c), lambda i, n: (0, 0)),
            vec, wspec, vec, wspec, vec, vec,
        ],
        out_specs=pl.BlockSpec((1, ts, c), lambda i, n: (i, n, 0)),
        out_shape=jax.ShapeDtypeStruct((b, s, c), F32),
        scratch_shapes=[pltpu.VMEM((ts + SUBLANES, c), F32), pltpu.VMEM((SUBLANES, c), F32)],
        compiler_params=_params(("parallel", "arbitrary")),
        name="rglru",
    )(proj, proj, conv_w, conv_b, wa, ba, wx, bx, lam)


def _outproj_kernel(x_ref, ya_ref, oc_ref, os_ref, ow_ref, yc_ref, gn_ref, w_ref, o_ref):
    ya = _rms(ya_ref[...], gn_ref[:, 0:D_ATT]).astype(BF16)
    yb = oc_ref[...] + os_ref[...] + ow_ref[...]
    yb = _rms(yb, gn_ref[:, D_ATT:2 * D_ATT]).astype(BF16)
    yc = _rms(yc_ref[...], gn_ref[:, 2 * D_ATT:]).astype(BF16)
    y = jnp.dot(ya, w_ref[0:D_ATT, :], preferred_element_type=F32)
    y = y + jnp.dot(yb, w_ref[D_ATT:2 * D_ATT, :], preferred_element_type=F32)
    y = y + jnp.dot(yc, w_ref[2 * D_ATT:, :], preferred_element_type=F32)
    o_ref[...] = x_ref[...] + y


def _outproj(x, ya, oc, osl, ow, yc, gn, w, *, tm=256):
    t, d = x.shape
    att = pl.BlockSpec((tm, D_ATT), lambda i: (i, 0))
    return pl.pallas_call(
        _outproj_kernel,
        grid=(t // tm,),
        in_specs=[
            pl.BlockSpec((tm, d), lambda i: (i, 0)),
            att, att, att, att,
            pl.BlockSpec((tm, RNN_WIDTH), lambda i: (i, 0)),
            pl.BlockSpec((1, gn.shape[1]), lambda i: (0, 0)),
            pl.BlockSpec(w.shape, lambda i: (0, 0)),
        ],
        out_specs=pl.BlockSpec((tm, d), lambda i: (i, 0)),
        out_shape=jax.ShapeDtypeStruct((t, d), F32),
        compiler_params=_params(("parallel",)),
        name="outproj",
    )(x, ya, oc, osl, ow, yc, gn, w)


def _reorder_w_in(w):
    offs = np.concatenate([[0], np.cumsum(_REF_SPLITS)])
    qa, ka, va, qb, kc, vc, ks, vs, kw, vw, gl, xr, xg = [
        w[:, int(offs[i]):int(offs[i + 1])] for i in range(len(_REF_SPLITS))]
    gl = jnp.pad(gl, ((0, 0), (0, LANES - gl.shape[1])))
    return jnp.concatenate([xr, xg, qa, qb, ka, va, kc, vc, ks, vs, kw, vw, gl], axis=1)


def _block_diag(w, per):
    *lead, n, r, c = w.shape
    w = w.reshape(*lead, n // per, per, r, c)
    eye = jnp.eye(per, dtype=w.dtype)
    out = jnp.einsum('...prc,pq->...prqc', w, eye)
    return out.reshape(*lead, n // per, per * r, per * c)


def _compress_weights(cmp_pos, w1, b1, w2):
    pos = jnp.tile(cmp_pos, (1, 1, N_KV))
    w1 = w1.reshape(2, CMP_LEN, 1, HEAD_DIM, CMP_HIDDEN)
    w1 = _block_diag(jnp.broadcast_to(w1, (2, CMP_LEN, N_KV, HEAD_DIM, CMP_HIDDEN)), N_KV)
    w1 = w1.reshape(2, CMP_LEN, N_KV * HEAD_DIM, N_KV * CMP_HIDDEN)
    w2 = _block_diag(jnp.broadcast_to(w2[:, None], (2, N_KV, CMP_HIDDEN, HEAD_DIM)), N_KV)
    w2 = w2.reshape(2, N_KV * CMP_HIDDEN, N_KV * HEAD_DIM)
    b1 = jnp.tile(b1[:, None, :], (1, 1, N_KV))
    return pos, w1.astype(BF16), b1, w2.astype(BF16)


def _overlap_matrix(seq, n_cmp_padded):
    n_c = (seq - CMP_LEN) // CMP_STRIDE + 1
    n_sel = seq // SEL_BLOCK
    cs = np.arange(n_c)[:, None] * CMP_STRIDE
    ss = np.arange(n_sel)[None, :] * SEL_BLOCK
    ov = np.clip(np.minimum(cs + CMP_LEN, ss + SEL_BLOCK) - np.maximum(cs, ss), 0, None)
    out = np.zeros((n_cmp_padded, n_sel), np.float32)
    out[:n_c] = ov / CMP_LEN
    return jnp.asarray(out.T)


def kernel(x, ffn1_norm, ffn1_w_gate, ffn1_w_up, ffn1_w_down, mix_norm, w_in, swa_sinks, cmp_pos, cmp_w1, cmp_b1, cmp_w2, conv_w, conv_b, lru_wa, lru_ba, lru_wx, lru_bx, lru_lambda, group_norm, w_out, ffn2_norm, ffn2_w_gate, ffn2_w_up, ffn2_w_down, final_norm):
    b, s, d = x.shape
    depth = w_in.shape[0]
    t = b * s
    overlap = _overlap_matrix(s, s // CMP_STRIDE)
    gate_tile = MXU_DIM // RNN_BLOCK_WIDTH
    xt = x.reshape(t, d)
    for l in range(depth):
        xt = _ffn(xt, ffn1_norm[l][None], ffn1_w_gate[l].astype(BF16),
                  ffn1_w_up[l].astype(BF16), ffn1_w_down[l].astype(BF16))

        proj = _inproj(xt, mix_norm[l][None], _reorder_w_in(w_in[l]).astype(BF16))
        proj = proj.reshape(b, s, D_PROJ)

        ya = _window_attention(proj, COL_QA, COL_KA, COL_VA, SWA_WINDOW, sinks=swa_sinks[l])

        kc, vc = _compress(proj, *_compress_weights(cmp_pos[l], cmp_w1[l], cmp_b1[l], cmp_w2[l]))
        o_cmp, bias = _cmp_attention(proj, kc, vc, overlap)
        o_slc = _sel_attention(proj, bias)
        o_win = _window_attention(proj, COL_QB, COL_KW, COL_VW, NSA_WINDOW,
                                  gate_row=2 * N_HEADS, q_blocks=2)

        yc = _rglru(proj, conv_w[l], conv_b[l][None],
                    _block_diag(lru_wa[l], gate_tile).astype(BF16), lru_ba[l][None],
                    _block_diag(lru_wx[l], gate_tile).astype(BF16), lru_bx[l][None],
                    lru_lambda[l][None])

        xt = _outproj(xt, ya.reshape(t, D_ATT), o_cmp.reshape(t, D_ATT),
                      o_slc.reshape(t, D_ATT), o_win.reshape(t, D_ATT),
                      yc.reshape(t, RNN_WIDTH), group_norm[l][None], w_out[l].astype(BF16))

        last = l == depth - 1
        xt = _ffn(xt, ffn2_norm[l][None], ffn2_w_gate[l].astype(BF16),
                  ffn2_w_up[l].astype(BF16), ffn2_w_down[l].astype(BF16),
                  final_g=final_norm[None] if last else None)
    return xt.reshape(b, s, d)
```

```python
import functools

import jax
import jax.numpy as jnp
import numpy as np
from jax import lax
from jax.experimental import pallas as pl
from jax.experimental.pallas import tpu as pltpu

F32 = jnp.float32
BF16 = jnp.bfloat16

D_MODEL = 2048
HEAD_DIM = 64
D_FF = 5632
N_HEADS = 8
N_KV = 2
GROUP = N_HEADS // N_KV
D_ATT = N_HEADS * HEAD_DIM
D_KV = N_KV * HEAD_DIM
SWA_WINDOW = 128
NSA_WINDOW = 512
ATTN_BLOCK = 128
CMP_LEN = 32
CMP_STRIDE = 16
CMP_HIDDEN = 256
SEL_BLOCK = 64
TOP_N = 16
RNN_WIDTH = 1024
RNN_BLOCK_WIDTH = 64
CONV_WIDTH = 4
LRU_C = 8.0
RMS_EPS = 1e-6
NEG_INF = -1e30
FORCED_SCORE = 1e4
SCALE = HEAD_DIM ** -0.5

LANES = 128
SUBLANES = 8
MXU_DIM = 256
VMEM_LIMIT = 56 * 1024 * 1024

COL_XR = 0
COL_XG = 1024
COL_QA = 2048
COL_QB = 2560
COL_KA = 3072
COL_VA = 3200
COL_KC = 3328
COL_VC = 3456
COL_KS = 3584
COL_VS = 3712
COL_KW = 3840
COL_VW = 3968
COL_GL = 4096
D_PROJ = 4224

_REF_SPLITS = (D_ATT, D_KV, D_KV, D_ATT, D_KV, D_KV, D_KV, D_KV, D_KV, D_KV, 3 * N_HEADS,
               RNN_WIDTH, RNN_WIDTH)


def _params(sem):
    return pltpu.CompilerParams(dimension_semantics=sem, vmem_limit_bytes=VMEM_LIMIT)


def _rms(x, g):
    return x * lax.rsqrt(jnp.mean(x * x, axis=-1, keepdims=True) + RMS_EPS) * g


def _cast_kernel(w_ref, o_ref):
    o_ref[...] = w_ref[...].astype(BF16)


def _to_bf16(w, *, rows):
    dep, k, n = w.shape
    spec = pl.BlockSpec((1, rows, n), lambda l, i: (l, i, 0))
    return pl.pallas_call(
        _cast_kernel,
        grid=(dep, k // rows),
        in_specs=[spec],
        out_specs=spec,
        out_shape=jax.ShapeDtypeStruct(w.shape, BF16),
        compiler_params=_params(("parallel", "parallel")),
        name="cast_bf16",
    )(w)


def _w_in_kernel(w_ref, o_ref):
    w = w_ref[0]
    offs = np.concatenate([[0], np.cumsum(_REF_SPLITS)])
    qa, ka, va, qb, kc, vc, ks, vs, kw, vw, gl, xr, xg = [
        w[:, int(offs[i]):int(offs[i + 1])] for i in range(len(_REF_SPLITS))]
    gl = jnp.concatenate([gl, jnp.zeros((w.shape[0], LANES - gl.shape[1]), F32)], axis=1)
    out = jnp.concatenate([xr, xg, qa, qb, ka, va, kc, vc, ks, vs, kw, vw, gl], axis=1)
    o_ref[0] = out.astype(BF16)


def _stage_w_in(w, *, rows=256):
    dep, k, n = w.shape
    return pl.pallas_call(
        _w_in_kernel,
        grid=(dep, k // rows),
        in_specs=[pl.BlockSpec((1, rows, n), lambda l, i: (l, i, 0))],
        out_specs=pl.BlockSpec((1, rows, D_PROJ), lambda l, i: (l, i, 0)),
        out_shape=jax.ShapeDtypeStruct((dep, k, D_PROJ), BF16),
        compiler_params=_params(("parallel", "parallel")),
        name="stage_w_in",
    )(w)


def _ffn_kernel(x_ref, g_ref, wg_ref, wu_ref, wd_ref, *rest, final_norm):
    if final_norm:
        fg_ref, o_ref, h_ref, acc_ref = rest
    else:
        o_ref, h_ref, acc_ref = rest
    j = pl.program_id(1)

    @pl.when(j == 0)
    def _():
        h_ref[...] = _rms(x_ref[...], g_ref[...]).astype(BF16)
        acc_ref[...] = jnp.zeros_like(acc_ref)

    h = h_ref[...]
    gate = jnp.dot(h, wg_ref[...], preferred_element_type=F32)
    up = jnp.dot(h, wu_ref[...], preferred_element_type=F32)
    act = (gate * jax.nn.sigmoid(gate) * up).astype(BF16)
    acc_ref[...] += jnp.dot(act, wd_ref[...], preferred_element_type=F32)

    @pl.when(j == pl.num_programs(1) - 1)
    def _():
        y = x_ref[...] + 0.5 * acc_ref[...]
        if final_norm:
            y = _rms(y, fg_ref[...])
        o_ref[...] = y


def _ffn(x, g, wg, wu, wd, layer, final_g=None, *, tm=512, tf=512):
    t, d = x.shape
    f = wg.shape[2]
    final_norm = final_g is not None
    in_specs = [
        pl.BlockSpec((tm, d), lambda i, j: (i, 0)),
        pl.BlockSpec((1, d), lambda i, j: (0, 0)),
        pl.BlockSpec((None, d, tf), lambda i, j: (layer, 0, j)),
        pl.BlockSpec((None, d, tf), lambda i, j: (layer, 0, j)),
        pl.BlockSpec((None, tf, d), lambda i, j: (layer, j, 0)),
    ]
    args = [x, g, wg, wu, wd]
    if final_norm:
        in_specs.append(pl.BlockSpec((1, d), lambda i, j: (0, 0)))
        args.append(final_g)
    return pl.pallas_call(
        functools.partial(_ffn_kernel, final_norm=final_norm),
        grid=(t // tm, f // tf),
        in_specs=in_specs,
        out_specs=pl.BlockSpec((tm, d), lambda i, j: (i, 0)),
        out_shape=jax.ShapeDtypeStruct((t, d), F32),
        scratch_shapes=[pltpu.VMEM((tm, d), BF16), pltpu.VMEM((tm, d), F32)],
        compiler_params=_params(("parallel", "arbitrary")),
        name="ffn",
    )(*args)


def _inproj_kernel(x_ref, g_ref, w_ref, o_ref, h_ref):
    @pl.when(pl.program_id(1) == 0)
    def _():
        h_ref[...] = _rms(x_ref[...], g_ref[...]).astype(BF16)

    o_ref[...] = jnp.dot(h_ref[...], w_ref[...], preferred_element_type=F32)


def _inproj(x, g, w, layer, *, tm=512, tn=1408):
    t, d = x.shape
    n = w.shape[2]
    return pl.pallas_call(
        _inproj_kernel,
        grid=(t // tm, n // tn),
        in_specs=[
            pl.BlockSpec((tm, d), lambda i, j: (i, 0)),
            pl.BlockSpec((1, d), lambda i, j: (0, 0)),
            pl.BlockSpec((None, d, tn), lambda i, j: (layer, 0, j)),
        ],
        out_specs=pl.BlockSpec((tm, tn), lambda i, j: (i, j)),
        out_shape=jax.ShapeDtypeStruct((t, n), F32),
        scratch_shapes=[pltpu.VMEM((tm, d), BF16)],
        compiler_params=_params(("parallel", "arbitrary")),
        name="inproj",
    )(x, g, w)


def _queries_t(q_ref, tok):
    return q_ref[0, tok, :].T * SCALE


def _group_t(qt, kv, extra=()):
    cols = []
    for g in range(GROUP):
        h = kv * GROUP + g
        parts = [qt[h * HEAD_DIM:(h + 1) * HEAD_DIM, :], *extra]
        cols.append(parts[0] if len(parts) == 1 else jnp.concatenate(parts, axis=0))
    return jnp.concatenate(cols, axis=1).astype(BF16)


def _store_heads(o_ref, tok, outs_t, gates_t, gate_row, tq):
    pieces = []
    for kv in range(N_KV):
        for g in range(GROUP):
            h = kv * GROUP + g
            piece = outs_t[kv][:, g * tq:(g + 1) * tq]
            if gates_t is not None:
                piece = piece * gates_t[gate_row + h:gate_row + h + 1, :]
            pieces.append(piece)
    o_ref[0, tok, :] = jnp.concatenate(pieces, axis=0).T


def _gates_t(gl_ref, tok):
    return jax.nn.sigmoid(gl_ref[0, tok, :].T)


def _window_kernel(*refs, window, n_prev, q_blocks, use_sink, gate_row):
    refs = list(refs)
    q_ref, k_ref, v_ref = refs[:3]
    rest = refs[3:]
    sink_ref = rest.pop(0) if use_sink else None
    gl_ref = rest.pop(0) if gate_row is not None else None
    o_ref, kb_ref, vt_ref = rest

    blk = ATTN_BLOCK
    cols = GROUP * blk
    span = (n_prev + 1) * blk
    n = pl.program_id(1)

    @pl.when(n == 0)
    def _():
        vt = v_ref[0].T
        for kv in range(N_KV):
            kb_ref[kv] = k_ref[0, :, kv * HEAD_DIM:(kv + 1) * HEAD_DIM].astype(BF16)
            vt_ref[kv] = vt[kv * HEAD_DIM:(kv + 1) * HEAD_DIM, :].astype(BF16)

    krow = lax.broadcasted_iota(jnp.int32, (span, cols), 0)
    qcol = lax.broadcasted_iota(jnp.int32, (span, cols), 1) & (blk - 1)
    starts, scores = [], []
    for j in range(q_blocks):
        nb = n * q_blocks + j
        start = pl.multiple_of(jnp.maximum(nb - n_prev, 0) * blk, blk)
        diff = (nb * blk + qcol) - (start + krow)
        mask = (diff >= 0) & (diff < window)
        qt = _queries_t(q_ref, slice(j * blk, (j + 1) * blk))
        starts.append(start)
        for kv in range(N_KV):
            s = jnp.dot(kb_ref[kv, pl.ds(start, span), :], _group_t(qt, kv),
                        preferred_element_type=F32)
            scores.append(jnp.where(mask, s, NEG_INF))
    probs = []
    for c, s in enumerate(scores):
        kv = c % N_KV
        m = jnp.max(s, axis=0, keepdims=True)
        if use_sink:
            sk = jnp.concatenate([jnp.full((1, blk), sink_ref[kv * GROUP + g], F32)
                                  for g in range(GROUP)], axis=1)
            m = jnp.maximum(m, sk)
            p = jnp.exp(s - m)
            denom = jnp.sum(p, axis=0, keepdims=True) + jnp.exp(sk - m)
        else:
            p = jnp.exp(s - m)
            denom = jnp.sum(p, axis=0, keepdims=True)
        probs.append((p.astype(BF16), denom))
    for j in range(q_blocks):
        tok = slice(j * blk, (j + 1) * blk)
        outs = []
        for kv in range(N_KV):
            p, denom = probs[j * N_KV + kv]
            o = jnp.dot(vt_ref[kv, :, pl.ds(starts[j], span)], p,
                        preferred_element_type=F32)
            outs.append(o * (1.0 / denom))
        gates = _gates_t(gl_ref, tok) if gate_row is not None else None
        _store_heads(o_ref, tok, outs, gates, gate_row, blk)


def _window_attention(proj, q_col, k_col, v_col, window, sinks=None, gate_row=None, *,
                      q_blocks=4):
    b, s, _ = proj.shape
    n_prev = -(-(window - 1) // ATTN_BLOCK)
    blk = q_blocks * ATTN_BLOCK
    in_specs = [
        pl.BlockSpec((1, blk, D_ATT), lambda i, n: (i, n, q_col // D_ATT)),
        pl.BlockSpec((1, s, D_KV), lambda i, n: (i, 0, k_col // D_KV)),
        pl.BlockSpec((1, s, D_KV), lambda i, n: (i, 0, v_col // D_KV)),
    ]
    args = [proj, proj, proj]
    if sinks is not None:
        in_specs.append(pl.BlockSpec(memory_space=pltpu.SMEM))
        args.append(sinks)
    if gate_row is not None:
        in_specs.append(pl.BlockSpec((1, blk, LANES), lambda i, n: (i, n, COL_GL // LANES)))
        args.append(proj)
    return pl.pallas_call(
        functools.partial(_window_kernel, window=window, n_prev=n_prev, q_blocks=q_blocks,
                          use_sink=sinks is not None, gate_row=gate_row),
        grid=(b, s // blk),
        in_specs=in_specs,
        out_specs=pl.BlockSpec((1, blk, D_ATT), lambda i, n: (i, n, 0)),
        out_shape=jax.ShapeDtypeStruct((b, s, D_ATT), F32),
        scratch_shapes=[pltpu.VMEM((N_KV, s, HEAD_DIM), BF16),
                        pltpu.VMEM((N_KV, HEAD_DIM, s), BF16)],
        compiler_params=_params(("parallel", "arbitrary")),
        name=f"window{window}",
    )(*args)


def _compress_kernel(k_ref, v_ref, pos_ref, w1_ref, b1_ref, w2_ref, kc_ref, vc_ref):
    n_rows = kc_ref.shape[1]
    for i, (src_ref, dst_ref) in enumerate(((k_ref, kc_ref), (v_ref, vc_ref))):
        lo = jnp.zeros((n_rows, N_KV * CMP_HIDDEN), F32)
        hi = jnp.zeros((n_rows, N_KV * CMP_HIDDEN), F32)
        for l in range(CMP_STRIDE):
            x = src_ref[0, pl.ds(l, n_rows, stride=CMP_STRIDE), :]
            lo = lo + jnp.dot((x + pos_ref[i, l:l + 1, :]).astype(BF16), w1_ref[i, l],
                              preferred_element_type=F32)
            hi = hi + jnp.dot((x + pos_ref[i, CMP_STRIDE + l:CMP_STRIDE + l + 1, :]).astype(BF16),
                              w1_ref[i, CMP_STRIDE + l], preferred_element_type=F32)
        hid = lo + pltpu.roll(hi, n_rows - 1, 0) + b1_ref[i]
        act = jax.nn.gelu(hid, approximate=True).astype(BF16)
        dst_ref[0] = jnp.dot(act, w2_ref[i], preferred_element_type=F32)


def _compress(proj, pos, w1, b1, w2):
    b, s, _ = proj.shape
    n_rows = s // CMP_STRIDE
    out_spec = pl.BlockSpec((1, n_rows, D_KV), lambda i: (i, 0, 0))
    out_shape = jax.ShapeDtypeStruct((b, n_rows, D_KV), F32)
    return pl.pallas_call(
        _compress_kernel,
        grid=(b,),
        in_specs=[
            pl.BlockSpec((1, s, D_KV), lambda i: (i, 0, COL_KC // D_KV)),
            pl.BlockSpec((1, s, D_KV), lambda i: (i, 0, COL_VC // D_KV)),
            pl.BlockSpec(pos.shape, lambda i: (0, 0, 0)),
            pl.BlockSpec(w1.shape, lambda i: (0, 0, 0, 0)),
            pl.BlockSpec(b1.shape, lambda i: (0, 0, 0)),
            pl.BlockSpec(w2.shape, lambda i: (0, 0, 0)),
        ],
        out_specs=[out_spec, out_spec],
        out_shape=[out_shape, out_shape],
        compiler_params=_params(("parallel",)),
        name="compress",
    )(proj, proj, pos, w1, b1, w2)


def _cmp_attn_kernel(q_ref, kc_ref, vc_ref, gl_ref, ov_ref, o_ref, bias_ref, *, tq, n_sel):
    n = pl.program_id(1)
    ncp = kc_ref.shape[1]
    cols = GROUP * tq
    pos_c = n * tq + (lax.broadcasted_iota(jnp.int32, (ncp, cols), 1) & (tq - 1))
    cend = lax.broadcasted_iota(jnp.int32, (ncp, cols), 0) * CMP_STRIDE + (CMP_LEN - 1)
    cmask = cend <= pos_c

    pos_s = n * tq + lax.broadcasted_iota(jnp.int32, (n_sel, tq), 1)
    bid = lax.broadcasted_iota(jnp.int32, (n_sel, tq), 0)
    cur = pos_s // SEL_BLOCK
    forced = (bid == 0) | (bid == cur) | (bid == cur - 1)
    valid = bid * SEL_BLOCK <= pos_s

    tok = slice(None)
    qt = _queries_t(q_ref, tok)
    vct = vc_ref[0].T
    outs = []
    for kv in range(N_KV):
        kc = kc_ref[0, :, kv * HEAD_DIM:(kv + 1) * HEAD_DIM].astype(BF16)
        s = jnp.dot(kc, _group_t(qt, kv), preferred_element_type=F32)
        s = jnp.where(cmask, s, NEG_INF)
        m = jnp.max(s, axis=0, keepdims=True)
        p = jnp.exp(s - m)
        p = p * (1.0 / jnp.sum(p, axis=0, keepdims=True))
        p = jnp.where(cmask, p, 0.0)
        outs.append(jnp.dot(vct[kv * HEAD_DIM:(kv + 1) * HEAD_DIM, :].astype(BF16),
                            p.astype(BF16), preferred_element_type=F32))
        psum = p[:, 0:tq]
        for g in range(1, GROUP):
            psum = psum + p[:, g * tq:(g + 1) * tq]

        imp = jnp.dot(ov_ref[...], psum, preferred_element_type=F32,
                      precision=lax.Precision.HIGHEST)
        imp = jnp.where(forced, FORCED_SCORE, imp)
        imp = jnp.where(valid, imp, NEG_INF)
        rank = jnp.zeros((n_sel, tq), jnp.int32)
        for i in range(n_sel):
            row = imp[i:i + 1, :]
            ahead = (row > imp) | ((row == imp) & (bid > i))
            rank = rank + ahead.astype(jnp.int32)
        bias_ref[0, kv * n_sel:(kv + 1) * n_sel, :] = jnp.where(rank < TOP_N, 0.0, NEG_INF)
    _store_heads(o_ref, tok, outs, _gates_t(gl_ref, tok), 0, tq)


def _cmp_attention(proj, kc, vc, overlap, *, tq=256):
    b, s, _ = proj.shape
    n_sel = overlap.shape[0]
    ncp = kc.shape[1]
    return pl.pallas_call(
        functools.partial(_cmp_attn_kernel, tq=tq, n_sel=n_sel),
        grid=(b, s // tq),
        in_specs=[
            pl.BlockSpec((1, tq, D_ATT), lambda i, n: (i, n, COL_QB // D_ATT)),
            pl.BlockSpec((1, ncp, D_KV), lambda i, n: (i, 0, 0)),
            pl.BlockSpec((1, ncp, D_KV), lambda i, n: (i, 0, 0)),
            pl.BlockSpec((1, tq, LANES), lambda i, n: (i, n, COL_GL // LANES)),
            pl.BlockSpec(overlap.shape, lambda i, n: (0, 0)),
        ],
        out_specs=[
            pl.BlockSpec((1, tq, D_ATT), lambda i, n: (i, n, 0)),
            pl.BlockSpec((1, N_KV * n_sel, tq), lambda i, n: (i, 0, n)),
        ],
        out_shape=[
            jax.ShapeDtypeStruct((b, s, D_ATT), F32),
            jax.ShapeDtypeStruct((b, N_KV * n_sel, s), F32),
        ],
        compiler_params=_params(("parallel", "arbitrary")),
        name="cmp_attn",
    )(proj, kc, vc, proj, overlap)


def _sel_attn_kernel(q_ref, k_ref, v_ref, bias_ref, gl_ref, o_ref,
                     qa_ref, ka_ref, vt_ref, *, tq, tk, chains, n_sel, gate_row):
    n = pl.program_id(1)
    s_len = k_ref.shape[1]
    cols = GROUP * tq
    n_pad = LANES - HEAD_DIM - n_sel

    @pl.when(n == 0)
    def _():
        kpos = lax.broadcasted_iota(jnp.int32, (s_len, n_sel), 0)
        blk = lax.broadcasted_iota(jnp.int32, (s_len, n_sel), 1)
        onehot = (kpos // SEL_BLOCK == blk).astype(F32)
        zeros = jnp.zeros((s_len, n_pad), F32)
        vt = v_ref[0].T
        for kv in range(N_KV):
            k = k_ref[0, :, kv * HEAD_DIM:(kv + 1) * HEAD_DIM]
            ka_ref[kv] = jnp.concatenate([k, onehot, zeros], axis=1).astype(BF16)
            for t in range(s_len // tk):
                vt_ref[kv, t] = vt[kv * HEAD_DIM:(kv + 1) * HEAD_DIM,
                                   t * tk:(t + 1) * tk].astype(BF16)

    tok = slice(None)
    qt = _queries_t(q_ref, tok)
    zq = jnp.zeros((n_pad, tq), F32)
    for kv in range(N_KV):
        bias = bias_ref[0, kv * n_sel:(kv + 1) * n_sel, :]
        qa_ref[kv] = _group_t(qt, kv, extra=(bias, zq))
    ccols = cols // chains
    qpos = n * tq + (lax.broadcasted_iota(jnp.int32, (tk, ccols), 1) & (tq - 1))
    krow = lax.broadcasted_iota(jnp.int32, (tk, ccols), 0)

    def step(kt, state, causal):
        k0 = pl.multiple_of(kt * tk, tk)
        scores = []
        for c in range(len(state)):
            kv, part = divmod(c, chains)
            cs = slice(part * ccols, (part + 1) * ccols)
            s = jnp.dot(ka_ref[kv, pl.ds(k0, tk), :], qa_ref[kv, :, cs],
                        preferred_element_type=F32)
            if causal:
                s = jnp.where(k0 + krow <= qpos, s, NEG_INF)
            scores.append(s)
        probs = []
        for s, (m_old, l_old, _) in zip(scores, state):
            m_new = jnp.maximum(m_old, jnp.max(s, axis=0, keepdims=True))
            alpha = jnp.exp(m_old - m_new)
            p = jnp.exp(s - m_new)
            l_new = alpha * l_old + jnp.sum(p, axis=0, keepdims=True)
            probs.append((m_new, l_new, alpha, p.astype(BF16)))
        new_state = []
        for c, ((m_new, l_new, alpha, p), (_, _, acc_old)) in enumerate(zip(probs, state)):
            kv = c // chains
            acc_new = alpha * acc_old + jnp.dot(vt_ref[kv, kt], p, preferred_element_type=F32)
            new_state.append((m_new, l_new, acc_new))
        return tuple(new_state)

    init = tuple((jnp.full((1, ccols), NEG_INF, F32), jnp.zeros((1, ccols), F32),
                  jnp.zeros((HEAD_DIM, ccols), F32)) for _ in range(N_KV * chains))
    n_full = (n * tq) // tk
    state = lax.fori_loop(0, n_full, functools.partial(step, causal=False), init)
    state = step(n_full, state, causal=True)

    outs = []
    for kv in range(N_KV):
        parts = [acc * (1.0 / l) for _, l, acc in state[kv * chains:(kv + 1) * chains]]
        outs.append(jnp.concatenate(parts, axis=1))
    _store_heads(o_ref, tok, outs, _gates_t(gl_ref, tok), gate_row, tq)


def _sel_attention(proj, bias, *, tq=256, tk=512, chains=2):
    b, s, _ = proj.shape
    n_sel = bias.shape[1] // N_KV
    cols = GROUP * tq
    return pl.pallas_call(
        functools.partial(_sel_attn_kernel, tq=tq, tk=tk, chains=chains, n_sel=n_sel,
                          gate_row=N_HEADS),
        grid=(b, s // tq),
        in_specs=[
            pl.BlockSpec((1, tq, D_ATT), lambda i, n: (i, n, COL_QB // D_ATT)),
            pl.BlockSpec((1, s, D_KV), lambda i, n: (i, 0, COL_KS // D_KV)),
            pl.BlockSpec((1, s, D_KV), lambda i, n: (i, 0, COL_VS // D_KV)),
            pl.BlockSpec((1, N_KV * n_sel, tq), lambda i, n: (i, 0, n)),
            pl.BlockSpec((1, tq, LANES), lambda i, n: (i, n, COL_GL // LANES)),
        ],
        out_specs=pl.BlockSpec((1, tq, D_ATT), lambda i, n: (i, n, 0)),
        out_shape=jax.ShapeDtypeStruct((b, s, D_ATT), F32),
        scratch_shapes=[
            pltpu.VMEM((N_KV, LANES, cols), BF16),
            pltpu.VMEM((N_KV, s, LANES), BF16),
            pltpu.VMEM((N_KV, s // tk, HEAD_DIM, tk), BF16),
        ],
        compiler_params=_params(("parallel", "arbitrary")),
        name="sel_attn",
    )(proj, proj, proj, bias, proj)


def _rglru_kernel(xr_ref, xg_ref, cw_ref, cb_ref, wa_ref, ba_ref, wx_ref, bx_ref, lam_ref,
                  o_ref, xbuf_ref, h_ref, *, ts):
    pad = SUBLANES

    @pl.when(pl.program_id(1) == 0)
    def _():
        xbuf_ref[0:pad, :] = jnp.zeros((pad, RNN_WIDTH), F32)
        h_ref[...] = jnp.zeros_like(h_ref)

    xr = xr_ref[0]
    xbuf_ref[pad:pad + ts, :] = xr
    xc = cb_ref[...] + cw_ref[CONV_WIDTH - 1:CONV_WIDTH, :] * xr
    for w in range(CONV_WIDTH - 1):
        shift = CONV_WIDTH - 1 - w
        xc = xc + cw_ref[w:w + 1, :] * xbuf_ref[pl.ds(pad - shift, ts), :]
    xbuf_ref[0:pad, :] = xr[ts - pad:ts, :]

    ra, rx = [], []
    for blk in range(RNN_WIDTH // MXU_DIM):
        xs = xc[:, blk * MXU_DIM:(blk + 1) * MXU_DIM].astype(BF16)
        ra.append(jnp.dot(xs, wa_ref[blk], preferred_element_type=F32))
        rx.append(jnp.dot(xs, wx_ref[blk], preferred_element_type=F32))
    r = 0.5 * (jnp.tanh(0.5 * (jnp.concatenate(ra, axis=1) + ba_ref[...])) + 1.0)
    gi = 0.5 * (jnp.tanh(0.5 * (jnp.concatenate(rx, axis=1) + bx_ref[...])) + 1.0)
    nl = -lam_ref[...]
    softplus = jnp.maximum(nl, 0.0) + jnp.log1p(jnp.exp(-jnp.abs(nl)))
    log_a = -LRU_C * r * softplus
    a = jnp.exp(log_a)
    u = jnp.sqrt(-jnp.tanh(log_a) * (a * a + 1.0)) * (gi * xc)

    n_slab = ts // SUBLANES
    a = a.reshape(n_slab, SUBLANES, RNN_WIDTH)
    u = u.reshape(n_slab, SUBLANES, RNN_WIDTH)
    sub = lax.broadcasted_iota(jnp.int32, a.shape, 1)
    d = 1
    while d < SUBLANES:
        keep = sub >= d
        a_prev = jnp.where(keep, pltpu.roll(a, d, 1), 1.0)
        u_prev = jnp.where(keep, pltpu.roll(u, d, 1), 0.0)
        u = a * u_prev + u
        a = a * a_prev
        d *= 2
    h_last = h_ref[0:1, :]
    slabs = []
    for r in range(n_slab):
        h = a[r] * h_last + u[r]
        slabs.append(h)
        h_last = h[SUBLANES - 1:SUBLANES, :]
    h_ref[0:1, :] = h_last
    o_ref[0] = jax.nn.gelu(xg_ref[0], approximate=True) * jnp.concatenate(slabs, axis=0)


def _rglru(proj, conv_w, conv_b, wa, ba, wx, bx, lam, *, ts=256):
    b, s, _ = proj.shape
    c = RNN_WIDTH
    vec = pl.BlockSpec((1, c), lambda i, n: (0, 0))
    wspec = pl.BlockSpec(wa.shape, lambda i, n: (0, 0, 0))
    return pl.pallas_call(
        functools.partial(_rglru_kernel, ts=ts),
        grid=(b, s // ts),
        in_specs=[
            pl.BlockSpec((1, ts, c), lambda i, n: (i, n, COL_XR // c)),
            pl.BlockSpec((1, ts, c), lambda i, n: (i, n, COL_XG // c)),
            pl.BlockSpec((CONV_WIDTH, c), lambda i, n: (0, 0)),
            vec, wspec, vec, wspec, vec, vec,
        ],
        out_specs=pl.BlockSpec((1, ts, c), lambda i, n: (i, n, 0)),
        out_shape=jax.ShapeDtypeStruct((b, s, c), F32),
        scratch_shapes=[pltpu.VMEM((ts + SUBLANES, c), F32), pltpu.VMEM((SUBLANES, c), F32)],
        compiler_params=_params(("parallel", "arbitrary")),
        name="rglru",
    )(proj, proj, conv_w, conv_b, wa, ba, wx, bx, lam)


def _outproj_kernel(x_ref, ya_ref, oc_ref, os_ref, ow_ref, yc_ref, gn_ref, w_ref, o_ref):
    ya = _rms(ya_ref[...], gn_ref[:, 0:D_ATT]).astype(BF16)
    yb = oc_ref[...] + os_ref[...] + ow_ref[...]
    yb = _rms(yb, gn_ref[:, D_ATT:2 * D_ATT]).astype(BF16)
    yc = _rms(yc_ref[...], gn_ref[:, 2 * D_ATT:]).astype(BF16)
    y = jnp.dot(ya, w_ref[0:D_ATT, :], preferred_element_type=F32)
    y = y + jnp.dot(yb, w_ref[D_ATT:2 * D_ATT, :], preferred_element_type=F32)
    y = y + jnp.dot(yc, w_ref[2 * D_ATT:, :], preferred_element_type=F32)
    o_ref[...] = x_ref[...] + y


def _outproj(x, ya, oc, osl, ow, yc, gn, w, layer, *, tm=256):
    t, d = x.shape
    att = pl.BlockSpec((tm, D_ATT), lambda i: (i, 0))
    return pl.pallas_call(
        _outproj_kernel,
        grid=(t // tm,),
        in_specs=[
            pl.BlockSpec((tm, d), lambda i: (i, 0)),
            att, att, att, att,
            pl.BlockSpec((tm, RNN_WIDTH), lambda i: (i, 0)),
            pl.BlockSpec((1, gn.shape[1]), lambda i: (0, 0)),
            pl.BlockSpec((None,) + w.shape[1:], lambda i: (layer, 0, 0)),
        ],
        out_specs=pl.BlockSpec((tm, d), lambda i: (i, 0)),
        out_shape=jax.ShapeDtypeStruct((t, d), F32),
        compiler_params=_params(("parallel",)),
        name="outproj",
    )(x, ya, oc, osl, ow, yc, gn, w)


def _block_diag(w, per):
    *lead, n, r, c = w.shape
    w = w.reshape(*lead, n // per, per, r, c)
    eye = jnp.eye(per, dtype=w.dtype)
    out = jnp.einsum('...prc,pq->...prqc', w, eye)
    return out.reshape(*lead, n // per, per * r, per * c)


def _compress_weights(cmp_pos, w1, b1, w2):
    pos = jnp.tile(cmp_pos, (1, 1, N_KV))
    w1 = w1.reshape(2, CMP_LEN, 1, HEAD_DIM, CMP_HIDDEN)
    w1 = _block_diag(jnp.broadcast_to(w1, (2, CMP_LEN, N_KV, HEAD_DIM, CMP_HIDDEN)), N_KV)
    w1 = w1.reshape(2, CMP_LEN, N_KV * HEAD_DIM, N_KV * CMP_HIDDEN)
    w2 = _block_diag(jnp.broadcast_to(w2[:, None], (2, N_KV, CMP_HIDDEN, HEAD_DIM)), N_KV)
    w2 = w2.reshape(2, N_KV * CMP_HIDDEN, N_KV * HEAD_DIM)
    b1 = jnp.tile(b1[:, None, :], (1, 1, N_KV))
    return pos, w1.astype(BF16), b1, w2.astype(BF16)


def _overlap_matrix(seq, n_cmp_padded):
    n_c = (seq - CMP_LEN) // CMP_STRIDE + 1
    n_sel = seq // SEL_BLOCK
    cs = np.arange(n_c)[:, None] * CMP_STRIDE
    ss = np.arange(n_sel)[None, :] * SEL_BLOCK
    ov = np.clip(np.minimum(cs + CMP_LEN, ss + SEL_BLOCK) - np.maximum(cs, ss), 0, None)
    out = np.zeros((n_cmp_padded, n_sel), np.float32)
    out[:n_c] = ov / CMP_LEN
    return jnp.asarray(out.T)


def kernel(x, ffn1_norm, ffn1_w_gate, ffn1_w_up, ffn1_w_down, mix_norm, w_in, swa_sinks, cmp_pos, cmp_w1, cmp_b1, cmp_w2, conv_w, conv_b, lru_wa, lru_ba, lru_wx, lru_bx, lru_lambda, group_norm, w_out, ffn2_norm, ffn2_w_gate, ffn2_w_up, ffn2_w_down, final_norm):
    b, s, d = x.shape
    depth = w_in.shape[0]
    t = b * s
    overlap = _overlap_matrix(s, s // CMP_STRIDE)
    gate_tile = MXU_DIM // RNN_BLOCK_WIDTH
    xt = x.reshape(t, d)
    ffn1_w = (_to_bf16(ffn1_w_gate, rows=256), _to_bf16(ffn1_w_up, rows=256),
              _to_bf16(ffn1_w_down, rows=512))
    ffn2_w = (_to_bf16(ffn2_w_gate, rows=256), _to_bf16(ffn2_w_up, rows=256),
              _to_bf16(ffn2_w_down, rows=512))
    w_in_b = _stage_w_in(w_in)
    w_out_b = _to_bf16(w_out, rows=512)
    for l in range(depth):
        xt = _ffn(xt, ffn1_norm[l][None], *ffn1_w, l)

        proj = _inproj(xt, mix_norm[l][None], w_in_b, l)
        proj = proj.reshape(b, s, D_PROJ)

        ya = _window_attention(proj, COL_QA, COL_KA, COL_VA, SWA_WINDOW, sinks=swa_sinks[l])

        kc, vc = _compress(proj, *_compress_weights(cmp_pos[l], cmp_w1[l], cmp_b1[l], cmp_w2[l]))
        o_cmp, bias = _cmp_attention(proj, kc, vc, overlap)
        o_slc = _sel_attention(proj, bias)
        o_win = _window_attention(proj, COL_QB, COL_KW, COL_VW, NSA_WINDOW,
                                  gate_row=2 * N_HEADS, q_blocks=2)

        yc = _rglru(proj, conv_w[l], conv_b[l][None],
                    _block_diag(lru_wa[l], gate_tile).astype(BF16), lru_ba[l][None],
                    _block_diag(lru_wx[l], gate_tile).astype(BF16), lru_bx[l][None],
                    lru_lambda[l][None])

        xt = _outproj(xt, ya.reshape(t, D_ATT), o_cmp.reshape(t, D_ATT),
                      o_slc.reshape(t, D_ATT), o_win.reshape(t, D_ATT),
                      yc.reshape(t, RNN_WIDTH), group_norm[l][None], w_out_b, l)

        last = l == depth - 1
        xt = _ffn(xt, ffn2_norm[l][None], *ffn2_w, l,
                  final_g=final_norm[None] if last else None)
    return xt.reshape(b, s, d)
```

```python
import functools

import jax
import jax.numpy as jnp
import numpy as np
from jax import lax
from jax.experimental import pallas as pl
from jax.experimental.pallas import tpu as pltpu

F32 = jnp.float32
BF16 = jnp.bfloat16

D_MODEL = 2048
HEAD_DIM = 64
D_FF = 5632
N_HEADS = 8
N_KV = 2
GROUP = N_HEADS // N_KV
D_ATT = N_HEADS * HEAD_DIM
D_KV = N_KV * HEAD_DIM
SWA_WINDOW = 128
NSA_WINDOW = 512
ATTN_BLOCK = 128
CMP_LEN = 32
CMP_STRIDE = 16
CMP_HIDDEN = 256
SEL_BLOCK = 64
TOP_N = 16
RNN_WIDTH = 1024
RNN_BLOCK_WIDTH = 64
CONV_WIDTH = 4
LRU_C = 8.0
RMS_EPS = 1e-6
NEG_INF = -1e30
FORCED_SCORE = 1e4
SCALE = HEAD_DIM ** -0.5

LANES = 128
SUBLANES = 8
MXU_DIM = 256
VMEM_BYTES = 64 * 1024 * 1024
VMEM_LIMIT = VMEM_BYTES - 8 * 1024 * 1024
VMEM_LIMIT_FFN = VMEM_BYTES - 4 * 1024 * 1024

COL_XR = 0
COL_XG = 1024
COL_QA = 2048
COL_QB = 2560
COL_KA = 3072
COL_VA = 3200
COL_KC = 3328
COL_VC = 3456
COL_KS = 3584
COL_VS = 3712
COL_KW = 3840
COL_VW = 3968
COL_GL = 4096
D_PROJ = 4224

_REF_SPLITS = (D_ATT, D_KV, D_KV, D_ATT, D_KV, D_KV, D_KV, D_KV, D_KV, D_KV, 3 * N_HEADS,
               RNN_WIDTH, RNN_WIDTH)


def _params(sem, vmem_limit=VMEM_LIMIT):
    return pltpu.CompilerParams(dimension_semantics=sem, vmem_limit_bytes=vmem_limit)


def _rms(x, g):
    return x * lax.rsqrt(jnp.mean(x * x, axis=-1, keepdims=True) + RMS_EPS) * g


def _cast_kernel(w_ref, o_ref):
    o_ref[...] = w_ref[...].astype(BF16)


def _to_bf16(w, *, rows):
    dep, k, n = w.shape
    spec = pl.BlockSpec((1, rows, n), lambda l, i: (l, i, 0))
    return pl.pallas_call(
        _cast_kernel,
        grid=(dep, k // rows),
        in_specs=[spec],
        out_specs=spec,
        out_shape=jax.ShapeDtypeStruct(w.shape, BF16),
        compiler_params=_params(("parallel", "parallel")),
        name="cast_bf16",
    )(w)


def _w_in_kernel(w_ref, o_ref):
    w = w_ref[0]
    offs = np.concatenate([[0], np.cumsum(_REF_SPLITS)])
    qa, ka, va, qb, kc, vc, ks, vs, kw, vw, gl, xr, xg = [
        w[:, int(offs[i]):int(offs[i + 1])] for i in range(len(_REF_SPLITS))]
    gl = jnp.concatenate([gl, jnp.zeros((w.shape[0], LANES - gl.shape[1]), F32)], axis=1)
    out = jnp.concatenate([xr, xg, qa, qb, ka, va, kc, vc, ks, vs, kw, vw, gl], axis=1)
    o_ref[0] = out.astype(BF16)


def _stage_w_in(w, *, rows=256):
    dep, k, n = w.shape
    return pl.pallas_call(
        _w_in_kernel,
        grid=(dep, k // rows),
        in_specs=[pl.BlockSpec((1, rows, n), lambda l, i: (l, i, 0))],
        out_specs=pl.BlockSpec((1, rows, D_PROJ), lambda l, i: (l, i, 0)),
        out_shape=jax.ShapeDtypeStruct((dep, k, D_PROJ), BF16),
        compiler_params=_params(("parallel", "parallel")),
        name="stage_w_in",
    )(w)


def _ffn_kernel(x_ref, g_ref, wg_ref, wu_ref, wd_ref, *rest, final_norm):
    if final_norm:
        fg_ref, o_ref, h_ref = rest
    else:
        o_ref, h_ref = rest
    j = pl.program_id(1)

    @pl.when(j == 0)
    def _():
        x = x_ref[...]
        h_ref[...] = _rms(x, g_ref[...]).astype(BF16)
        o_ref[...] = x

    h = h_ref[...]
    gate = jnp.dot(h, wg_ref[...], preferred_element_type=F32)
    up = jnp.dot(h, wu_ref[...], preferred_element_type=F32)
    act = (0.5 * gate * jax.nn.sigmoid(gate) * up).astype(BF16)
    o_ref[...] += jnp.dot(act, wd_ref[...], preferred_element_type=F32)

    if final_norm:
        @pl.when(j == pl.num_programs(1) - 1)
        def _():
            o_ref[...] = _rms(o_ref[...], fg_ref[...])


def _ffn(x, g, wg, wu, wd, layer, final_g=None, *, tm=1024, tf=256):
    t, d = x.shape
    f = wg.shape[2]
    final_norm = final_g is not None
    in_specs = [
        pl.BlockSpec((tm, d), lambda i, j: (i, 0)),
        pl.BlockSpec((1, d), lambda i, j: (0, 0)),
        pl.BlockSpec((None, d, tf), lambda i, j: (layer, 0, j)),
        pl.BlockSpec((None, d, tf), lambda i, j: (layer, 0, j)),
        pl.BlockSpec((None, tf, d), lambda i, j: (layer, j, 0)),
    ]
    args = [x, g, wg, wu, wd]
    if final_norm:
        in_specs.append(pl.BlockSpec((1, d), lambda i, j: (0, 0)))
        args.append(final_g)
    return pl.pallas_call(
        functools.partial(_ffn_kernel, final_norm=final_norm),
        grid=(t // tm, f // tf),
        in_specs=in_specs,
        out_specs=pl.BlockSpec((tm, d), lambda i, j: (i, 0)),
        out_shape=jax.ShapeDtypeStruct((t, d), F32),
        scratch_shapes=[pltpu.VMEM((tm, d), BF16)],
        compiler_params=_params(("parallel", "arbitrary"), VMEM_LIMIT_FFN),
        name="ffn",
    )(*args)


def _inproj_kernel(x_ref, g_ref, w_ref, o_ref, h_ref):
    @pl.when(pl.program_id(1) == 0)
    def _():
        h_ref[...] = _rms(x_ref[...], g_ref[...]).astype(BF16)

    o_ref[...] = jnp.dot(h_ref[...], w_ref[...], preferred_element_type=F32)


def _inproj(x, g, w, layer, *, tm=1024, tn=1408):
    t, d = x.shape
    n = w.shape[2]
    return pl.pallas_call(
        _inproj_kernel,
        grid=(t // tm, n // tn),
        in_specs=[
            pl.BlockSpec((tm, d), lambda i, j: (i, 0)),
            pl.BlockSpec((1, d), lambda i, j: (0, 0)),
            pl.BlockSpec((None, d, tn), lambda i, j: (layer, 0, j)),
        ],
        out_specs=pl.BlockSpec((tm, tn), lambda i, j: (i, j)),
        out_shape=jax.ShapeDtypeStruct((t, n), F32),
        scratch_shapes=[pltpu.VMEM((tm, d), BF16)],
        compiler_params=_params(("parallel", "arbitrary")),
        name="inproj",
    )(x, g, w)


def _queries_t(q_ref, tok):
    return q_ref[0, tok, :].T * SCALE


def _group_t(qt, kv, extra=()):
    cols = []
    for g in range(GROUP):
        h = kv * GROUP + g
        parts = [qt[h * HEAD_DIM:(h + 1) * HEAD_DIM, :], *extra]
        cols.append(parts[0] if len(parts) == 1 else jnp.concatenate(parts, axis=0))
    return jnp.concatenate(cols, axis=1).astype(BF16)


def _store_heads(o_ref, tok, outs_t, gates_t, gate_row, tq):
    pieces = []
    for kv in range(N_KV):
        for g in range(GROUP):
            h = kv * GROUP + g
            piece = outs_t[kv][:, g * tq:(g + 1) * tq]
            if gates_t is not None:
                piece = piece * gates_t[gate_row + h:gate_row + h + 1, :]
            pieces.append(piece)
    o_ref[0, tok, :] = jnp.concatenate(pieces, axis=0).T


def _gates_t(gl_ref, tok):
    return jax.nn.sigmoid(gl_ref[0, tok, :].T)


def _window_kernel(*refs, window, n_prev, q_blocks, use_sink, gate_row):
    refs = list(refs)
    q_ref, k_ref, v_ref = refs[:3]
    rest = refs[3:]
    sink_ref = rest.pop(0) if use_sink else None
    gl_ref = rest.pop(0) if gate_row is not None else None
    o_ref, kb_ref, vt_ref = rest

    blk = ATTN_BLOCK
    cols = GROUP * blk
    span = (n_prev + 1) * blk
    n = pl.program_id(1)

    @pl.when(n == 0)
    def _():
        vt = v_ref[0].T
        for kv in range(N_KV):
            kb_ref[kv] = k_ref[0, :, kv * HEAD_DIM:(kv + 1) * HEAD_DIM].astype(BF16)
            vt_ref[kv] = vt[kv * HEAD_DIM:(kv + 1) * HEAD_DIM, :].astype(BF16)

    krow = lax.broadcasted_iota(jnp.int32, (span, cols), 0)
    qcol = lax.broadcasted_iota(jnp.int32, (span, cols), 1) & (blk - 1)
    starts, scores = [], []
    for j in range(q_blocks):
        nb = n * q_blocks + j
        start = pl.multiple_of(jnp.maximum(nb - n_prev, 0) * blk, blk)
        diff = (nb * blk + qcol) - (start + krow)
        mask = (diff >= 0) & (diff < window)
        qt = _queries_t(q_ref, slice(j * blk, (j + 1) * blk))
        starts.append(start)
        for kv in range(N_KV):
            s = jnp.dot(kb_ref[kv, pl.ds(start, span), :], _group_t(qt, kv),
                        preferred_element_type=F32)
            scores.append(jnp.where(mask, s, NEG_INF))
    probs = []
    for c, s in enumerate(scores):
        kv = c % N_KV
        m = jnp.max(s, axis=0, keepdims=True)
        if use_sink:
            sk = jnp.concatenate([jnp.full((1, blk), sink_ref[kv * GROUP + g], F32)
                                  for g in range(GROUP)], axis=1)
            m = jnp.maximum(m, sk)
            p = jnp.exp(s - m)
            denom = jnp.sum(p, axis=0, keepdims=True) + jnp.exp(sk - m)
        else:
            p = jnp.exp(s - m)
            denom = jnp.sum(p, axis=0, keepdims=True)
        probs.append((p.astype(BF16), denom))
    for j in range(q_blocks):
        tok = slice(j * blk, (j + 1) * blk)
        outs = []
        for kv in range(N_KV):
            p, denom = probs[j * N_KV + kv]
            o = jnp.dot(vt_ref[kv, :, pl.ds(starts[j], span)], p,
                        preferred_element_type=F32)
            outs.append(o * (1.0 / denom))
        gates = _gates_t(gl_ref, tok) if gate_row is not None else None
        _store_heads(o_ref, tok, outs, gates, gate_row, blk)


def _window_attention(proj, q_col, k_col, v_col, window, sinks=None, gate_row=None, *,
                      q_blocks=4):
    b, s, _ = proj.shape
    n_prev = -(-(window - 1) // ATTN_BLOCK)
    blk = q_blocks * ATTN_BLOCK
    in_specs = [
        pl.BlockSpec((1, blk, D_ATT), lambda i, n: (i, n, q_col // D_ATT)),
        pl.BlockSpec((1, s, D_KV), lambda i, n: (i, 0, k_col // D_KV)),
        pl.BlockSpec((1, s, D_KV), lambda i, n: (i, 0, v_col // D_KV)),
    ]
    args = [proj, proj, proj]
    if sinks is not None:
        in_specs.append(pl.BlockSpec(memory_space=pltpu.SMEM))
        args.append(sinks)
    if gate_row is not None:
        in_specs.append(pl.BlockSpec((1, blk, LANES), lambda i, n: (i, n, COL_GL // LANES)))
        args.append(proj)
    return pl.pallas_call(
        functools.partial(_window_kernel, window=window, n_prev=n_prev, q_blocks=q_blocks,
                          use_sink=sinks is not None, gate_row=gate_row),
        grid=(b, s // blk),
        in_specs=in_specs,
        out_specs=pl.BlockSpec((1, blk, D_ATT), lambda i, n: (i, n, 0)),
        out_shape=jax.ShapeDtypeStruct((b, s, D_ATT), F32),
        scratch_shapes=[pltpu.VMEM((N_KV, s, HEAD_DIM), BF16),
                        pltpu.VMEM((N_KV, HEAD_DIM, s), BF16)],
        compiler_params=_params(("parallel", "arbitrary")),
        name=f"window{window}",
    )(*args)


def _compress_kernel(k_ref, v_ref, pos_ref, w1_ref, b1_ref, w2_ref, kc_ref, vc_ref):
    n_rows = kc_ref.shape[1]
    for i, (src_ref, dst_ref) in enumerate(((k_ref, kc_ref), (v_ref, vc_ref))):
        lo = jnp.zeros((n_rows, N_KV * CMP_HIDDEN), F32)
        hi = jnp.zeros((n_rows, N_KV * CMP_HIDDEN), F32)
        for l in range(CMP_STRIDE):
            x = src_ref[0, pl.ds(l, n_rows, stride=CMP_STRIDE), :]
            lo = lo + jnp.dot((x + pos_ref[i, l:l + 1, :]).astype(BF16), w1_ref[i, l],
                              preferred_element_type=F32)
            hi = hi + jnp.dot((x + pos_ref[i, CMP_STRIDE + l:CMP_STRIDE + l + 1, :]).astype(BF16),
                              w1_ref[i, CMP_STRIDE + l], preferred_element_type=F32)
        hid = lo + pltpu.roll(hi, n_rows - 1, 0) + b1_ref[i]
        act = jax.nn.gelu(hid, approximate=True).astype(BF16)
        dst_ref[0] = jnp.dot(act, w2_ref[i], preferred_element_type=F32)


def _compress(proj, pos, w1, b1, w2):
    b, s, _ = proj.shape
    n_rows = s // CMP_STRIDE
    out_spec = pl.BlockSpec((1, n_rows, D_KV), lambda i: (i, 0, 0))
    out_shape = jax.ShapeDtypeStruct((b, n_rows, D_KV), F32)
    return pl.pallas_call(
        _compress_kernel,
        grid=(b,),
        in_specs=[
            pl.BlockSpec((1, s, D_KV), lambda i: (i, 0, COL_KC // D_KV)),
            pl.BlockSpec((1, s, D_KV), lambda i: (i, 0, COL_VC // D_KV)),
            pl.BlockSpec(pos.shape, lambda i: (0, 0, 0)),
            pl.BlockSpec(w1.shape, lambda i: (0, 0, 0, 0)),
            pl.BlockSpec(b1.shape, lambda i: (0, 0, 0)),
            pl.BlockSpec(w2.shape, lambda i: (0, 0, 0)),
        ],
        out_specs=[out_spec, out_spec],
        out_shape=[out_shape, out_shape],
        compiler_params=_params(("parallel",)),
        name="compress",
    )(proj, proj, pos, w1, b1, w2)


def _cmp_attn_kernel(q_ref, kc_ref, vc_ref, gl_ref, ov_ref, o_ref, bias_ref, *, tq, n_sel):
    n = pl.program_id(1)
    ncp = kc_ref.shape[1]
    cols = GROUP * tq
    pos_c = n * tq + (lax.broadcasted_iota(jnp.int32, (ncp, cols), 1) & (tq - 1))
    cend = lax.broadcasted_iota(jnp.int32, (ncp, cols), 0) * CMP_STRIDE + (CMP_LEN - 1)
    cmask = cend <= pos_c

    pos_s = n * tq + lax.broadcasted_iota(jnp.int32, (n_sel, tq), 1)
    bid = lax.broadcasted_iota(jnp.int32, (n_sel, tq), 0)
    cur = pos_s // SEL_BLOCK
    forced = (bid == 0) | (bid == cur) | (bid == cur - 1)
    valid = bid * SEL_BLOCK <= pos_s

    tok = slice(None)
    qt = _queries_t(q_ref, tok)
    vct = vc_ref[0].T
    outs = []
    for kv in range(N_KV):
        kc = kc_ref[0, :, kv * HEAD_DIM:(kv + 1) * HEAD_DIM].astype(BF16)
        s = jnp.dot(kc, _group_t(qt, kv), preferred_element_type=F32)
        s = jnp.where(cmask, s, NEG_INF)
        m = jnp.max(s, axis=0, keepdims=True)
        p = jnp.exp(s - m)
        p = p * (1.0 / jnp.sum(p, axis=0, keepdims=True))
        p = jnp.where(cmask, p, 0.0)
        outs.append(jnp.dot(vct[kv * HEAD_DIM:(kv + 1) * HEAD_DIM, :].astype(BF16),
                            p.astype(BF16), preferred_element_type=F32))
        psum = p[:, 0:tq]
        for g in range(1, GROUP):
            psum = psum + p[:, g * tq:(g + 1) * tq]

        imp = jnp.dot(ov_ref[...], psum, preferred_element_type=F32,
                      precision=lax.Precision.HIGHEST)
        imp = jnp.where(forced, FORCED_SCORE, imp)
        imp = jnp.where(valid, imp, NEG_INF)
        rank = jnp.zeros((n_sel, tq), jnp.int32)
        for i in range(n_sel):
            row = imp[i:i + 1, :]
            ahead = (row > imp) | ((row == imp) & (bid > i))
            rank = rank + ahead.astype(jnp.int32)
        bias_ref[0, kv * n_sel:(kv + 1) * n_sel, :] = jnp.where(rank < TOP_N, 0.0, NEG_INF)
    _store_heads(o_ref, tok, outs, _gates_t(gl_ref, tok), 0, tq)


def _cmp_attention(proj, kc, vc, overlap, *, tq=256):
    b, s, _ = proj.shape
    n_sel = overlap.shape[0]
    ncp = kc.shape[1]
    return pl.pallas_call(
        functools.partial(_cmp_attn_kernel, tq=tq, n_sel=n_sel),
        grid=(b, s // tq),
        in_specs=[
            pl.BlockSpec((1, tq, D_ATT), lambda i, n: (i, n, COL_QB // D_ATT)),
            pl.BlockSpec((1, ncp, D_KV), lambda i, n: (i, 0, 0)),
            pl.BlockSpec((1, ncp, D_KV), lambda i, n: (i, 0, 0)),
            pl.BlockSpec((1, tq, LANES), lambda i, n: (i, n, COL_GL // LANES)),
            pl.BlockSpec(overlap.shape, lambda i, n: (0, 0)),
        ],
        out_specs=[
            pl.BlockSpec((1, tq, D_ATT), lambda i, n: (i, n, 0)),
            pl.BlockSpec((1, N_KV * n_sel, tq), lambda i, n: (i, 0, n)),
        ],
        out_shape=[
            jax.ShapeDtypeStruct((b, s, D_ATT), F32),
            jax.ShapeDtypeStruct((b, N_KV * n_sel, s), F32),
        ],
        compiler_params=_params(("parallel", "arbitrary")),
        name="cmp_attn",
    )(proj, kc, vc, proj, overlap)


def _sel_attn_kernel(q_ref, k_ref, v_ref, bias_ref, gl_ref, o_ref,
                     qa_ref, ka_ref, vt_ref, *, tq, tk, chains, n_sel, gate_row):
    n = pl.program_id(1)
    s_len = k_ref.shape[1]
    cols = GROUP * tq
    n_pad = LANES - HEAD_DIM - n_sel

    @pl.when(n == 0)
    def _():
        kpos = lax.broadcasted_iota(jnp.int32, (s_len, n_sel), 0)
        blk = lax.broadcasted_iota(jnp.int32, (s_len, n_sel), 1)
        onehot = (kpos // SEL_BLOCK == blk).astype(F32)
        zeros = jnp.zeros((s_len, n_pad), F32)
        vt = v_ref[0].T
        for kv in range(N_KV):
            k = k_ref[0, :, kv * HEAD_DIM:(kv + 1) * HEAD_DIM]
            ka_ref[kv] = jnp.concatenate([k, onehot, zeros], axis=1).astype(BF16)
            for t in range(s_len // tk):
                vt_ref[kv, t] = vt[kv * HEAD_DIM:(kv + 1) * HEAD_DIM,
                                   t * tk:(t + 1) * tk].astype(BF16)

    tok = slice(None)
    qt = _queries_t(q_ref, tok)
    zq = jnp.zeros((n_pad, tq), F32)
    for kv in range(N_KV):
        bias = bias_ref[0, kv * n_sel:(kv + 1) * n_sel, :]
        qa_ref[kv] = _group_t(qt, kv, extra=(bias, zq))
    ccols = cols // chains
    qpos = n * tq + (lax.broadcasted_iota(jnp.int32, (tk, ccols), 1) & (tq - 1))
    krow = lax.broadcasted_iota(jnp.int32, (tk, ccols), 0)

    def step(kt, state, causal):
        k0 = pl.multiple_of(kt * tk, tk)
        scores = []
        for c in range(len(state)):
            kv, part = divmod(c, chains)
            cs = slice(part * ccols, (part + 1) * ccols)
            s = jnp.dot(ka_ref[kv, pl.ds(k0, tk), :], qa_ref[kv, :, cs],
                        preferred_element_type=F32)
            if causal:
                s = jnp.where(k0 + krow <= qpos, s, NEG_INF)
            scores.append(s)
        probs = []
        for s, (m_old, l_old, _) in zip(scores, state):
            m_new = jnp.maximum(m_old, jnp.max(s, axis=0, keepdims=True))
            alpha = jnp.exp(m_old - m_new)
            p = jnp.exp(s - m_new)
            l_new = alpha * l_old + jnp.sum(p, axis=0, keepdims=True)
            probs.append((m_new, l_new, alpha, p.astype(BF16)))
        new_state = []
        for c, ((m_new, l_new, alpha, p), (_, _, acc_old)) in enumerate(zip(probs, state)):
            kv = c // chains
            acc_new = alpha * acc_old + jnp.dot(vt_ref[kv, kt], p, preferred_element_type=F32)
            new_state.append((m_new, l_new, acc_new))
        return tuple(new_state)

    init = tuple((jnp.full((1, ccols), NEG_INF, F32), jnp.zeros((1, ccols), F32),
                  jnp.zeros((HEAD_DIM, ccols), F32)) for _ in range(N_KV * chains))
    n_full = (n * tq) // tk
    state = lax.fori_loop(0, n_full, functools.partial(step, causal=False), init)
    state = step(n_full, state, causal=True)

    outs = []
    for kv in range(N_KV):
        parts = [acc * (1.0 / l) for _, l, acc in state[kv * chains:(kv + 1) * chains]]
        outs.append(jnp.concatenate(parts, axis=1))
    _store_heads(o_ref, tok, outs, _gates_t(gl_ref, tok), gate_row, tq)


def _sel_attention(proj, bias, *, tq=256, tk=512, chains=2):
    b, s, _ = proj.shape
    n_sel = bias.shape[1] // N_KV
    cols = GROUP * tq
    return pl.pallas_call(
        functools.partial(_sel_attn_kernel, tq=tq, tk=tk, chains=chains, n_sel=n_sel,
                          gate_row=N_HEADS),
        grid=(b, s // tq),
        in_specs=[
            pl.BlockSpec((1, tq, D_ATT), lambda i, n: (i, n, COL_QB // D_ATT)),
            pl.BlockSpec((1, s, D_KV), lambda i, n: (i, 0, COL_KS // D_KV)),
            pl.BlockSpec((1, s, D_KV), lambda i, n: (i, 0, COL_VS // D_KV)),
            pl.BlockSpec((1, N_KV * n_sel, tq), lambda i, n: (i, 0, n)),
            pl.BlockSpec((1, tq, LANES), lambda i, n: (i, n, COL_GL // LANES)),
        ],
        out_specs=pl.BlockSpec((1, tq, D_ATT), lambda i, n: (i, n, 0)),
        out_shape=jax.ShapeDtypeStruct((b, s, D_ATT), F32),
        scratch_shapes=[
            pltpu.VMEM((N_KV, LANES, cols), BF16),
            pltpu.VMEM((N_KV, s, LANES), BF16),
            pltpu.VMEM((N_KV, s // tk, HEAD_DIM, tk), BF16),
        ],
        compiler_params=_params(("parallel", "arbitrary")),
        name="sel_attn",
    )(proj, proj, proj, bias, proj)


def _rglru_kernel(xr_ref, xg_ref, cw_ref, cb_ref, wa_ref, ba_ref, wx_ref, bx_ref, lam_ref,
                  o_ref, xbuf_ref, h_ref, *, ts):
    pad = SUBLANES

    @pl.when(pl.program_id(1) == 0)
    def _():
        xbuf_ref[0:pad, :] = jnp.zeros((pad, RNN_WIDTH), F32)
        h_ref[...] = jnp.zeros_like(h_ref)

    xr = xr_ref[0]
    xbuf_ref[pad:pad + ts, :] = xr
    xc = cb_ref[...] + cw_ref[CONV_WIDTH - 1:CONV_WIDTH, :] * xr
    for w in range(CONV_WIDTH - 1):
        shift = CONV_WIDTH - 1 - w
        xc = xc + cw_ref[w:w + 1, :] * xbuf_ref[pl.ds(pad - shift, ts), :]
    xbuf_ref[0:pad, :] = xr[ts - pad:ts, :]

    ra, rx = [], []
    for blk in range(RNN_WIDTH // MXU_DIM):
        xs = xc[:, blk * MXU_DIM:(blk + 1) * MXU_DIM].astype(BF16)
        ra.append(jnp.dot(xs, wa_ref[blk], preferred_element_type=F32))
        rx.append(jnp.dot(xs, wx_ref[blk], preferred_element_type=F32))
    r = 0.5 * (jnp.tanh(0.5 * (jnp.concatenate(ra, axis=1) + ba_ref[...])) + 1.0)
    gi = 0.5 * (jnp.tanh(0.5 * (jnp.concatenate(rx, axis=1) + bx_ref[...])) + 1.0)
    nl = -lam_ref[...]
    softplus = jnp.maximum(nl, 0.0) + jnp.log1p(jnp.exp(-jnp.abs(nl)))
    log_a = -LRU_C * r * softplus
    a = jnp.exp(log_a)
    u = jnp.sqrt(-jnp.tanh(log_a) * (a * a + 1.0)) * (gi * xc)

    n_slab = ts // SUBLANES
    a = a.reshape(n_slab, SUBLANES, RNN_WIDTH)
    u = u.reshape(n_slab, SUBLANES, RNN_WIDTH)
    sub = lax.broadcasted_iota(jnp.int32, a.shape, 1)
    d = 1
    while d < SUBLANES:
        keep = sub >= d
        a_prev = jnp.where(keep, pltpu.roll(a, d, 1), 1.0)
        u_prev = jnp.where(keep, pltpu.roll(u, d, 1), 0.0)
        u = a * u_prev + u
        a = a * a_prev
        d *= 2
    h_last = h_ref[0:1, :]
    slabs = []
    for r in range(n_slab):
        h = a[r] * h_last + u[r]
        slabs.append(h)
        h_last = h[SUBLANES - 1:SUBLANES, :]
    h_ref[0:1, :] = h_last
    o_ref[0] = jax.nn.gelu(xg_ref[0], approximate=True) * jnp.concatenate(slabs, axis=0)


def _rglru(proj, conv_w, conv_b, wa, ba, wx, bx, lam, *, ts=256):
    b, s, _ = proj.shape
    c = RNN_WIDTH
    vec = pl.BlockSpec((1, c), lambda i, n: (0, 0))
    wspec = pl.BlockSpec(wa.shape, lambda i, n: (0, 0, 0))
    return pl.pallas_call(
        functools.partial(_rglru_kernel, ts=ts),
        grid=(b, s // ts),
        in_specs=[
            pl.BlockSpec((1, ts, c), lambda i, n: (i, n, COL_XR // c)),
            pl.BlockSpec((1, ts, c), lambda i, n: (i, n, COL_XG // c)),
            pl.BlockSpec((CONV_WIDTH, c), lambda i, n: (0, 0)),
            vec, wspec, vec, wspec, vec, vec,
        ],
        out_specs=pl.BlockSpec((1, ts, c), lambda i, n: (i, n, 0)),
        out_shape=jax.ShapeDtypeStruct((b, s, c), F32),
        scratch_shapes=[pltpu.VMEM((ts + SUBLANES, c), F32), pltpu.VMEM((SUBLANES, c), F32)],
        compiler_params=_params(("parallel", "arbitrary")),
        name="rglru",
    )(proj, proj, conv_w, conv_b, wa, ba, wx, bx, lam)


def _outproj_kernel(x_ref, ya_ref, oc_ref, os_ref, ow_ref, yc_ref, gn_ref, w_ref, o_ref):
    ya = _rms(ya_ref[...], gn_ref[:, 0:D_ATT]).astype(BF16)
    yb = oc_ref[...] + os_ref[...] + ow_ref[...]
    yb = _rms(yb, gn_ref[:, D_ATT:2 * D_ATT]).astype(BF16)
    yc = _rms(yc_ref[...], gn_ref[:, 2 * D_ATT:]).astype(BF16)
    y = jnp.dot(ya, w_ref[0:D_ATT, :], preferred_element_type=F32)
    y = y + jnp.dot(yb, w_ref[D_ATT:2 * D_ATT, :], preferred_element_type=F32)
    y = y + jnp.dot(yc, w_ref[2 * D_ATT:, :], preferred_element_type=F32)
    o_ref[...] = x_ref[...] + y


def _outproj(x, ya, oc, osl, ow, yc, gn, w, layer, *, tm=256):
    t, d = x.shape
    att = pl.BlockSpec((tm, D_ATT), lambda i: (i, 0))
    return pl.pallas_call(
        _outproj_kernel,
        grid=(t // tm,),
        in_specs=[
            pl.BlockSpec((tm, d), lambda i: (i, 0)),
            att, att, att, att,
            pl.BlockSpec((tm, RNN_WIDTH), lambda i: (i, 0)),
            pl.BlockSpec((1, gn.shape[1]), lambda i: (0, 0)),
            pl.BlockSpec((None,) + w.shape[1:], lambda i: (layer, 0, 0)),
        ],
        out_specs=pl.BlockSpec((tm, d), lambda i: (i, 0)),
        out_shape=jax.ShapeDtypeStruct((t, d), F32),
        compiler_params=_params(("parallel",)),
        name="outproj",
    )(x, ya, oc, osl, ow, yc, gn, w)


def _block_diag(w, per):
    *lead, n, r, c = w.shape
    w = w.reshape(*lead, n // per, per, r, c)
    eye = jnp.eye(per, dtype=w.dtype)
    out = jnp.einsum('...prc,pq->...prqc', w, eye)
    return out.reshape(*lead, n // per, per * r, per * c)


def _compress_weights(cmp_pos, w1, b1, w2):
    pos = jnp.tile(cmp_pos, (1, 1, N_KV))
    w1 = w1.reshape(2, CMP_LEN, 1, HEAD_DIM, CMP_HIDDEN)
    w1 = _block_diag(jnp.broadcast_to(w1, (2, CMP_LEN, N_KV, HEAD_DIM, CMP_HIDDEN)), N_KV)
    w1 = w1.reshape(2, CMP_LEN, N_KV * HEAD_DIM, N_KV * CMP_HIDDEN)
    w2 = _block_diag(jnp.broadcast_to(w2[:, None], (2, N_KV, CMP_HIDDEN, HEAD_DIM)), N_KV)
    w2 = w2.reshape(2, N_KV * CMP_HIDDEN, N_KV * HEAD_DIM)
    b1 = jnp.tile(b1[:, None, :], (1, 1, N_KV))
    return pos, w1.astype(BF16), b1, w2.astype(BF16)


def _overlap_matrix(seq, n_cmp_padded):
    n_c = (seq - CMP_LEN) // CMP_STRIDE + 1
    n_sel = seq // SEL_BLOCK
    cs = np.arange(n_c)[:, None] * CMP_STRIDE
    ss = np.arange(n_sel)[None, :] * SEL_BLOCK
    ov = np.clip(np.minimum(cs + CMP_LEN, ss + SEL_BLOCK) - np.maximum(cs, ss), 0, None)
    out = np.zeros((n_cmp_padded, n_sel), np.float32)
    out[:n_c] = ov / CMP_LEN
    return jnp.asarray(out.T)


def kernel(x, ffn1_norm, ffn1_w_gate, ffn1_w_up, ffn1_w_down, mix_norm, w_in, swa_sinks, cmp_pos, cmp_w1, cmp_b1, cmp_w2, conv_w, conv_b, lru_wa, lru_ba, lru_wx, lru_bx, lru_lambda, group_norm, w_out, ffn2_norm, ffn2_w_gate, ffn2_w_up, ffn2_w_down, final_norm):
    b, s, d = x.shape
    depth = w_in.shape[0]
    t = b * s
    overlap = _overlap_matrix(s, s // CMP_STRIDE)
    gate_tile = MXU_DIM // RNN_BLOCK_WIDTH
    xt = x.reshape(t, d)
    ffn1_w = (_to_bf16(ffn1_w_gate, rows=256), _to_bf16(ffn1_w_up, rows=256),
              _to_bf16(ffn1_w_down, rows=512))
    ffn2_w = (_to_bf16(ffn2_w_gate, rows=256), _to_bf16(ffn2_w_up, rows=256),
              _to_bf16(ffn2_w_down, rows=512))
    w_in_b = _stage_w_in(w_in)
    w_out_b = _to_bf16(w_out, rows=512)
    for l in range(depth):
        xt = _ffn(xt, ffn1_norm[l][None], *ffn1_w, l)

        proj = _inproj(xt, mix_norm[l][None], w_in_b, l)
        proj = proj.reshape(b, s, D_PROJ)

        ya = _window_attention(proj, COL_QA, COL_KA, COL_VA, SWA_WINDOW, sinks=swa_sinks[l])

        kc, vc = _compress(proj, *_compress_weights(cmp_pos[l], cmp_w1[l], cmp_b1[l], cmp_w2[l]))
        o_cmp, bias = _cmp_attention(proj, kc, vc, overlap)
        o_slc = _sel_attention(proj, bias)
        o_win = _window_attention(proj, COL_QB, COL_KW, COL_VW, NSA_WINDOW,
                                  gate_row=2 * N_HEADS, q_blocks=2)

        yc = _rglru(proj, conv_w[l], conv_b[l][None],
                    _block_diag(lru_wa[l], gate_tile).astype(BF16), lru_ba[l][None],
                    _block_diag(lru_wx[l], gate_tile).astype(BF16), lru_bx[l][None],
                    lru_lambda[l][None])

        xt = _outproj(xt, ya.reshape(t, D_ATT), o_cmp.reshape(t, D_ATT),
                      o_slc.reshape(t, D_ATT), o_win.reshape(t, D_ATT),
                      yc.reshape(t, RNN_WIDTH), group_norm[l][None], w_out_b, l)

        last = l == depth - 1
        xt = _ffn(xt, ffn2_norm[l][None], *ffn2_w, l,
                  final_g=final_norm[None] if last else None)
    return xt.reshape(b, s, d)
```

```python
import functools

import jax
import jax.numpy as jnp
import numpy as np
from jax import lax
from jax.experimental import pallas as pl
from jax.experimental.pallas import tpu as pltpu

F32 = jnp.float32
BF16 = jnp.bfloat16

D_MODEL = 2048
HEAD_DIM = 64
D_FF = 5632
N_HEADS = 8
N_KV = 2
GROUP = N_HEADS // N_KV
D_ATT = N_HEADS * HEAD_DIM
D_KV = N_KV * HEAD_DIM
SWA_WINDOW = 128
NSA_WINDOW = 512
ATTN_BLOCK = 128
CMP_LEN = 32
CMP_STRIDE = 16
CMP_HIDDEN = 256
SEL_BLOCK = 64
TOP_N = 16
RNN_WIDTH = 1024
RNN_BLOCK_WIDTH = 64
CONV_WIDTH = 4
LRU_C = 8.0
RMS_EPS = 1e-6
NEG_INF = -1e30
FORCED_SCORE = 1e4
SCALE = HEAD_DIM ** -0.5

LANES = 128
SUBLANES = 8
MXU_DIM = 256
VMEM_BYTES = 64 * 1024 * 1024
VMEM_LIMIT = VMEM_BYTES - 8 * 1024 * 1024
VMEM_LIMIT_FFN = VMEM_BYTES - 4 * 1024 * 1024
DOWN_CHUNK = 512

COL_XR = 0
COL_XG = 1024
COL_QA = 2048
COL_QB = 2560
COL_KA = 3072
COL_VA = 3200
COL_KC = 3328
COL_VC = 3456
COL_KS = 3584
COL_VS = 3712
COL_KW = 3840
COL_VW = 3968
COL_GL = 4096
D_PROJ = 4224

_REF_SPLITS = (D_ATT, D_KV, D_KV, D_ATT, D_KV, D_KV, D_KV, D_KV, D_KV, D_KV, 3 * N_HEADS,
               RNN_WIDTH, RNN_WIDTH)


def _params(sem, vmem_limit=VMEM_LIMIT):
    return pltpu.CompilerParams(dimension_semantics=sem, vmem_limit_bytes=vmem_limit)


def _rms(x, g):
    return x * lax.rsqrt(jnp.mean(x * x, axis=-1, keepdims=True) + RMS_EPS) * g


def _cast_kernel(w_ref, o_ref):
    o_ref[...] = w_ref[...].astype(BF16)


def _to_bf16(w, *, rows):
    dep, k, n = w.shape
    spec = pl.BlockSpec((1, rows, n), lambda l, i: (l, i, 0))
    return pl.pallas_call(
        _cast_kernel,
        grid=(dep, k // rows),
        in_specs=[spec],
        out_specs=spec,
        out_shape=jax.ShapeDtypeStruct(w.shape, BF16),
        compiler_params=_params(("parallel", "parallel")),
        name="cast_bf16",
    )(w)


def _w_in_kernel(w_ref, o_ref):
    w = w_ref[0]
    offs = np.concatenate([[0], np.cumsum(_REF_SPLITS)])
    qa, ka, va, qb, kc, vc, ks, vs, kw, vw, gl, xr, xg = [
        w[:, int(offs[i]):int(offs[i + 1])] for i in range(len(_REF_SPLITS))]
    gl = jnp.concatenate([gl, jnp.zeros((w.shape[0], LANES - gl.shape[1]), F32)], axis=1)
    out = jnp.concatenate([xr, xg, qa, qb, ka, va, kc, vc, ks, vs, kw, vw, gl], axis=1)
    o_ref[0] = out.astype(BF16)


def _stage_w_in(w, *, rows=256):
    dep, k, n = w.shape
    return pl.pallas_call(
        _w_in_kernel,
        grid=(dep, k // rows),
        in_specs=[pl.BlockSpec((1, rows, n), lambda l, i: (l, i, 0))],
        out_specs=pl.BlockSpec((1, rows, D_PROJ), lambda l, i: (l, i, 0)),
        out_shape=jax.ShapeDtypeStruct((dep, k, D_PROJ), BF16),
        compiler_params=_params(("parallel", "parallel")),
        name="stage_w_in",
    )(w)


def _ffn_kernel(x_ref, g_ref, wg_ref, wu_ref, wd_ref, *rest, final_norm):
    if final_norm:
        fg_ref, o_ref, h_ref = rest
    else:
        o_ref, h_ref = rest
    j = pl.program_id(1)

    @pl.when(j == 0)
    def _():
        x = x_ref[...]
        h_ref[...] = _rms(x, g_ref[...]).astype(BF16)
        o_ref[...] = x

    h = h_ref[...]
    gate = jnp.dot(h, wg_ref[...], preferred_element_type=F32)
    up = jnp.dot(h, wu_ref[...], preferred_element_type=F32)
    act = (0.5 * gate * jax.nn.sigmoid(gate) * up).astype(BF16)
    for c in range(0, o_ref.shape[1], DOWN_CHUNK):
        o_ref[:, c:c + DOWN_CHUNK] += jnp.dot(act, wd_ref[:, c:c + DOWN_CHUNK],
                                              preferred_element_type=F32)

    if final_norm:
        @pl.when(j == pl.num_programs(1) - 1)
        def _():
            o_ref[...] = _rms(o_ref[...], fg_ref[...])


def _ffn(x, g, wg, wu, wd, layer, final_g=None, *, tm=1024, tf=512):
    t, d = x.shape
    f = wg.shape[2]
    final_norm = final_g is not None
    in_specs = [
        pl.BlockSpec((tm, d), lambda i, j: (i, 0)),
        pl.BlockSpec((1, d), lambda i, j: (0, 0)),
        pl.BlockSpec((None, d, tf), lambda i, j: (layer, 0, j)),
        pl.BlockSpec((None, d, tf), lambda i, j: (layer, 0, j)),
        pl.BlockSpec((None, tf, d), lambda i, j: (layer, j, 0)),
    ]
    args = [x, g, wg, wu, wd]
    if final_norm:
        in_specs.append(pl.BlockSpec((1, d), lambda i, j: (0, 0)))
        args.append(final_g)
    return pl.pallas_call(
        functools.partial(_ffn_kernel, final_norm=final_norm),
        grid=(t // tm, f // tf),
        in_specs=in_specs,
        out_specs=pl.BlockSpec((tm, d), lambda i, j: (i, 0)),
        out_shape=jax.ShapeDtypeStruct((t, d), F32),
        scratch_shapes=[pltpu.VMEM((tm, d), BF16)],
        compiler_params=_params(("parallel", "arbitrary"), VMEM_LIMIT_FFN),
        name="ffn",
    )(*args)


def _inproj_kernel(x_ref, g_ref, w_ref, o_ref, h_ref):
    @pl.when(pl.program_id(1) == 0)
    def _():
        h_ref[...] = _rms(x_ref[...], g_ref[...]).astype(BF16)

    o_ref[...] = jnp.dot(h_ref[...], w_ref[...], preferred_element_type=F32)


def _inproj(x, g, w, layer, *, tm=1024, tn=1408):
    t, d = x.shape
    n = w.shape[2]
    return pl.pallas_call(
        _inproj_kernel,
        grid=(t // tm, n // tn),
        in_specs=[
            pl.BlockSpec((tm, d), lambda i, j: (i, 0)),
            pl.BlockSpec((1, d), lambda i, j: (0, 0)),
            pl.BlockSpec((None, d, tn), lambda i, j: (layer, 0, j)),
        ],
        out_specs=pl.BlockSpec((tm, tn), lambda i, j: (i, j)),
        out_shape=jax.ShapeDtypeStruct((t, n), F32),
        scratch_shapes=[pltpu.VMEM((tm, d), BF16)],
        compiler_params=_params(("parallel", "arbitrary")),
        name="inproj",
    )(x, g, w)


def _queries_t(q_ref, tok):
    return q_ref[0, tok, :].T * SCALE


def _group_t(qt, kv, extra=()):
    cols = []
    for g in range(GROUP):
        h = kv * GROUP + g
        parts = [qt[h * HEAD_DIM:(h + 1) * HEAD_DIM, :], *extra]
        cols.append(parts[0] if len(parts) == 1 else jnp.concatenate(parts, axis=0))
    return jnp.concatenate(cols, axis=1).astype(BF16)


def _store_heads(o_ref, tok, outs_t, gates_t, gate_row, tq):
    pieces = []
    for kv in range(N_KV):
        for g in range(GROUP):
            h = kv * GROUP + g
            piece = outs_t[kv][:, g * tq:(g + 1) * tq]
            if gates_t is not None:
                piece = piece * gates_t[gate_row + h:gate_row + h + 1, :]
            pieces.append(piece)
    o_ref[0, tok, :] = jnp.concatenate(pieces, axis=0).T


def _gates_t(gl_ref, tok):
    return jax.nn.sigmoid(gl_ref[0, tok, :].T)


def _window_kernel(*refs, window, n_prev, q_blocks, use_sink, gate_row):
    refs = list(refs)
    q_ref, k_ref, v_ref = refs[:3]
    rest = refs[3:]
    sink_ref = rest.pop(0) if use_sink else None
    gl_ref = rest.pop(0) if gate_row is not None else None
    o_ref, kb_ref, vt_ref = rest

    blk = ATTN_BLOCK
    cols = GROUP * blk
    span = (n_prev + 1) * blk
    n = pl.program_id(1)

    @pl.when(n == 0)
    def _():
        vt = v_ref[0].T
        for kv in range(N_KV):
            kb_ref[kv] = k_ref[0, :, kv * HEAD_DIM:(kv + 1) * HEAD_DIM].astype(BF16)
            vt_ref[kv] = vt[kv * HEAD_DIM:(kv + 1) * HEAD_DIM, :].astype(BF16)

    krow = lax.broadcasted_iota(jnp.int32, (span, cols), 0)
    qcol = lax.broadcasted_iota(jnp.int32, (span, cols), 1) & (blk - 1)
    starts, scores = [], []
    for j in range(q_blocks):
        nb = n * q_blocks + j
        start = pl.multiple_of(jnp.maximum(nb - n_prev, 0) * blk, blk)
        diff = (nb * blk + qcol) - (start + krow)
        mask = (diff >= 0) & (diff < window)
        qt = _queries_t(q_ref, slice(j * blk, (j + 1) * blk))
        starts.append(start)
        for kv in range(N_KV):
            s = jnp.dot(kb_ref[kv, pl.ds(start, span), :], _group_t(qt, kv),
                        preferred_element_type=F32)
            scores.append(jnp.where(mask, s, NEG_INF))
    probs = []
    for c, s in enumerate(scores):
        kv = c % N_KV
        m = jnp.max(s, axis=0, keepdims=True)
        if use_sink:
            sk = jnp.concatenate([jnp.full((1, blk), sink_ref[kv * GROUP + g], F32)
                                  for g in range(GROUP)], axis=1)
            m = jnp.maximum(m, sk)
            p = jnp.exp(s - m)
            denom = jnp.sum(p, axis=0, keepdims=True) + jnp.exp(sk - m)
        else:
            p = jnp.exp(s - m)
            denom = jnp.sum(p, axis=0, keepdims=True)
        probs.append((p.astype(BF16), denom))
    for j in range(q_blocks):
        tok = slice(j * blk, (j + 1) * blk)
        outs = []
        for kv in range(N_KV):
            p, denom = probs[j * N_KV + kv]
            o = jnp.dot(vt_ref[kv, :, pl.ds(starts[j], span)], p,
                        preferred_element_type=F32)
            outs.append(o * (1.0 / denom))
        gates = _gates_t(gl_ref, tok) if gate_row is not None else None
        _store_heads(o_ref, tok, outs, gates, gate_row, blk)


def _window_attention(proj, q_col, k_col, v_col, window, sinks=None, gate_row=None, *,
                      q_blocks=4):
    b, s, _ = proj.shape
    n_prev = -(-(window - 1) // ATTN_BLOCK)
    blk = q_blocks * ATTN_BLOCK
    in_specs = [
        pl.BlockSpec((1, blk, D_ATT), lambda i, n: (i, n, q_col // D_ATT)),
        pl.BlockSpec((1, s, D_KV), lambda i, n: (i, 0, k_col // D_KV)),
        pl.BlockSpec((1, s, D_KV), lambda i, n: (i, 0, v_col // D_KV)),
    ]
    args = [proj, proj, proj]
    if sinks is not None:
        in_specs.append(pl.BlockSpec(memory_space=pltpu.SMEM))
        args.append(sinks)
    if gate_row is not None:
        in_specs.append(pl.BlockSpec((1, blk, LANES), lambda i, n: (i, n, COL_GL // LANES)))
        args.append(proj)
    return pl.pallas_call(
        functools.partial(_window_kernel, window=window, n_prev=n_prev, q_blocks=q_blocks,
                          use_sink=sinks is not None, gate_row=gate_row),
        grid=(b, s // blk),
        in_specs=in_specs,
        out_specs=pl.BlockSpec((1, blk, D_ATT), lambda i, n: (i, n, 0)),
        out_shape=jax.ShapeDtypeStruct((b, s, D_ATT), F32),
        scratch_shapes=[pltpu.VMEM((N_KV, s, HEAD_DIM), BF16),
                        pltpu.VMEM((N_KV, HEAD_DIM, s), BF16)],
        compiler_params=_params(("parallel", "arbitrary")),
        name=f"window{window}",
    )(*args)


def _compress_kernel(k_ref, v_ref, pos_ref, w1_ref, b1_ref, w2_ref, kc_ref, vc_ref):
    n_rows = kc_ref.shape[1]
    for i, (src_ref, dst_ref) in enumerate(((k_ref, kc_ref), (v_ref, vc_ref))):
        lo = jnp.zeros((n_rows, N_KV * CMP_HIDDEN), F32)
        hi = jnp.zeros((n_rows, N_KV * CMP_HIDDEN), F32)
        for l in range(CMP_STRIDE):
            x = src_ref[0, pl.ds(l, n_rows, stride=CMP_STRIDE), :]
            lo = lo + jnp.dot((x + pos_ref[i, l:l + 1, :]).astype(BF16), w1_ref[i, l],
                              preferred_element_type=F32)
            hi = hi + jnp.dot((x + pos_ref[i, CMP_STRIDE + l:CMP_STRIDE + l + 1, :]).astype(BF16),
                              w1_ref[i, CMP_STRIDE + l], preferred_element_type=F32)
        hid = lo + pltpu.roll(hi, n_rows - 1, 0) + b1_ref[i]
        act = jax.nn.gelu(hid, approximate=True).astype(BF16)
        dst_ref[0] = jnp.dot(act, w2_ref[i], preferred_element_type=F32)


def _compress(proj, pos, w1, b1, w2):
    b, s, _ = proj.shape
    n_rows = s // CMP_STRIDE
    out_spec = pl.BlockSpec((1, n_rows, D_KV), lambda i: (i, 0, 0))
    out_shape = jax.ShapeDtypeStruct((b, n_rows, D_KV), F32)
    return pl.pallas_call(
        _compress_kernel,
        grid=(b,),
        in_specs=[
            pl.BlockSpec((1, s, D_KV), lambda i: (i, 0, COL_KC // D_KV)),
            pl.BlockSpec((1, s, D_KV), lambda i: (i, 0, COL_VC // D_KV)),
            pl.BlockSpec(pos.shape, lambda i: (0, 0, 0)),
            pl.BlockSpec(w1.shape, lambda i: (0, 0, 0, 0)),
            pl.BlockSpec(b1.shape, lambda i: (0, 0, 0)),
            pl.BlockSpec(w2.shape, lambda i: (0, 0, 0)),
        ],
        out_specs=[out_spec, out_spec],
        out_shape=[out_shape, out_shape],
        compiler_params=_params(("parallel",)),
        name="compress",
    )(proj, proj, pos, w1, b1, w2)


def _cmp_attn_kernel(q_ref, kc_ref, vc_ref, gl_ref, ov_ref, o_ref, bias_ref, *, tq, n_sel):
    n = pl.program_id(1)
    ncp = kc_ref.shape[1]
    cols = GROUP * tq
    pos_c = n * tq + (lax.broadcasted_iota(jnp.int32, (ncp, cols), 1) & (tq - 1))
    cend = lax.broadcasted_iota(jnp.int32, (ncp, cols), 0) * CMP_STRIDE + (CMP_LEN - 1)
    cmask = cend <= pos_c

    pos_s = n * tq + lax.broadcasted_iota(jnp.int32, (n_sel, tq), 1)
    bid = lax.broadcasted_iota(jnp.int32, (n_sel, tq), 0)
    cur = pos_s // SEL_BLOCK
    forced = (bid == 0) | (bid == cur) | (bid == cur - 1)
    valid = bid * SEL_BLOCK <= pos_s

    tok = slice(None)
    qt = _queries_t(q_ref, tok)
    vct = vc_ref[0].T
    outs = []
    for kv in range(N_KV):
        kc = kc_ref[0, :, kv * HEAD_DIM:(kv + 1) * HEAD_DIM].astype(BF16)
        s = jnp.dot(kc, _group_t(qt, kv), preferred_element_type=F32)
        s = jnp.where(cmask, s, NEG_INF)
        m = jnp.max(s, axis=0, keepdims=True)
        p = jnp.exp(s - m)
        p = p * (1.0 / jnp.sum(p, axis=0, keepdims=True))
        p = jnp.where(cmask, p, 0.0)
        outs.append(jnp.dot(vct[kv * HEAD_DIM:(kv + 1) * HEAD_DIM, :].astype(BF16),
                            p.astype(BF16), preferred_element_type=F32))
        psum = p[:, 0:tq]
        for g in range(1, GROUP):
            psum = psum + p[:, g * tq:(g + 1) * tq]

        imp = jnp.dot(ov_ref[...], psum, preferred_element_type=F32,
                      precision=lax.Precision.HIGHEST)
        imp = jnp.where(forced, FORCED_SCORE, imp)
        imp = jnp.where(valid, imp, NEG_INF)
        rank = jnp.zeros((n_sel, tq), jnp.int32)
        for i in range(n_sel):
            row = imp[i:i + 1, :]
            ahead = (row > imp) | ((row == imp) & (bid > i))
            rank = rank + ahead.astype(jnp.int32)
        bias_ref[0, kv * n_sel:(kv + 1) * n_sel, :] = jnp.where(rank < TOP_N, 0.0, NEG_INF)
    _store_heads(o_ref, tok, outs, _gates_t(gl_ref, tok), 0, tq)


def _cmp_attention(proj, kc, vc, overlap, *, tq=256):
    b, s, _ = proj.shape
    n_sel = overlap.shape[0]
    ncp = kc.shape[1]
    return pl.pallas_call(
        functools.partial(_cmp_attn_kernel, tq=tq, n_sel=n_sel),
        grid=(b, s // tq),
        in_specs=[
            pl.BlockSpec((1, tq, D_ATT), lambda i, n: (i, n, COL_QB // D_ATT)),
            pl.BlockSpec((1, ncp, D_KV), lambda i, n: (i, 0, 0)),
            pl.BlockSpec((1, ncp, D_KV), lambda i, n: (i, 0, 0)),
            pl.BlockSpec((1, tq, LANES), lambda i, n: (i, n, COL_GL // LANES)),
            pl.BlockSpec(overlap.shape, lambda i, n: (0, 0)),
        ],
        out_specs=[
            pl.BlockSpec((1, tq, D_ATT), lambda i, n: (i, n, 0)),
            pl.BlockSpec((1, N_KV * n_sel, tq), lambda i, n: (i, 0, n)),
        ],
        out_shape=[
            jax.ShapeDtypeStruct((b, s, D_ATT), F32),
            jax.ShapeDtypeStruct((b, N_KV * n_sel, s), F32),
        ],
        compiler_params=_params(("parallel", "arbitrary")),
        name="cmp_attn",
    )(proj, kc, vc, proj, overlap)


def _sel_attn_kernel(q_ref, k_ref, v_ref, bias_ref, gl_ref, o_ref,
                     qa_ref, ka_ref, vt_ref, *, tq, tk, chains, n_sel, gate_row):
    n = pl.program_id(1)
    s_len = k_ref.shape[1]
    cols = GROUP * tq
    n_pad = LANES - HEAD_DIM - n_sel

    @pl.when(n == 0)
    def _():
        kpos = lax.broadcasted_iota(jnp.int32, (s_len, n_sel), 0)
        blk = lax.broadcasted_iota(jnp.int32, (s_len, n_sel), 1)
        onehot = (kpos // SEL_BLOCK == blk).astype(F32)
        zeros = jnp.zeros((s_len, n_pad), F32)
        vt = v_ref[0].T
        for kv in range(N_KV):
            k = k_ref[0, :, kv * HEAD_DIM:(kv + 1) * HEAD_DIM]
            ka_ref[kv] = jnp.concatenate([k, onehot, zeros], axis=1).astype(BF16)
            for t in range(s_len // tk):
                vt_ref[kv, t] = vt[kv * HEAD_DIM:(kv + 1) * HEAD_DIM,
                                   t * tk:(t + 1) * tk].astype(BF16)

    tok = slice(None)
    qt = _queries_t(q_ref, tok)
    zq = jnp.zeros((n_pad, tq), F32)
    for kv in range(N_KV):
        bias = bias_ref[0, kv * n_sel:(kv + 1) * n_sel, :]
        qa_ref[kv] = _group_t(qt, kv, extra=(bias, zq))
    ccols = cols // chains
    qpos = n * tq + (lax.broadcasted_iota(jnp.int32, (tk, ccols), 1) & (tq - 1))
    krow = lax.broadcasted_iota(jnp.int32, (tk, ccols), 0)

    def step(kt, state, causal):
        k0 = pl.multiple_of(kt * tk, tk)
        scores = []
        for c in range(len(state)):
            kv, part = divmod(c, chains)
            cs = slice(part * ccols, (part + 1) * ccols)
            s = jnp.dot(ka_ref[kv, pl.ds(k0, tk), :], qa_ref[kv, :, cs],
                        preferred_element_type=F32)
            if causal:
                s = jnp.where(k0 + krow <= qpos, s, NEG_INF)
            scores.append(s)
        probs = []
        for s, (m_old, l_old, _) in zip(scores, state):
            m_new = jnp.maximum(m_old, jnp.max(s, axis=0, keepdims=True))
            alpha = jnp.exp(m_old - m_new)
            p = jnp.exp(s - m_new)
            l_new = alpha * l_old + jnp.sum(p, axis=0, keepdims=True)
            probs.append((m_new, l_new, alpha, p.astype(BF16)))
        new_state = []
        for c, ((m_new, l_new, alpha, p), (_, _, acc_old)) in enumerate(zip(probs, state)):
            kv = c // chains
            acc_new = alpha * acc_old + jnp.dot(vt_ref[kv, kt], p, preferred_element_type=F32)
            new_state.append((m_new, l_new, acc_new))
        return tuple(new_state)

    init = tuple((jnp.full((1, ccols), NEG_INF, F32), jnp.zeros((1, ccols), F32),
                  jnp.zeros((HEAD_DIM, ccols), F32)) for _ in range(N_KV * chains))
    n_full = (n * tq) // tk
    state = lax.fori_loop(0, n_full, functools.partial(step, causal=False), init)
    state = step(n_full, state, causal=True)

    outs = []
    for kv in range(N_KV):
        parts = [acc * (1.0 / l) for _, l, acc in state[kv * chains:(kv + 1) * chains]]
        outs.append(jnp.concatenate(parts, axis=1))
    _store_heads(o_ref, tok, outs, _gates_t(gl_ref, tok), gate_row, tq)


def _sel_attention(proj, bias, *, tq=512, tk=512, chains=4):
    b, s, _ = proj.shape
    n_sel = bias.shape[1] // N_KV
    cols = GROUP * tq
    return pl.pallas_call(
        functools.partial(_sel_attn_kernel, tq=tq, tk=tk, chains=chains, n_sel=n_sel,
                          gate_row=N_HEADS),
        grid=(b, s // tq),
        in_specs=[
            pl.BlockSpec((1, tq, D_ATT), lambda i, n: (i, n, COL_QB // D_ATT)),
            pl.BlockSpec((1, s, D_KV), lambda i, n: (i, 0, COL_KS // D_KV)),
            pl.BlockSpec((1, s, D_KV), lambda i, n: (i, 0, COL_VS // D_KV)),
            pl.BlockSpec((1, N_KV * n_sel, tq), lambda i, n: (i, 0, n)),
            pl.BlockSpec((1, tq, LANES), lambda i, n: (i, n, COL_GL // LANES)),
        ],
        out_specs=pl.BlockSpec((1, tq, D_ATT), lambda i, n: (i, n, 0)),
        out_shape=jax.ShapeDtypeStruct((b, s, D_ATT), F32),
        scratch_shapes=[
            pltpu.VMEM((N_KV, LANES, cols), BF16),
            pltpu.VMEM((N_KV, s, LANES), BF16),
            pltpu.VMEM((N_KV, s // tk, HEAD_DIM, tk), BF16),
        ],
        compiler_params=_params(("parallel", "arbitrary")),
        name="sel_attn",
    )(proj, proj, proj, bias, proj)


def _rglru_kernel(xr_ref, xg_ref, cw_ref, cb_ref, wa_ref, ba_ref, wx_ref, bx_ref, lam_ref,
                  o_ref, xbuf_ref, h_ref, *, ts):
    pad = SUBLANES

    @pl.when(pl.program_id(1) == 0)
    def _():
        xbuf_ref[0:pad, :] = jnp.zeros((pad, RNN_WIDTH), F32)
        h_ref[...] = jnp.zeros_like(h_ref)

    xr = xr_ref[0]
    xbuf_ref[pad:pad + ts, :] = xr
    xc = cb_ref[...] + cw_ref[CONV_WIDTH - 1:CONV_WIDTH, :] * xr
    for w in range(CONV_WIDTH - 1):
        shift = CONV_WIDTH - 1 - w
        xc = xc + cw_ref[w:w + 1, :] * xbuf_ref[pl.ds(pad - shift, ts), :]
    xbuf_ref[0:pad, :] = xr[ts - pad:ts, :]

    ra, rx = [], []
    for blk in range(RNN_WIDTH // MXU_DIM):
        xs = xc[:, blk * MXU_DIM:(blk + 1) * MXU_DIM].astype(BF16)
        ra.append(jnp.dot(xs, wa_ref[blk], preferred_element_type=F32))
        rx.append(jnp.dot(xs, wx_ref[blk], preferred_element_type=F32))
    r = 0.5 * (jnp.tanh(0.5 * (jnp.concatenate(ra, axis=1) + ba_ref[...])) + 1.0)
    gi = 0.5 * (jnp.tanh(0.5 * (jnp.concatenate(rx, axis=1) + bx_ref[...])) + 1.0)
    nl = -lam_ref[...]
    softplus = jnp.maximum(nl, 0.0) + jnp.log1p(jnp.exp(-jnp.abs(nl)))
    log_a = -LRU_C * r * softplus
    a = jnp.exp(log_a)
    u = jnp.sqrt(-jnp.tanh(log_a) * (a * a + 1.0)) * (gi * xc)

    n_slab = ts // SUBLANES
    a = a.reshape(n_slab, SUBLANES, RNN_WIDTH)
    u = u.reshape(n_slab, SUBLANES, RNN_WIDTH)
    sub = lax.broadcasted_iota(jnp.int32, a.shape, 1)
    d = 1
    while d < SUBLANES:
        keep = sub >= d
        a_prev = jnp.where(keep, pltpu.roll(a, d, 1), 1.0)
        u_prev = jnp.where(keep, pltpu.roll(u, d, 1), 0.0)
        u = a * u_prev + u
        a = a * a_prev
        d *= 2
    h_last = h_ref[0:1, :]
    slabs = []
    for r in range(n_slab):
        h = a[r] * h_last + u[r]
        slabs.append(h)
        h_last = h[SUBLANES - 1:SUBLANES, :]
    h_ref[0:1, :] = h_last
    o_ref[0] = jax.nn.gelu(xg_ref[0], approximate=True) * jnp.concatenate(slabs, axis=0)


def _rglru(proj, conv_w, conv_b, wa, ba, wx, bx, lam, *, ts=256):
    b, s, _ = proj.shape
    c = RNN_WIDTH
    vec = pl.BlockSpec((1, c), lambda i, n: (0, 0))
    wspec = pl.BlockSpec(wa.shape, lambda i, n: (0, 0, 0))
    return pl.pallas_call(
        functools.partial(_rglru_kernel, ts=ts),
        grid=(b, s // ts),
        in_specs=[
            pl.BlockSpec((1, ts, c), lambda i, n: (i, n, COL_XR // c)),
            pl.BlockSpec((1, ts, c), lambda i, n: (i, n, COL_XG // c)),
            pl.BlockSpec((CONV_WIDTH, c), lambda i, n: (0, 0)),
            vec, wspec, vec, wspec, vec, vec,
        ],
        out_specs=pl.BlockSpec((1, ts, c), lambda i, n: (i, n, 0)),
        out_shape=jax.ShapeDtypeStruct((b, s, c), F32),
        scratch_shapes=[pltpu.VMEM((ts + SUBLANES, c), F32), pltpu.VMEM((SUBLANES, c), F32)],
        compiler_params=_params(("parallel", "arbitrary")),
        name="rglru",
    )(proj, proj, conv_w, conv_b, wa, ba, wx, bx, lam)


def _outproj_kernel(x_ref, ya_ref, oc_ref, os_ref, ow_ref, yc_ref, gn_ref, w_ref, o_ref):
    ya = _rms(ya_ref[...], gn_ref[:, 0:D_ATT]).astype(BF16)
    yb = oc_ref[...] + os_ref[...] + ow_ref[...]
    yb = _rms(yb, gn_ref[:, D_ATT:2 * D_ATT]).astype(BF16)
    yc = _rms(yc_ref[...], gn_ref[:, 2 * D_ATT:]).astype(BF16)
    y = jnp.dot(ya, w_ref[0:D_ATT, :], preferred_element_type=F32)
    y = y + jnp.dot(yb, w_ref[D_ATT:2 * D_ATT, :], preferred_element_type=F32)
    y = y + jnp.dot(yc, w_ref[2 * D_ATT:, :], preferred_element_type=F32)
    o_ref[...] = x_ref[...] + y


def _outproj(x, ya, oc, osl, ow, yc, gn, w, layer, *, tm=256):
    t, d = x.shape
    att = pl.BlockSpec((tm, D_ATT), lambda i: (i, 0))
    return pl.pallas_call(
        _outproj_kernel,
        grid=(t // tm,),
        in_specs=[
            pl.BlockSpec((tm, d), lambda i: (i, 0)),
            att, att, att, att,
            pl.BlockSpec((tm, RNN_WIDTH), lambda i: (i, 0)),
            pl.BlockSpec((1, gn.shape[1]), lambda i: (0, 0)),
            pl.BlockSpec((None,) + w.shape[1:], lambda i: (layer, 0, 0)),
        ],
        out_specs=pl.BlockSpec((tm, d), lambda i: (i, 0)),
        out_shape=jax.ShapeDtypeStruct((t, d), F32),
        compiler_params=_params(("parallel",)),
        name="outproj",
    )(x, ya, oc, osl, ow, yc, gn, w)


def _block_diag(w, per):
    *lead, n, r, c = w.shape
    w = w.reshape(*lead, n // per, per, r, c)
    eye = jnp.eye(per, dtype=w.dtype)
    out = jnp.einsum('...prc,pq->...prqc', w, eye)
    return out.reshape(*lead, n // per, per * r, per * c)


def _compress_weights(cmp_pos, w1, b1, w2):
    pos = jnp.tile(cmp_pos, (1, 1, N_KV))
    w1 = w1.reshape(2, CMP_LEN, 1, HEAD_DIM, CMP_HIDDEN)
    w1 = _block_diag(jnp.broadcast_to(w1, (2, CMP_LEN, N_KV, HEAD_DIM, CMP_HIDDEN)), N_KV)
    w1 = w1.reshape(2, CMP_LEN, N_KV * HEAD_DIM, N_KV * CMP_HIDDEN)
    w2 = _block_diag(jnp.broadcast_to(w2[:, None], (2, N_KV, CMP_HIDDEN, HEAD_DIM)), N_KV)
    w2 = w2.reshape(2, N_KV * CMP_HIDDEN, N_KV * HEAD_DIM)
    b1 = jnp.tile(b1[:, None, :], (1, 1, N_KV))
    return pos, w1.astype(BF16), b1, w2.astype(BF16)


def _overlap_matrix(seq, n_cmp_padded):
    n_c = (seq - CMP_LEN) // CMP_STRIDE + 1
    n_sel = seq // SEL_BLOCK
    cs = np.arange(n_c)[:, None] * CMP_STRIDE
    ss = np.arange(n_sel)[None, :] * SEL_BLOCK
    ov = np.clip(np.minimum(cs + CMP_LEN, ss + SEL_BLOCK) - np.maximum(cs, ss), 0, None)
    out = np.zeros((n_cmp_padded, n_sel), np.float32)
    out[:n_c] = ov / CMP_LEN
    return jnp.asarray(out.T)


def kernel(x, ffn1_norm, ffn1_w_gate, ffn1_w_up, ffn1_w_down, mix_norm, w_in, swa_sinks, cmp_pos, cmp_w1, cmp_b1, cmp_w2, conv_w, conv_b, lru_wa, lru_ba, lru_wx, lru_bx, lru_lambda, group_norm, w_out, ffn2_norm, ffn2_w_gate, ffn2_w_up, ffn2_w_down, final_norm):
    b, s, d = x.shape
    depth = w_in.shape[0]
    t = b * s
    overlap = _overlap_matrix(s, s // CMP_STRIDE)
    gate_tile = MXU_DIM // RNN_BLOCK_WIDTH
    xt = x.reshape(t, d)
    ffn1_w = (_to_bf16(ffn1_w_gate, rows=256), _to_bf16(ffn1_w_up, rows=256),
              _to_bf16(ffn1_w_down, rows=512))
    ffn2_w = (_to_bf16(ffn2_w_gate, rows=256), _to_bf16(ffn2_w_up, rows=256),
              _to_bf16(ffn2_w_down, rows=512))
    w_in_b = _stage_w_in(w_in)
    w_out_b = _to_bf16(w_out, rows=512)
    for l in range(depth):
        xt = _ffn(xt, ffn1_norm[l][None], *ffn1_w, l)

        proj = _inproj(xt, mix_norm[l][None], w_in_b, l)
        proj = proj.reshape(b, s, D_PROJ)

        ya = _window_attention(proj, COL_QA, COL_KA, COL_VA, SWA_WINDOW, sinks=swa_sinks[l])

        kc, vc = _compress(proj, *_compress_weights(cmp_pos[l], cmp_w1[l], cmp_b1[l], cmp_w2[l]))
        o_cmp, bias = _cmp_attention(proj, kc, vc, overlap)
        o_slc = _sel_attention(proj, bias)
        o_win = _window_attention(proj, COL_QB, COL_KW, COL_VW, NSA_WINDOW,
                                  gate_row=2 * N_HEADS, q_blocks=2)

        yc = _rglru(proj, conv_w[l], conv_b[l][None],
                    _block_diag(lru_wa[l], gate_tile).astype(BF16), lru_ba[l][None],
                    _block_diag(lru_wx[l], gate_tile).astype(BF16), lru_bx[l][None],
                    lru_lambda[l][None])

        xt = _outproj(xt, ya.reshape(t, D_ATT), o_cmp.reshape(t, D_ATT),
                      o_slc.reshape(t, D_ATT), o_win.reshape(t, D_ATT),
                      yc.reshape(t, RNN_WIDTH), group_norm[l][None], w_out_b, l)

        last = l == depth - 1
        xt = _ffn(xt, ffn2_norm[l][None], *ffn2_w, l,
                  final_g=final_norm[None] if last else None)
    return xt.reshape(b, s, d)
```

```python
import functools

import jax
import jax.numpy as jnp
import numpy as np
from jax import lax
from jax.experimental import pallas as pl
from jax.experimental.pallas import tpu as pltpu

F32 = jnp.float32
BF16 = jnp.bfloat16

D_MODEL = 2048
HEAD_DIM = 64
D_FF = 5632
N_HEADS = 8
N_KV = 2
GROUP = N_HEADS // N_KV
D_ATT = N_HEADS * HEAD_DIM
D_KV = N_KV * HEAD_DIM
SWA_WINDOW = 128
NSA_WINDOW = 512
ATTN_BLOCK = 128
CMP_LEN = 32
CMP_STRIDE = 16
CMP_HIDDEN = 256
SEL_BLOCK = 64
TOP_N = 16
RNN_WIDTH = 1024
RNN_BLOCK_WIDTH = 64
CONV_WIDTH = 4
LRU_C = 8.0
RMS_EPS = 1e-6
NEG_INF = -1e30
FORCED_SCORE = 1e4
SCALE = HEAD_DIM ** -0.5

LANES = 128
SUBLANES = 8
MXU_DIM = 256
VMEM_BYTES = 64 * 1024 * 1024
VMEM_LIMIT = VMEM_BYTES - 8 * 1024 * 1024
VMEM_LIMIT_BIG = VMEM_BYTES - 4 * 1024 * 1024
DOWN_CHUNK = 512

COL_XR = 0
COL_XG = 1024
COL_QA = 2048
COL_QB = 2560
COL_KA = 3072
COL_VA = 3200
COL_KC = 3328
COL_VC = 3456
COL_KS = 3584
COL_VS = 3712
COL_KW = 3840
COL_VW = 3968
COL_GL = 4096
D_PROJ = 4224

_REF_SPLITS = (D_ATT, D_KV, D_KV, D_ATT, D_KV, D_KV, D_KV, D_KV, D_KV, D_KV, 3 * N_HEADS,
               RNN_WIDTH, RNN_WIDTH)


def _params(sem, vmem_limit=VMEM_LIMIT):
    return pltpu.CompilerParams(dimension_semantics=sem, vmem_limit_bytes=vmem_limit)


def _rms(x, g):
    return x * lax.rsqrt(jnp.mean(x * x, axis=-1, keepdims=True) + RMS_EPS) * g


def _cast_kernel(w_ref, o_ref):
    o_ref[...] = w_ref[...].astype(BF16)


def _to_bf16(w, *, rows, layer=None):
    dep, k, n = w.shape
    first, count = (0, dep) if layer is None else (layer, 1)
    return pl.pallas_call(
        _cast_kernel,
        grid=(count, k // rows),
        in_specs=[pl.BlockSpec((1, rows, n), lambda l, i: (first + l, i, 0))],
        out_specs=pl.BlockSpec((1, rows, n), lambda l, i: (l, i, 0)),
        out_shape=jax.ShapeDtypeStruct((count, k, n), BF16),
        compiler_params=_params(("parallel", "parallel")),
        name="cast_bf16",
    )(w)


def _w_in_kernel(w_ref, o_ref):
    w = w_ref[0]
    offs = np.concatenate([[0], np.cumsum(_REF_SPLITS)])
    qa, ka, va, qb, kc, vc, ks, vs, kw, vw, gl, xr, xg = [
        w[:, int(offs[i]):int(offs[i + 1])] for i in range(len(_REF_SPLITS))]
    gl = jnp.concatenate([gl, jnp.zeros((w.shape[0], LANES - gl.shape[1]), F32)], axis=1)
    out = jnp.concatenate([xr, xg, qa, qb, ka, va, kc, vc, ks, vs, kw, vw, gl], axis=1)
    o_ref[0] = out.astype(BF16)


def _stage_w_in(w, *, rows=256):
    dep, k, n = w.shape
    return pl.pallas_call(
        _w_in_kernel,
        grid=(dep, k // rows),
        in_specs=[pl.BlockSpec((1, rows, n), lambda l, i: (l, i, 0))],
        out_specs=pl.BlockSpec((1, rows, D_PROJ), lambda l, i: (l, i, 0)),
        out_shape=jax.ShapeDtypeStruct((dep, k, D_PROJ), BF16),
        compiler_params=_params(("parallel", "parallel")),
        name="stage_w_in",
    )(w)


def _ffn_kernel(x_ref, g_ref, wg_ref, wu_ref, wd_ref, *rest, final_norm):
    if final_norm:
        fg_ref, o_ref, h_ref = rest
    else:
        o_ref, h_ref = rest
    j = pl.program_id(1)

    @pl.when(j == 0)
    def _():
        x = x_ref[...]
        h_ref[...] = _rms(x, g_ref[...]).astype(BF16)
        o_ref[...] = x

    h = h_ref[...]
    gate = jnp.dot(h, wg_ref[...], preferred_element_type=F32)
    up = jnp.dot(h, wu_ref[...], preferred_element_type=F32)
    act = (0.5 * gate * jax.nn.sigmoid(gate) * up).astype(BF16)
    for c in range(0, o_ref.shape[1], DOWN_CHUNK):
        o_ref[:, c:c + DOWN_CHUNK] += jnp.dot(act, wd_ref[:, c:c + DOWN_CHUNK],
                                              preferred_element_type=F32)

    if final_norm:
        @pl.when(j == pl.num_programs(1) - 1)
        def _():
            o_ref[...] = _rms(o_ref[...], fg_ref[...])


def _ffn(x, g, wg, wu, wd, layer, final_g=None, *, tm=1024, tf=512):
    t, d = x.shape
    f = wg.shape[2]
    final_norm = final_g is not None
    in_specs = [
        pl.BlockSpec((tm, d), lambda i, j: (i, 0)),
        pl.BlockSpec((1, d), lambda i, j: (0, 0)),
        pl.BlockSpec((None, d, tf), lambda i, j: (layer, 0, j)),
        pl.BlockSpec((None, d, tf), lambda i, j: (layer, 0, j)),
        pl.BlockSpec((None, tf, d), lambda i, j: (layer, j, 0)),
    ]
    args = [x, g, wg, wu, wd]
    if final_norm:
        in_specs.append(pl.BlockSpec((1, d), lambda i, j: (0, 0)))
        args.append(final_g)
    return pl.pallas_call(
        functools.partial(_ffn_kernel, final_norm=final_norm),
        grid=(t // tm, f // tf),
        in_specs=in_specs,
        out_specs=pl.BlockSpec((tm, d), lambda i, j: (i, 0)),
        out_shape=jax.ShapeDtypeStruct((t, d), F32),
        scratch_shapes=[pltpu.VMEM((tm, d), BF16)],
        compiler_params=_params(("parallel", "arbitrary"), VMEM_LIMIT_BIG),
        name="ffn",
    )(*args)


def _inproj_kernel(x_ref, g_ref, w_ref, o_ref, h_ref):
    @pl.when(pl.program_id(1) == 0)
    def _():
        h_ref[...] = _rms(x_ref[...], g_ref[...]).astype(BF16)

    o_ref[...] = jnp.dot(h_ref[...], w_ref[...], preferred_element_type=F32)


def _inproj(x, g, w, layer, *, tm=1024, tn=1408):
    t, d = x.shape
    n = w.shape[2]
    return pl.pallas_call(
        _inproj_kernel,
        grid=(t // tm, n // tn),
        in_specs=[
            pl.BlockSpec((tm, d), lambda i, j: (i, 0)),
            pl.BlockSpec((1, d), lambda i, j: (0, 0)),
            pl.BlockSpec((None, d, tn), lambda i, j: (layer, 0, j)),
        ],
        out_specs=pl.BlockSpec((tm, tn), lambda i, j: (i, j)),
        out_shape=jax.ShapeDtypeStruct((t, n), F32),
        scratch_shapes=[pltpu.VMEM((tm, d), BF16)],
        compiler_params=_params(("parallel", "arbitrary")),
        name="inproj",
    )(x, g, w)


def _queries_t(q_ref, tok):
    return q_ref[0, tok, :].T * SCALE


def _group_t(qt, kv, extra=()):
    cols = []
    for g in range(GROUP):
        h = kv * GROUP + g
        parts = [qt[h * HEAD_DIM:(h + 1) * HEAD_DIM, :], *extra]
        cols.append(parts[0] if len(parts) == 1 else jnp.concatenate(parts, axis=0))
    return jnp.concatenate(cols, axis=1).astype(BF16)


def _store_heads(o_ref, tok, outs_t, gates_t, gate_row, tq):
    pieces = []
    for kv in range(N_KV):
        for g in range(GROUP):
            h = kv * GROUP + g
            piece = outs_t[kv][:, g * tq:(g + 1) * tq]
            if gates_t is not None:
                piece = piece * gates_t[gate_row + h:gate_row + h + 1, :]
            pieces.append(piece)
    o_ref[0, tok, :] = jnp.concatenate(pieces, axis=0).T


def _gates_t(gl_ref, tok):
    return jax.nn.sigmoid(gl_ref[0, tok, :].T)


def _window_kernel(*refs, window, n_prev, q_blocks, use_sink, gate_row):
    refs = list(refs)
    q_ref, k_ref, v_ref = refs[:3]
    rest = refs[3:]
    sink_ref = rest.pop(0) if use_sink else None
    gl_ref = rest.pop(0) if gate_row is not None else None
    o_ref, kb_ref, vt_ref = rest

    blk = ATTN_BLOCK
    cols = GROUP * blk
    span = (n_prev + 1) * blk
    n = pl.program_id(1)

    @pl.when(n == 0)
    def _():
        vt = v_ref[0].T
        for kv in range(N_KV):
            kb_ref[kv] = k_ref[0, :, kv * HEAD_DIM:(kv + 1) * HEAD_DIM].astype(BF16)
            vt_ref[kv] = vt[kv * HEAD_DIM:(kv + 1) * HEAD_DIM, :].astype(BF16)

    krow = lax.broadcasted_iota(jnp.int32, (span, cols), 0)
    qcol = lax.broadcasted_iota(jnp.int32, (span, cols), 1) & (blk - 1)
    starts, scores = [], []
    for j in range(q_blocks):
        nb = n * q_blocks + j
        start = pl.multiple_of(jnp.maximum(nb - n_prev, 0) * blk, blk)
        diff = (nb * blk + qcol) - (start + krow)
        mask = (diff >= 0) & (diff < window)
        qt = _queries_t(q_ref, slice(j * blk, (j + 1) * blk))
        starts.append(start)
        for kv in range(N_KV):
            s = jnp.dot(kb_ref[kv, pl.ds(start, span), :], _group_t(qt, kv),
                        preferred_element_type=F32)
            scores.append(jnp.where(mask, s, NEG_INF))
    probs = []
    for c, s in enumerate(scores):
        kv = c % N_KV
        m = jnp.max(s, axis=0, keepdims=True)
        if use_sink:
            sk = jnp.concatenate([jnp.full((1, blk), sink_ref[kv * GROUP + g], F32)
                                  for g in range(GROUP)], axis=1)
            m = jnp.maximum(m, sk)
            p = jnp.exp(s - m)
            denom = jnp.sum(p, axis=0, keepdims=True) + jnp.exp(sk - m)
        else:
            p = jnp.exp(s - m)
            denom = jnp.sum(p, axis=0, keepdims=True)
        probs.append((p.astype(BF16), denom))
    for j in range(q_blocks):
        tok = slice(j * blk, (j + 1) * blk)
        outs = []
        for kv in range(N_KV):
            p, denom = probs[j * N_KV + kv]
            o = jnp.dot(vt_ref[kv, :, pl.ds(starts[j], span)], p,
                        preferred_element_type=F32)
            outs.append(o * (1.0 / denom))
        gates = _gates_t(gl_ref, tok) if gate_row is not None else None
        _store_heads(o_ref, tok, outs, gates, gate_row, blk)


def _window_attention(proj, q_col, k_col, v_col, window, sinks=None, gate_row=None, *,
                      q_blocks=4):
    b, s, _ = proj.shape
    n_prev = -(-(window - 1) // ATTN_BLOCK)
    blk = q_blocks * ATTN_BLOCK
    in_specs = [
        pl.BlockSpec((1, blk, D_ATT), lambda i, n: (i, n, q_col // D_ATT)),
        pl.BlockSpec((1, s, D_KV), lambda i, n: (i, 0, k_col // D_KV)),
        pl.BlockSpec((1, s, D_KV), lambda i, n: (i, 0, v_col // D_KV)),
    ]
    args = [proj, proj, proj]
    if sinks is not None:
        in_specs.append(pl.BlockSpec(memory_space=pltpu.SMEM))
        args.append(sinks)
    if gate_row is not None:
        in_specs.append(pl.BlockSpec((1, blk, LANES), lambda i, n: (i, n, COL_GL // LANES)))
        args.append(proj)
    return pl.pallas_call(
        functools.partial(_window_kernel, window=window, n_prev=n_prev, q_blocks=q_blocks,
                          use_sink=sinks is not None, gate_row=gate_row),
        grid=(b, s // blk),
        in_specs=in_specs,
        out_specs=pl.BlockSpec((1, blk, D_ATT), lambda i, n: (i, n, 0)),
        out_shape=jax.ShapeDtypeStruct((b, s, D_ATT), F32),
        scratch_shapes=[pltpu.VMEM((N_KV, s, HEAD_DIM), BF16),
                        pltpu.VMEM((N_KV, HEAD_DIM, s), BF16)],
        compiler_params=_params(("parallel", "arbitrary")),
        name=f"window{window}",
    )(*args)


def _compress_kernel(k_ref, v_ref, pos_ref, w1_ref, b1_ref, w2_ref, kc_ref, vc_ref):
    n_rows = kc_ref.shape[1]
    for i, (src_ref, dst_ref) in enumerate(((k_ref, kc_ref), (v_ref, vc_ref))):
        lo = jnp.zeros((n_rows, N_KV * CMP_HIDDEN), F32)
        hi = jnp.zeros((n_rows, N_KV * CMP_HIDDEN), F32)
        for l in range(CMP_STRIDE):
            x = src_ref[0, pl.ds(l, n_rows, stride=CMP_STRIDE), :]
            lo = lo + jnp.dot((x + pos_ref[i, l:l + 1, :]).astype(BF16), w1_ref[i, l],
                              preferred_element_type=F32)
            hi = hi + jnp.dot((x + pos_ref[i, CMP_STRIDE + l:CMP_STRIDE + l + 1, :]).astype(BF16),
                              w1_ref[i, CMP_STRIDE + l], preferred_element_type=F32)
        hid = lo + pltpu.roll(hi, n_rows - 1, 0) + b1_ref[i]
        act = jax.nn.gelu(hid, approximate=True).astype(BF16)
        dst_ref[0] = jnp.dot(act, w2_ref[i], preferred_element_type=F32)


def _compress(proj, pos, w1, b1, w2):
    b, s, _ = proj.shape
    n_rows = s // CMP_STRIDE
    out_spec = pl.BlockSpec((1, n_rows, D_KV), lambda i: (i, 0, 0))
    out_shape = jax.ShapeDtypeStruct((b, n_rows, D_KV), F32)
    return pl.pallas_call(
        _compress_kernel,
        grid=(b,),
        in_specs=[
            pl.BlockSpec((1, s, D_KV), lambda i: (i, 0, COL_KC // D_KV)),
            pl.BlockSpec((1, s, D_KV), lambda i: (i, 0, COL_VC // D_KV)),
            pl.BlockSpec(pos.shape, lambda i: (0, 0, 0)),
            pl.BlockSpec(w1.shape, lambda i: (0, 0, 0, 0)),
            pl.BlockSpec(b1.shape, lambda i: (0, 0, 0)),
            pl.BlockSpec(w2.shape, lambda i: (0, 0, 0)),
        ],
        out_specs=[out_spec, out_spec],
        out_shape=[out_shape, out_shape],
        compiler_params=_params(("parallel",)),
        name="compress",
    )(proj, proj, pos, w1, b1, w2)


def _cmp_attn_kernel(q_ref, kc_ref, vc_ref, gl_ref, ov_ref, o_ref, bias_ref, *, tq, n_sel):
    n = pl.program_id(1)
    ncp = kc_ref.shape[1]
    cols = GROUP * tq
    pos_c = n * tq + (lax.broadcasted_iota(jnp.int32, (ncp, cols), 1) & (tq - 1))
    cend = lax.broadcasted_iota(jnp.int32, (ncp, cols), 0) * CMP_STRIDE + (CMP_LEN - 1)
    cmask = cend <= pos_c

    pos_s = n * tq + lax.broadcasted_iota(jnp.int32, (n_sel, tq), 1)
    bid = lax.broadcasted_iota(jnp.int32, (n_sel, tq), 0)
    cur = pos_s // SEL_BLOCK
    forced = (bid == 0) | (bid == cur) | (bid == cur - 1)
    valid = bid * SEL_BLOCK <= pos_s

    tok = slice(None)
    qt = _queries_t(q_ref, tok)
    vct = vc_ref[0].T
    outs = []
    for kv in range(N_KV):
        kc = kc_ref[0, :, kv * HEAD_DIM:(kv + 1) * HEAD_DIM].astype(BF16)
        s = jnp.dot(kc, _group_t(qt, kv), preferred_element_type=F32)
        s = jnp.where(cmask, s, NEG_INF)
        m = jnp.max(s, axis=0, keepdims=True)
        p = jnp.exp(s - m)
        p = p * (1.0 / jnp.sum(p, axis=0, keepdims=True))
        p = jnp.where(cmask, p, 0.0)
        outs.append(jnp.dot(vct[kv * HEAD_DIM:(kv + 1) * HEAD_DIM, :].astype(BF16),
                            p.astype(BF16), preferred_element_type=F32))
        psum = p[:, 0:tq]
        for g in range(1, GROUP):
            psum = psum + p[:, g * tq:(g + 1) * tq]

        imp = jnp.dot(ov_ref[...], psum, preferred_element_type=F32,
                      precision=lax.Precision.HIGHEST)
        imp = jnp.where(forced, FORCED_SCORE, imp)
        imp = jnp.where(valid, imp, NEG_INF)
        rank = jnp.zeros((n_sel, tq), jnp.int32)
        for i in range(n_sel):
            row = imp[i:i + 1, :]
            ahead = (row > imp) | ((row == imp) & (bid > i))
            rank = rank + ahead.astype(jnp.int32)
        bias_ref[0, kv * n_sel:(kv + 1) * n_sel, :] = jnp.where(rank < TOP_N, 0.0, NEG_INF)
    _store_heads(o_ref, tok, outs, _gates_t(gl_ref, tok), 0, tq)


def _cmp_attention(proj, kc, vc, overlap, *, tq=256):
    b, s, _ = proj.shape
    n_sel = overlap.shape[0]
    ncp = kc.shape[1]
    return pl.pallas_call(
        functools.partial(_cmp_attn_kernel, tq=tq, n_sel=n_sel),
        grid=(b, s // tq),
        in_specs=[
            pl.BlockSpec((1, tq, D_ATT), lambda i, n: (i, n, COL_QB // D_ATT)),
            pl.BlockSpec((1, ncp, D_KV), lambda i, n: (i, 0, 0)),
            pl.BlockSpec((1, ncp, D_KV), lambda i, n: (i, 0, 0)),
            pl.BlockSpec((1, tq, LANES), lambda i, n: (i, n, COL_GL // LANES)),
            pl.BlockSpec(overlap.shape, lambda i, n: (0, 0)),
        ],
        out_specs=[
            pl.BlockSpec((1, tq, D_ATT), lambda i, n: (i, n, 0)),
            pl.BlockSpec((1, N_KV * n_sel, tq), lambda i, n: (i, 0, n)),
        ],
        out_shape=[
            jax.ShapeDtypeStruct((b, s, D_ATT), F32),
            jax.ShapeDtypeStruct((b, N_KV * n_sel, s), F32),
        ],
        compiler_params=_params(("parallel", "arbitrary")),
        name="cmp_attn",
    )(proj, kc, vc, proj, overlap)


def _sel_attn_kernel(q_ref, k_ref, v_ref, bias_ref, gl_ref, *rest,
                     tq, tk, chains, n_sel, gate_row, n_cast):
    cast_src = rest[:n_cast]
    o_ref = rest[n_cast]
    cast_dst = rest[n_cast + 1:2 * n_cast + 1]
    qa_ref, ka_ref, vt_ref = rest[2 * n_cast + 1:]
    for src, dst in zip(cast_src, cast_dst):
        dst[...] = src[...].astype(BF16)

    n = pl.program_id(1)
    s_len = k_ref.shape[1]
    cols = GROUP * tq
    n_pad = LANES - HEAD_DIM - n_sel

    @pl.when(n == 0)
    def _():
        kpos = lax.broadcasted_iota(jnp.int32, (s_len, n_sel), 0)
        blk = lax.broadcasted_iota(jnp.int32, (s_len, n_sel), 1)
        onehot = (kpos // SEL_BLOCK == blk).astype(F32)
        zeros = jnp.zeros((s_len, n_pad), F32)
        vt = v_ref[0].T
        for kv in range(N_KV):
            k = k_ref[0, :, kv * HEAD_DIM:(kv + 1) * HEAD_DIM]
            ka_ref[kv] = jnp.concatenate([k, onehot, zeros], axis=1).astype(BF16)
            for t in range(s_len // tk):
                vt_ref[kv, t] = vt[kv * HEAD_DIM:(kv + 1) * HEAD_DIM,
                                   t * tk:(t + 1) * tk].astype(BF16)

    tok = slice(None)
    qt = _queries_t(q_ref, tok)
    zq = jnp.zeros((n_pad, tq), F32)
    for kv in range(N_KV):
        bias = bias_ref[0, kv * n_sel:(kv + 1) * n_sel, :]
        qa_ref[kv] = _group_t(qt, kv, extra=(bias, zq))
    ccols = cols // chains
    qpos = n * tq + (lax.broadcasted_iota(jnp.int32, (tk, ccols), 1) & (tq - 1))
    krow = lax.broadcasted_iota(jnp.int32, (tk, ccols), 0)

    def step(kt, state, causal):
        k0 = pl.multiple_of(kt * tk, tk)
        scores = []
        for c in range(len(state)):
            kv, part = divmod(c, chains)
            cs = slice(part * ccols, (part + 1) * ccols)
            s = jnp.dot(ka_ref[kv, pl.ds(k0, tk), :], qa_ref[kv, :, cs],
                        preferred_element_type=F32)
            if causal:
                s = jnp.where(k0 + krow <= qpos, s, NEG_INF)
            scores.append(s)
        probs = []
        for s, (m_old, l_old, _) in zip(scores, state):
            m_new = jnp.maximum(m_old, jnp.max(s, axis=0, keepdims=True))
            alpha = jnp.exp(m_old - m_new)
            p = jnp.exp(s - m_new)
            l_new = alpha * l_old + jnp.sum(p, axis=0, keepdims=True)
            probs.append((m_new, l_new, alpha, p.astype(BF16)))
        new_state = []
        for c, ((m_new, l_new, alpha, p), (_, _, acc_old)) in enumerate(zip(probs, state)):
            kv = c // chains
            acc_new = alpha * acc_old + jnp.dot(vt_ref[kv, kt], p, preferred_element_type=F32)
            new_state.append((m_new, l_new, acc_new))
        return tuple(new_state)

    init = tuple((jnp.full((1, ccols), NEG_INF, F32), jnp.zeros((1, ccols), F32),
                  jnp.zeros((HEAD_DIM, ccols), F32)) for _ in range(N_KV * chains))
    n_full = (n * tq) // tk
    state = lax.fori_loop(0, n_full, functools.partial(step, causal=False), init)
    state = step(n_full, state, causal=True)

    outs = []
    for kv in range(N_KV):
        parts = [acc * (1.0 / l) for _, l, acc in state[kv * chains:(kv + 1) * chains]]
        outs.append(jnp.concatenate(parts, axis=1))
    _store_heads(o_ref, tok, outs, _gates_t(gl_ref, tok), gate_row, tq)


def _sel_attention(proj, bias, cast_jobs=(), *, tq=512, tk=512, chains=4):
    b, s, _ = proj.shape
    n_sel = bias.shape[1] // N_KV
    cols = GROUP * tq
    nq = s // tq
    steps = b * nq
    cast_in, cast_out, cast_shapes = [], [], []
    for w, layer in cast_jobs:
        _, k, nn = w.shape
        rows = k // steps
        cast_in.append(pl.BlockSpec((None, rows, nn),
                                    lambda i, n, layer=layer: (layer, i * nq + n, 0)))
        cast_out.append(pl.BlockSpec((None, rows, nn), lambda i, n: (0, i * nq + n, 0)))
        cast_shapes.append(jax.ShapeDtypeStruct((1, k, nn), BF16))
    outs = pl.pallas_call(
        functools.partial(_sel_attn_kernel, tq=tq, tk=tk, chains=chains, n_sel=n_sel,
                          gate_row=N_HEADS, n_cast=len(cast_jobs)),
        grid=(b, nq),
        in_specs=[
            pl.BlockSpec((1, tq, D_ATT), lambda i, n: (i, n, COL_QB // D_ATT)),
            pl.BlockSpec((1, s, D_KV), lambda i, n: (i, 0, COL_KS // D_KV)),
            pl.BlockSpec((1, s, D_KV), lambda i, n: (i, 0, COL_VS // D_KV)),
            pl.BlockSpec((1, N_KV * n_sel, tq), lambda i, n: (i, 0, n)),
            pl.BlockSpec((1, tq, LANES), lambda i, n: (i, n, COL_GL // LANES)),
            *cast_in,
        ],
        out_specs=[pl.BlockSpec((1, tq, D_ATT), lambda i, n: (i, n, 0)), *cast_out],
        out_shape=[jax.ShapeDtypeStruct((b, s, D_ATT), F32), *cast_shapes],
        scratch_shapes=[
            pltpu.VMEM((N_KV, LANES, cols), BF16),
            pltpu.VMEM((N_KV, s, LANES), BF16),
            pltpu.VMEM((N_KV, s // tk, HEAD_DIM, tk), BF16),
        ],
        compiler_params=_params(("parallel", "arbitrary"), VMEM_LIMIT_BIG),
        name="sel_attn",
    )(proj, proj, proj, bias, proj, *[w for w, _ in cast_jobs])
    return outs[0], list(outs[1:])


def _rglru_kernel(xr_ref, xg_ref, cw_ref, cb_ref, wa_ref, ba_ref, wx_ref, bx_ref, lam_ref,
                  o_ref, xbuf_ref, h_ref, *, ts):
    pad = SUBLANES

    @pl.when(pl.program_id(1) == 0)
    def _():
        xbuf_ref[0:pad, :] = jnp.zeros((pad, RNN_WIDTH), F32)
        h_ref[...] = jnp.zeros_like(h_ref)

    xr = xr_ref[0]
    xbuf_ref[pad:pad + ts, :] = xr
    xc = cb_ref[...] + cw_ref[CONV_WIDTH - 1:CONV_WIDTH, :] * xr
    for w in range(CONV_WIDTH - 1):
        shift = CONV_WIDTH - 1 - w
        xc = xc + cw_ref[w:w + 1, :] * xbuf_ref[pl.ds(pad - shift, ts), :]
    xbuf_ref[0:pad, :] = xr[ts - pad:ts, :]

    ra, rx = [], []
    for blk in range(RNN_WIDTH // MXU_DIM):
        xs = xc[:, blk * MXU_DIM:(blk + 1) * MXU_DIM].astype(BF16)
        ra.append(jnp.dot(xs, wa_ref[blk], preferred_element_type=F32))
        rx.append(jnp.dot(xs, wx_ref[blk], preferred_element_type=F32))
    r = 0.5 * (jnp.tanh(0.5 * (jnp.concatenate(ra, axis=1) + ba_ref[...])) + 1.0)
    gi = 0.5 * (jnp.tanh(0.5 * (jnp.concatenate(rx, axis=1) + bx_ref[...])) + 1.0)
    nl = -lam_ref[...]
    softplus = jnp.maximum(nl, 0.0) + jnp.log1p(jnp.exp(-jnp.abs(nl)))
    log_a = -LRU_C * r * softplus
    a = jnp.exp(log_a)
    u = jnp.sqrt(-jnp.tanh(log_a) * (a * a + 1.0)) * (gi * xc)

    n_slab = ts // SUBLANES
    a = a.reshape(n_slab, SUBLANES, RNN_WIDTH)
    u = u.reshape(n_slab, SUBLANES, RNN_WIDTH)
    sub = lax.broadcasted_iota(jnp.int32, a.shape, 1)
    d = 1
    while d < SUBLANES:
        keep = sub >= d
        a_prev = jnp.where(keep, pltpu.roll(a, d, 1), 1.0)
        u_prev = jnp.where(keep, pltpu.roll(u, d, 1), 0.0)
        u = a * u_prev + u
        a = a * a_prev
        d *= 2
    h_last = h_ref[0:1, :]
    slabs = []
    for r in range(n_slab):
        h = a[r] * h_last + u[r]
        slabs.append(h)
        h_last = h[SUBLANES - 1:SUBLANES, :]
    h_ref[0:1, :] = h_last
    o_ref[0] = jax.nn.gelu(xg_ref[0], approximate=True) * jnp.concatenate(slabs, axis=0)


def _rglru(proj, conv_w, conv_b, wa, ba, wx, bx, lam, *, ts=256):
    b, s, _ = proj.shape
    c = RNN_WIDTH
    vec = pl.BlockSpec((1, c), lambda i, n: (0, 0))
    wspec = pl.BlockSpec(wa.shape, lambda i, n: (0, 0, 0))
    return pl.pallas_call(
        functools.partial(_rglru_kernel, ts=ts),
        grid=(b, s // ts),
        in_specs=[
            pl.BlockSpec((1, ts, c), lambda i, n: (i, n, COL_XR // c)),
            pl.BlockSpec((1, ts, c), lambda i, n: (i, n, COL_XG // c)),
            pl.BlockSpec((CONV_WIDTH, c), lambda i, n: (0, 0)),
            vec, wspec, vec, wspec, vec, vec,
        ],
        out_specs=pl.BlockSpec((1, ts, c), lambda i, n: (i, n, 0)),
        out_shape=jax.ShapeDtypeStruct((b, s, c), F32),
        scratch_shapes=[pltpu.VMEM((ts + SUBLANES, c), F32), pltpu.VMEM((SUBLANES, c), F32)],
        compiler_params=_params(("parallel", "arbitrary")),
        name="rglru",
    )(proj, proj, conv_w, conv_b, wa, ba, wx, bx, lam)


def _outproj_kernel(x_ref, ya_ref, oc_ref, os_ref, ow_ref, yc_ref, gn_ref, w_ref, o_ref):
    ya = _rms(ya_ref[...], gn_ref[:, 0:D_ATT]).astype(BF16)
    yb = oc_ref[...] + os_ref[...] + ow_ref[...]
    yb = _rms(yb, gn_ref[:, D_ATT:2 * D_ATT]).astype(BF16)
    yc = _rms(yc_ref[...], gn_ref[:, 2 * D_ATT:]).astype(BF16)
    y = jnp.dot(ya, w_ref[0:D_ATT, :], preferred_element_type=F32)
    y = y + jnp.dot(yb, w_ref[D_ATT:2 * D_ATT, :], preferred_element_type=F32)
    y = y + jnp.dot(yc, w_ref[2 * D_ATT:, :], preferred_element_type=F32)
    o_ref[...] = x_ref[...] + y


def _outproj(x, ya, oc, osl, ow, yc, gn, w, layer, *, tm=256):
    t, d = x.shape
    att = pl.BlockSpec((tm, D_ATT), lambda i: (i, 0))
    return pl.pallas_call(
        _outproj_kernel,
        grid=(t // tm,),
        in_specs=[
            pl.BlockSpec((tm, d), lambda i: (i, 0)),
            att, att, att, att,
            pl.BlockSpec((tm, RNN_WIDTH), lambda i: (i, 0)),
            pl.BlockSpec((1, gn.shape[1]), lambda i: (0, 0)),
            pl.BlockSpec((None,) + w.shape[1:], lambda i: (layer, 0, 0)),
        ],
        out_specs=pl.BlockSpec((tm, d), lambda i: (i, 0)),
        out_shape=jax.ShapeDtypeStruct((t, d), F32),
        compiler_params=_params(("parallel",)),
        name="outproj",
    )(x, ya, oc, osl, ow, yc, gn, w)


def _block_diag(w, per):
    *lead, n, r, c = w.shape
    w = w.reshape(*lead, n // per, per, r, c)
    eye = jnp.eye(per, dtype=w.dtype)
    out = jnp.einsum('...prc,pq->...prqc', w, eye)
    return out.reshape(*lead, n // per, per * r, per * c)


def _compress_weights(cmp_pos, w1, b1, w2):
    pos = jnp.tile(cmp_pos, (1, 1, N_KV))
    w1 = w1.reshape(2, CMP_LEN, 1, HEAD_DIM, CMP_HIDDEN)
    w1 = _block_diag(jnp.broadcast_to(w1, (2, CMP_LEN, N_KV, HEAD_DIM, CMP_HIDDEN)), N_KV)
    w1 = w1.reshape(2, CMP_LEN, N_KV * HEAD_DIM, N_KV * CMP_HIDDEN)
    w2 = _block_diag(jnp.broadcast_to(w2[:, None], (2, N_KV, CMP_HIDDEN, HEAD_DIM)), N_KV)
    w2 = w2.reshape(2, N_KV * CMP_HIDDEN, N_KV * HEAD_DIM)
    b1 = jnp.tile(b1[:, None, :], (1, 1, N_KV))
    return pos, w1.astype(BF16), b1, w2.astype(BF16)


def _overlap_matrix(seq, n_cmp_padded):
    n_c = (seq - CMP_LEN) // CMP_STRIDE + 1
    n_sel = seq // SEL_BLOCK
    cs = np.arange(n_c)[:, None] * CMP_STRIDE
    ss = np.arange(n_sel)[None, :] * SEL_BLOCK
    ov = np.clip(np.minimum(cs + CMP_LEN, ss + SEL_BLOCK) - np.maximum(cs, ss), 0, None)
    out = np.zeros((n_cmp_padded, n_sel), np.float32)
    out[:n_c] = ov / CMP_LEN
    return jnp.asarray(out.T)


def kernel(x, ffn1_norm, ffn1_w_gate, ffn1_w_up, ffn1_w_down, mix_norm, w_in, swa_sinks, cmp_pos, cmp_w1, cmp_b1, cmp_w2, conv_w, conv_b, lru_wa, lru_ba, lru_wx, lru_bx, lru_lambda, group_norm, w_out, ffn2_norm, ffn2_w_gate, ffn2_w_up, ffn2_w_down, final_norm):
    b, s, d = x.shape
    depth = w_in.shape[0]
    t = b * s
    overlap = _overlap_matrix(s, s // CMP_STRIDE)
    gate_tile = MXU_DIM // RNN_BLOCK_WIDTH
    xt = x.reshape(t, d)
    ffn1_f32 = (ffn1_w_gate, ffn1_w_up, ffn1_w_down)
    ffn2_f32 = (ffn2_w_gate, ffn2_w_up, ffn2_w_down)
    ffn1_w = [_to_bf16(ffn1_w_gate, rows=256, layer=0), _to_bf16(ffn1_w_up, rows=256, layer=0),
              _to_bf16(ffn1_w_down, rows=512, layer=0)]
    w_in_b = _stage_w_in(w_in)
    w_out_b = _to_bf16(w_out, rows=512)
    for l in range(depth):
        xt = _ffn(xt, ffn1_norm[l][None], *ffn1_w, 0)

        proj = _inproj(xt, mix_norm[l][None], w_in_b, l)
        proj = proj.reshape(b, s, D_PROJ)

        ya = _window_attention(proj, COL_QA, COL_KA, COL_VA, SWA_WINDOW, sinks=swa_sinks[l])

        kc, vc = _compress(proj, *_compress_weights(cmp_pos[l], cmp_w1[l], cmp_b1[l], cmp_w2[l]))
        o_cmp, bias = _cmp_attention(proj, kc, vc, overlap)
        jobs = [(w, l) for w in ffn2_f32]
        if l + 1 < depth:
            jobs += [(w, l + 1) for w in ffn1_f32]
        o_slc, staged = _sel_attention(proj, bias, jobs)
        ffn2_w, ffn1_w = staged[:3], staged[3:]
        o_win = _window_attention(proj, COL_QB, COL_KW, COL_VW, NSA_WINDOW,
                                  gate_row=2 * N_HEADS, q_blocks=2)

        yc = _rglru(proj, conv_w[l], conv_b[l][None],
                    _block_diag(lru_wa[l], gate_tile).astype(BF16), lru_ba[l][None],
                    _block_diag(lru_wx[l], gate_tile).astype(BF16), lru_bx[l][None],
                    lru_lambda[l][None])

        xt = _outproj(xt, ya.reshape(t, D_ATT), o_cmp.reshape(t, D_ATT),
                      o_slc.reshape(t, D_ATT), o_win.reshape(t, D_ATT),
                      yc.reshape(t, RNN_WIDTH), group_norm[l][None], w_out_b, l)

        last = l == depth - 1
        xt = _ffn(xt, ffn2_norm[l][None], *ffn2_w, 0,
                  final_g=final_norm[None] if last else None)
    return xt.reshape(b, s, d)
```

```python
import functools

import jax
import jax.numpy as jnp
import numpy as np
from jax import lax
from jax.experimental import pallas as pl
from jax.experimental.pallas import tpu as pltpu

F32 = jnp.float32
BF16 = jnp.bfloat16

D_MODEL = 2048
HEAD_DIM = 64
D_FF = 5632
N_HEADS = 8
N_KV = 2
GROUP = N_HEADS // N_KV
D_ATT = N_HEADS * HEAD_DIM
D_KV = N_KV * HEAD_DIM
SWA_WINDOW = 128
NSA_WINDOW = 512
ATTN_BLOCK = 128
CMP_LEN = 32
CMP_STRIDE = 16
CMP_HIDDEN = 256
SEL_BLOCK = 64
TOP_N = 16
RNN_WIDTH = 1024
RNN_BLOCK_WIDTH = 64
CONV_WIDTH = 4
LRU_C = 8.0
RMS_EPS = 1e-6
NEG_INF = -1e30
FORCED_SCORE = 1e4
SCALE = HEAD_DIM ** -0.5

LANES = 128
SUBLANES = 8
MXU_DIM = 256
VMEM_BYTES = 64 * 1024 * 1024
VMEM_LIMIT = VMEM_BYTES - 8 * 1024 * 1024
VMEM_LIMIT_BIG = VMEM_BYTES - 4 * 1024 * 1024
DOWN_CHUNK = 512

COL_XR = 0
COL_XG = 1024
COL_QA = 2048
COL_QB = 2560
COL_KA = 3072
COL_VA = 3200
COL_KC = 3328
COL_VC = 3456
COL_KS = 3584
COL_VS = 3712
COL_KW = 3840
COL_VW = 3968
COL_GL = 4096
D_PROJ = 4224

_REF_SPLITS = (D_ATT, D_KV, D_KV, D_ATT, D_KV, D_KV, D_KV, D_KV, D_KV, D_KV, 3 * N_HEADS,
               RNN_WIDTH, RNN_WIDTH)


def _params(sem, vmem_limit=VMEM_LIMIT):
    return pltpu.CompilerParams(dimension_semantics=sem, vmem_limit_bytes=vmem_limit)


def _rms(x, g):
    return x * lax.rsqrt(jnp.mean(x * x, axis=-1, keepdims=True) + RMS_EPS) * g


def _cast_kernel(w_ref, o_ref):
    o_ref[...] = w_ref[...].astype(BF16)


def _to_bf16(w, *, rows, layer=None):
    dep, k, n = w.shape
    first, count = (0, dep) if layer is None else (layer, 1)
    return pl.pallas_call(
        _cast_kernel,
        grid=(count, k // rows),
        in_specs=[pl.BlockSpec((1, rows, n), lambda l, i: (first + l, i, 0))],
        out_specs=pl.BlockSpec((1, rows, n), lambda l, i: (l, i, 0)),
        out_shape=jax.ShapeDtypeStruct((count, k, n), BF16),
        compiler_params=_params(("parallel", "parallel")),
        name="cast_bf16",
    )(w)


def _w_in_kernel(w_ref, o_ref):
    wt = w_ref[0]
    offs = np.concatenate([[0], np.cumsum(_REF_SPLITS)])
    qa, ka, va, qb, kc, vc, ks, vs, kw, vw, gl, xr, xg = [
        wt[int(offs[i]):int(offs[i + 1]), :] for i in range(len(_REF_SPLITS))]
    gl = jnp.concatenate([gl, jnp.zeros((LANES - gl.shape[0], wt.shape[1]), F32)], axis=0)
    col = 0
    for piece in (xr, xg, qa, qb, ka, va, kc, vc, ks, vs, kw, vw, gl):
        o_ref[0, :, col:col + piece.shape[0]] = piece.T.astype(BF16)
        col += piece.shape[0]


def _stage_w_in(w, *, rows=256):
    dep, k, n = w.shape
    return pl.pallas_call(
        _w_in_kernel,
        grid=(dep, k // rows),
        in_specs=[pl.BlockSpec((1, n, rows), lambda l, i: (l, 0, i))],
        out_specs=pl.BlockSpec((1, rows, D_PROJ), lambda l, i: (l, i, 0)),
        out_shape=jax.ShapeDtypeStruct((dep, k, D_PROJ), BF16),
        compiler_params=_params(("parallel", "parallel")),
        name="stage_w_in",
    )(jnp.swapaxes(w, 1, 2))


def _ffn_kernel(x_ref, g_ref, wg_ref, wu_ref, wd_ref, *rest, final_norm):
    if final_norm:
        fg_ref, o_ref, h_ref = rest
    else:
        o_ref, h_ref = rest
    j = pl.program_id(1)

    @pl.when(j == 0)
    def _():
        x = x_ref[...]
        h_ref[...] = _rms(x, g_ref[...]).astype(BF16)
        o_ref[...] = x

    h = h_ref[...]
    gate = jnp.dot(h, wg_ref[...], preferred_element_type=F32)
    up = jnp.dot(h, wu_ref[...], preferred_element_type=F32)
    act = (0.5 * gate * jax.nn.sigmoid(gate) * up).astype(BF16)
    for c in range(0, o_ref.shape[1], DOWN_CHUNK):
        o_ref[:, c:c + DOWN_CHUNK] += jnp.dot(act, wd_ref[:, c:c + DOWN_CHUNK],
                                              preferred_element_type=F32)

    if final_norm:
        @pl.when(j == pl.num_programs(1) - 1)
        def _():
            o_ref[...] = _rms(o_ref[...], fg_ref[...])


def _ffn(x, g, wg, wu, wd, layer, final_g=None, *, tm=1024, tf=512):
    t, d = x.shape
    f = wg.shape[2]
    final_norm = final_g is not None
    in_specs = [
        pl.BlockSpec((tm, d), lambda i, j: (i, 0)),
        pl.BlockSpec((1, d), lambda i, j: (0, 0)),
        pl.BlockSpec((None, d, tf), lambda i, j: (layer, 0, j)),
        pl.BlockSpec((None, d, tf), lambda i, j: (layer, 0, j)),
        pl.BlockSpec((None, tf, d), lambda i, j: (layer, j, 0)),
    ]
    args = [x, g, wg, wu, wd]
    if final_norm:
        in_specs.append(pl.BlockSpec((1, d), lambda i, j: (0, 0)))
        args.append(final_g)
    return pl.pallas_call(
        functools.partial(_ffn_kernel, final_norm=final_norm),
        grid=(t // tm, f // tf),
        in_specs=in_specs,
        out_specs=pl.BlockSpec((tm, d), lambda i, j: (i, 0)),
        out_shape=jax.ShapeDtypeStruct((t, d), F32),
        scratch_shapes=[pltpu.VMEM((tm, d), BF16)],
        compiler_params=_params(("parallel", "arbitrary"), VMEM_LIMIT_BIG),
        name="ffn",
    )(*args)


def _inproj_kernel(x_ref, g_ref, w_ref, o_ref, h_ref):
    @pl.when(pl.program_id(1) == 0)
    def _():
        h_ref[...] = _rms(x_ref[...], g_ref[...]).astype(BF16)

    o_ref[...] = jnp.dot(h_ref[...], w_ref[...], preferred_element_type=F32)


def _inproj(x, g, w, layer, *, tm=1024, tn=1408):
    t, d = x.shape
    n = w.shape[2]
    return pl.pallas_call(
        _inproj_kernel,
        grid=(t // tm, n // tn),
        in_specs=[
            pl.BlockSpec((tm, d), lambda i, j: (i, 0)),
            pl.BlockSpec((1, d), lambda i, j: (0, 0)),
            pl.BlockSpec((None, d, tn), lambda i, j: (layer, 0, j)),
        ],
        out_specs=pl.BlockSpec((tm, tn), lambda i, j: (i, j)),
        out_shape=jax.ShapeDtypeStruct((t, n), F32),
        scratch_shapes=[pltpu.VMEM((tm, d), BF16)],
        compiler_params=_params(("parallel", "arbitrary")),
        name="inproj",
    )(x, g, w)


def _queries_t(q_ref, tok):
    return q_ref[0, tok, :].T * SCALE


def _group_t(qt, kv, extra=()):
    cols = []
    for g in range(GROUP):
        h = kv * GROUP + g
        parts = [qt[h * HEAD_DIM:(h + 1) * HEAD_DIM, :], *extra]
        cols.append(parts[0] if len(parts) == 1 else jnp.concatenate(parts, axis=0))
    return jnp.concatenate(cols, axis=1).astype(BF16)


def _store_heads(o_ref, tok, outs_t, gates_t, gate_row, tq):
    pieces = []
    for kv in range(N_KV):
        for g in range(GROUP):
            h = kv * GROUP + g
            piece = outs_t[kv][:, g * tq:(g + 1) * tq]
            if gates_t is not None:
                piece = piece * gates_t[gate_row + h:gate_row + h + 1, :]
            pieces.append(piece)
    o_ref[0, tok, :] = jnp.concatenate(pieces, axis=0).T


def _gates_t(gl_ref, tok):
    return jax.nn.sigmoid(gl_ref[0, tok, :].T)


def _window_kernel(*refs, n_prev, q_blocks, use_sink, gate_row):
    refs = list(refs)
    q_ref, k_ref, v_ref = refs[:3]
    rest = refs[3:]
    sink_ref = rest.pop(0) if use_sink else None
    gl_ref = rest.pop(0) if gate_row is not None else None
    o_ref, kb_ref, vt_ref = rest

    blk = ATTN_BLOCK
    cols = GROUP * blk
    span = (n_prev + 1) * blk
    pad = n_prev * blk
    s_len = k_ref.shape[1]
    n = pl.program_id(1)

    @pl.when(n == 0)
    def _():
        vt = v_ref[0].T
        lane = lax.broadcasted_iota(jnp.int32, (pad, LANES), 1)
        pad_keys = jnp.where(lane == HEAD_DIM, NEG_INF, 0.0).astype(BF16)
        for kv in range(N_KV):
            k = k_ref[0, :, kv * HEAD_DIM:(kv + 1) * HEAD_DIM]
            kb_ref[kv, 0:pad, :] = pad_keys
            kb_ref[kv, pad:pad + s_len, :] = jnp.concatenate(
                [k, jnp.zeros((s_len, LANES - HEAD_DIM), F32)], axis=1).astype(BF16)
            vt_ref[kv, :, 0:pad] = jnp.zeros((HEAD_DIM, pad), BF16)
            vt_ref[kv, :, pad:pad + s_len] = vt[kv * HEAD_DIM:(kv + 1) * HEAD_DIM, :].astype(BF16)

    krow = lax.broadcasted_iota(jnp.int32, (blk, cols), 0)
    qcol = lax.broadcasted_iota(jnp.int32, (blk, cols), 1) & (blk - 1)
    in_window = krow > qcol
    causal = krow <= qcol
    one_row = lax.broadcasted_iota(jnp.int32, (LANES - HEAD_DIM, blk), 0) == 0
    pad_hit = jnp.where(one_row, 1.0, 0.0)
    starts, scores = [], []
    for j in range(q_blocks):
        start = pl.multiple_of((n * q_blocks + j) * blk, blk)
        qt = _queries_t(q_ref, slice(j * blk, (j + 1) * blk))
        starts.append(start)
        for kv in range(N_KV):
            s = jnp.dot(kb_ref[kv, pl.ds(start, span), :], _group_t(qt, kv, extra=(pad_hit,)),
                        preferred_element_type=F32)
            bands = [jnp.where(in_window, s[0:blk], NEG_INF)]
            if n_prev > 1:
                bands.append(s[blk:span - blk])
            bands.append(jnp.where(causal, s[span - blk:span], NEG_INF))
            scores.append(jnp.concatenate(bands, axis=0))
    probs = []
    for c, s in enumerate(scores):
        kv = c % N_KV
        m = jnp.max(s, axis=0, keepdims=True)
        if use_sink:
            sk = jnp.concatenate([jnp.full((1, blk), sink_ref[kv * GROUP + g], F32)
                                  for g in range(GROUP)], axis=1)
            m = jnp.maximum(m, sk)
            p = jnp.exp(s - m)
            denom = jnp.sum(p, axis=0, keepdims=True) + jnp.exp(sk - m)
        else:
            p = jnp.exp(s - m)
            denom = jnp.sum(p, axis=0, keepdims=True)
        probs.append((p.astype(BF16), denom))
    for j in range(q_blocks):
        tok = slice(j * blk, (j + 1) * blk)
        outs = []
        for kv in range(N_KV):
            p, denom = probs[j * N_KV + kv]
            o = jnp.dot(vt_ref[kv, :, pl.ds(starts[j], span)], p,
                        preferred_element_type=F32)
            outs.append(o * (1.0 / denom))
        gates = _gates_t(gl_ref, tok) if gate_row is not None else None
        _store_heads(o_ref, tok, outs, gates, gate_row, blk)


def _window_attention(proj, q_col, k_col, v_col, window, sinks=None, gate_row=None, *,
                      q_blocks=4):
    b, s, _ = proj.shape
    n_prev = window // ATTN_BLOCK
    assert window == n_prev * ATTN_BLOCK, "band masks assume a window of whole blocks"
    pad = n_prev * ATTN_BLOCK
    blk = q_blocks * ATTN_BLOCK
    in_specs = [
        pl.BlockSpec((1, blk, D_ATT), lambda i, n: (i, n, q_col // D_ATT)),
        pl.BlockSpec((1, s, D_KV), lambda i, n: (i, 0, k_col // D_KV)),
        pl.BlockSpec((1, s, D_KV), lambda i, n: (i, 0, v_col // D_KV)),
    ]
    args = [proj, proj, proj]
    if sinks is not None:
        in_specs.append(pl.BlockSpec(memory_space=pltpu.SMEM))
        args.append(sinks)
    if gate_row is not None:
        in_specs.append(pl.BlockSpec((1, blk, LANES), lambda i, n: (i, n, COL_GL // LANES)))
        args.append(proj)
    return pl.pallas_call(
        functools.partial(_window_kernel, n_prev=n_prev, q_blocks=q_blocks,
                          use_sink=sinks is not None, gate_row=gate_row),
        grid=(b, s // blk),
        in_specs=in_specs,
        out_specs=pl.BlockSpec((1, blk, D_ATT), lambda i, n: (i, n, 0)),
        out_shape=jax.ShapeDtypeStruct((b, s, D_ATT), F32),
        scratch_shapes=[pltpu.VMEM((N_KV, pad + s, LANES), BF16),
                        pltpu.VMEM((N_KV, HEAD_DIM, pad + s), BF16)],
        compiler_params=_params(("parallel", "arbitrary")),
        name=f"window{window}",
    )(*args)


def _compress_kernel(k_ref, v_ref, pos_ref, w1_ref, b1_ref, w2_ref, kc_ref, vc_ref):
    n_rows = kc_ref.shape[1]
    for i, (src_ref, dst_ref) in enumerate(((k_ref, kc_ref), (v_ref, vc_ref))):
        lo = jnp.zeros((n_rows, N_KV * CMP_HIDDEN), F32)
        hi = jnp.zeros((n_rows, N_KV * CMP_HIDDEN), F32)
        for l in range(CMP_STRIDE):
            x = src_ref[0, pl.ds(l, n_rows, stride=CMP_STRIDE), :]
            lo = lo + jnp.dot((x + pos_ref[i, l:l + 1, :]).astype(BF16), w1_ref[i, l],
                              preferred_element_type=F32)
            hi = hi + jnp.dot((x + pos_ref[i, CMP_STRIDE + l:CMP_STRIDE + l + 1, :]).astype(BF16),
                              w1_ref[i, CMP_STRIDE + l], preferred_element_type=F32)
        hid = lo + pltpu.roll(hi, n_rows - 1, 0) + b1_ref[i]
        act = jax.nn.gelu(hid, approximate=True).astype(BF16)
        dst_ref[0] = jnp.dot(act, w2_ref[i], preferred_element_type=F32)


def _compress(proj, pos, w1, b1, w2):
    b, s, _ = proj.shape
    n_rows = s // CMP_STRIDE
    out_spec = pl.BlockSpec((1, n_rows, D_KV), lambda i: (i, 0, 0))
    out_shape = jax.ShapeDtypeStruct((b, n_rows, D_KV), F32)
    return pl.pallas_call(
        _compress_kernel,
        grid=(b,),
        in_specs=[
            pl.BlockSpec((1, s, D_KV), lambda i: (i, 0, COL_KC // D_KV)),
            pl.BlockSpec((1, s, D_KV), lambda i: (i, 0, COL_VC // D_KV)),
            pl.BlockSpec(pos.shape, lambda i: (0, 0, 0)),
            pl.BlockSpec(w1.shape, lambda i: (0, 0, 0, 0)),
            pl.BlockSpec(b1.shape, lambda i: (0, 0, 0)),
            pl.BlockSpec(w2.shape, lambda i: (0, 0, 0)),
        ],
        out_specs=[out_spec, out_spec],
        out_shape=[out_shape, out_shape],
        compiler_params=_params(("parallel",)),
        name="compress",
    )(proj, proj, pos, w1, b1, w2)


def _cmp_attn_kernel(q_ref, kc_ref, vc_ref, gl_ref, ov_ref, o_ref, bias_ref, *, tq, n_sel):
    n = pl.program_id(1)
    ncp = kc_ref.shape[1]
    cols = GROUP * tq
    pos_c = n * tq + (lax.broadcasted_iota(jnp.int32, (ncp, cols), 1) & (tq - 1))
    cend = lax.broadcasted_iota(jnp.int32, (ncp, cols), 0) * CMP_STRIDE + (CMP_LEN - 1)
    cmask = cend <= pos_c

    pos_s = n * tq + lax.broadcasted_iota(jnp.int32, (n_sel, tq), 1)
    bid = lax.broadcasted_iota(jnp.int32, (n_sel, tq), 0)
    cur = pos_s // SEL_BLOCK
    forced = (bid == 0) | (bid == cur) | (bid == cur - 1)
    valid = bid * SEL_BLOCK <= pos_s

    tok = slice(None)
    qt = _queries_t(q_ref, tok)
    vct = vc_ref[0].T
    outs = []
    for kv in range(N_KV):
        kc = kc_ref[0, :, kv * HEAD_DIM:(kv + 1) * HEAD_DIM].astype(BF16)
        s = jnp.dot(kc, _group_t(qt, kv), preferred_element_type=F32)
        s = jnp.where(cmask, s, NEG_INF)
        m = jnp.max(s, axis=0, keepdims=True)
        p = jnp.exp(s - m)
        p = p * (1.0 / jnp.sum(p, axis=0, keepdims=True))
        p = jnp.where(cmask, p, 0.0)
        outs.append(jnp.dot(vct[kv * HEAD_DIM:(kv + 1) * HEAD_DIM, :].astype(BF16),
                            p.astype(BF16), preferred_element_type=F32))
        psum = p[:, 0:tq]
        for g in range(1, GROUP):
            psum = psum + p[:, g * tq:(g + 1) * tq]

        imp = jnp.dot(ov_ref[...], psum, preferred_element_type=F32,
                      precision=lax.Precision.HIGHEST)
        imp = jnp.where(forced, FORCED_SCORE, imp)
        imp = jnp.where(valid, imp, NEG_INF)
        rank = jnp.zeros((n_sel, tq), jnp.int32)
        for i in range(n_sel):
            row = imp[i:i + 1, :]
            ahead = (row > imp) | ((row == imp) & (bid > i))
            rank = rank + ahead.astype(jnp.int32)
        bias_ref[0, kv * n_sel:(kv + 1) * n_sel, :] = jnp.where(rank < TOP_N, 0.0, NEG_INF)
    _store_heads(o_ref, tok, outs, _gates_t(gl_ref, tok), 0, tq)


def _cmp_attention(proj, kc, vc, overlap, *, tq=256):
    b, s, _ = proj.shape
    n_sel = overlap.shape[0]
    ncp = kc.shape[1]
    return pl.pallas_call(
        functools.partial(_cmp_attn_kernel, tq=tq, n_sel=n_sel),
        grid=(b, s // tq),
        in_specs=[
            pl.BlockSpec((1, tq, D_ATT), lambda i, n: (i, n, COL_QB // D_ATT)),
            pl.BlockSpec((1, ncp, D_KV), lambda i, n: (i, 0, 0)),
            pl.BlockSpec((1, ncp, D_KV), lambda i, n: (i, 0, 0)),
            pl.BlockSpec((1, tq, LANES), lambda i, n: (i, n, COL_GL // LANES)),
            pl.BlockSpec(overlap.shape, lambda i, n: (0, 0)),
        ],
        out_specs=[
            pl.BlockSpec((1, tq, D_ATT), lambda i, n: (i, n, 0)),
            pl.BlockSpec((1, N_KV * n_sel, tq), lambda i, n: (i, 0, n)),
        ],
        out_shape=[
            jax.ShapeDtypeStruct((b, s, D_ATT), F32),
            jax.ShapeDtypeStruct((b, N_KV * n_sel, s), F32),
        ],
        compiler_params=_params(("parallel", "arbitrary")),
        name="cmp_attn",
    )(proj, kc, vc, proj, overlap)


def _sel_attn_kernel(q_ref, k_ref, v_ref, bias_ref, gl_ref, *rest,
                     tq, tk, chains, n_sel, gate_row, n_cast):
    cast_src = rest[:n_cast]
    o_ref = rest[n_cast]
    cast_dst = rest[n_cast + 1:2 * n_cast + 1]
    qa_ref, ka_ref, vt_ref = rest[2 * n_cast + 1:]
    for src, dst in zip(cast_src, cast_dst):
        dst[...] = src[...].astype(BF16)

    n = pl.program_id(1)
    s_len = k_ref.shape[1]
    cols = GROUP * tq
    n_pad = LANES - HEAD_DIM - n_sel

    @pl.when(n == 0)
    def _():
        kpos = lax.broadcasted_iota(jnp.int32, (s_len, n_sel), 0)
        blk = lax.broadcasted_iota(jnp.int32, (s_len, n_sel), 1)
        onehot = (kpos // SEL_BLOCK == blk).astype(F32)
        zeros = jnp.zeros((s_len, n_pad), F32)
        vt = v_ref[0].T
        for kv in range(N_KV):
            k = k_ref[0, :, kv * HEAD_DIM:(kv + 1) * HEAD_DIM]
            ka_ref[kv] = jnp.concatenate([k, onehot, zeros], axis=1).astype(BF16)
            for t in range(s_len // tk):
                vt_ref[kv, t] = vt[kv * HEAD_DIM:(kv + 1) * HEAD_DIM,
                                   t * tk:(t + 1) * tk].astype(BF16)

    tok = slice(None)
    qt = _queries_t(q_ref, tok)
    zq = jnp.zeros((n_pad, tq), F32)
    for kv in range(N_KV):
        bias = bias_ref[0, kv * n_sel:(kv + 1) * n_sel, :]
        qa_ref[kv] = _group_t(qt, kv, extra=(bias, zq))
    ccols = cols // chains
    qpos = n * tq + (lax.broadcasted_iota(jnp.int32, (tk, ccols), 1) & (tq - 1))
    krow = lax.broadcasted_iota(jnp.int32, (tk, ccols), 0)

    def step(kt, state, causal):
        k0 = pl.multiple_of(kt * tk, tk)
        scores = []
        for c in range(len(state)):
            kv, part = divmod(c, chains)
            cs = slice(part * ccols, (part + 1) * ccols)
            s = jnp.dot(ka_ref[kv, pl.ds(k0, tk), :], qa_ref[kv, :, cs],
                        preferred_element_type=F32)
            if causal:
                s = jnp.where(k0 + krow <= qpos, s, NEG_INF)
            scores.append(s)
        probs = []
        for s, (m_old, l_old, _) in zip(scores, state):
            m_new = jnp.maximum(m_old, jnp.max(s, axis=0, keepdims=True))
            alpha = jnp.exp(m_old - m_new)
            p = jnp.exp(s - m_new)
            l_new = alpha * l_old + jnp.sum(p, axis=0, keepdims=True)
            probs.append((m_new, l_new, alpha, p.astype(BF16)))
        new_state = []
        for c, ((m_new, l_new, alpha, p), (_, _, acc_old)) in enumerate(zip(probs, state)):
            kv = c // chains
            acc_new = alpha * acc_old + jnp.dot(vt_ref[kv, kt], p, preferred_element_type=F32)
            new_state.append((m_new, l_new, acc_new))
        return tuple(new_state)

    init = tuple((jnp.full((1, ccols), NEG_INF, F32), jnp.zeros((1, ccols), F32),
                  jnp.zeros((HEAD_DIM, ccols), F32)) for _ in range(N_KV * chains))
    n_full = (n * tq) // tk
    state = lax.fori_loop(0, n_full, functools.partial(step, causal=False), init)
    state = step(n_full, state, causal=True)

    outs = []
    for kv in range(N_KV):
        parts = [acc * (1.0 / l) for _, l, acc in state[kv * chains:(kv + 1) * chains]]
        outs.append(jnp.concatenate(parts, axis=1))
    _store_heads(o_ref, tok, outs, _gates_t(gl_ref, tok), gate_row, tq)


def _sel_attention(proj, bias, cast_jobs=(), *, tq=512, tk=512, chains=4):
    b, s, _ = proj.shape
    n_sel = bias.shape[1] // N_KV
    cols = GROUP * tq
    nq = s // tq
    steps = b * nq
    cast_in, cast_out, cast_shapes = [], [], []
    for w, layer in cast_jobs:
        _, k, nn = w.shape
        rows = k // steps
        cast_in.append(pl.BlockSpec((None, rows, nn),
                                    lambda i, n, layer=layer: (layer, i * nq + n, 0)))
        cast_out.append(pl.BlockSpec((None, rows, nn), lambda i, n: (0, i * nq + n, 0)))
        cast_shapes.append(jax.ShapeDtypeStruct((1, k, nn), BF16))
    outs = pl.pallas_call(
        functools.partial(_sel_attn_kernel, tq=tq, tk=tk, chains=chains, n_sel=n_sel,
                          gate_row=N_HEADS, n_cast=len(cast_jobs)),
        grid=(b, nq),
        in_specs=[
            pl.BlockSpec((1, tq, D_ATT), lambda i, n: (i, n, COL_QB // D_ATT)),
            pl.BlockSpec((1, s, D_KV), lambda i, n: (i, 0, COL_KS // D_KV)),
            pl.BlockSpec((1, s, D_KV), lambda i, n: (i, 0, COL_VS // D_KV)),
            pl.BlockSpec((1, N_KV * n_sel, tq), lambda i, n: (i, 0, n)),
            pl.BlockSpec((1, tq, LANES), lambda i, n: (i, n, COL_GL // LANES)),
            *cast_in,
        ],
        out_specs=[pl.BlockSpec((1, tq, D_ATT), lambda i, n: (i, n, 0)), *cast_out],
        out_shape=[jax.ShapeDtypeStruct((b, s, D_ATT), F32), *cast_shapes],
        scratch_shapes=[
            pltpu.VMEM((N_KV, LANES, cols), BF16),
            pltpu.VMEM((N_KV, s, LANES), BF16),
            pltpu.VMEM((N_KV, s // tk, HEAD_DIM, tk), BF16),
        ],
        compiler_params=_params(("parallel", "arbitrary"), VMEM_LIMIT_BIG),
        name="sel_attn",
    )(proj, proj, proj, bias, proj, *[w for w, _ in cast_jobs])
    return outs[0], list(outs[1:])


def _rglru_kernel(xr_ref, xg_ref, cw_ref, cb_ref, wa_ref, ba_ref, wx_ref, bx_ref, lam_ref,
                  o_ref, xbuf_ref, h_ref, *, ts):
    pad = SUBLANES

    @pl.when(pl.program_id(1) == 0)
    def _():
        xbuf_ref[0:pad, :] = jnp.zeros((pad, RNN_WIDTH), F32)
        h_ref[...] = jnp.zeros_like(h_ref)

    xr = xr_ref[0]
    xbuf_ref[pad:pad + ts, :] = xr
    xc = cb_ref[...] + cw_ref[CONV_WIDTH - 1:CONV_WIDTH, :] * xr
    for w in range(CONV_WIDTH - 1):
        shift = CONV_WIDTH - 1 - w
        xc = xc + cw_ref[w:w + 1, :] * xbuf_ref[pl.ds(pad - shift, ts), :]
    xbuf_ref[0:pad, :] = xr[ts - pad:ts, :]

    ra, rx = [], []
    for blk in range(RNN_WIDTH // MXU_DIM):
        xs = xc[:, blk * MXU_DIM:(blk + 1) * MXU_DIM].astype(BF16)
        ra.append(jnp.dot(xs, wa_ref[blk], preferred_element_type=F32))
        rx.append(jnp.dot(xs, wx_ref[blk], preferred_element_type=F32))
    r = 0.5 * (jnp.tanh(0.5 * (jnp.concatenate(ra, axis=1) + ba_ref[...])) + 1.0)
    gi = 0.5 * (jnp.tanh(0.5 * (jnp.concatenate(rx, axis=1) + bx_ref[...])) + 1.0)
    nl = -lam_ref[...]
    softplus = jnp.maximum(nl, 0.0) + jnp.log1p(jnp.exp(-jnp.abs(nl)))
    log_a = -LRU_C * r * softplus
    a = jnp.exp(log_a)
    u = jnp.sqrt(-jnp.tanh(log_a) * (a * a + 1.0)) * (gi * xc)

    n_slab = ts // SUBLANES
    a = a.reshape(n_slab, SUBLANES, RNN_WIDTH)
    u = u.reshape(n_slab, SUBLANES, RNN_WIDTH)
    sub = lax.broadcasted_iota(jnp.int32, a.shape, 1)
    d = 1
    while d < SUBLANES:
        keep = sub >= d
        a_prev = jnp.where(keep, pltpu.roll(a, d, 1), 1.0)
        u_prev = jnp.where(keep, pltpu.roll(u, d, 1), 0.0)
        u = a * u_prev + u
        a = a * a_prev
        d *= 2
    h_last = h_ref[0:1, :]
    slabs = []
    for r in range(n_slab):
        h = a[r] * h_last + u[r]
        slabs.append(h)
        h_last = h[SUBLANES - 1:SUBLANES, :]
    h_ref[0:1, :] = h_last
    o_ref[0] = jax.nn.gelu(xg_ref[0], approximate=True) * jnp.concatenate(slabs, axis=0)


def _rglru(proj, conv_w, conv_b, wa, ba, wx, bx, lam, *, ts=256):
    b, s, _ = proj.shape
    c = RNN_WIDTH
    vec = pl.BlockSpec((1, c), lambda i, n: (0, 0))
    wspec = pl.BlockSpec(wa.shape, lambda i, n: (0, 0, 0))
    return pl.pallas_call(
        functools.partial(_rglru_kernel, ts=ts),
        grid=(b, s // ts),
        in_specs=[
            pl.BlockSpec((1, ts, c), lambda i, n: (i, n, COL_XR // c)),
            pl.BlockSpec((1, ts, c), lambda i, n: (i, n, COL_XG // c)),
            pl.BlockSpec((CONV_WIDTH, c), lambda i, n: (0, 0)),
            vec, wspec, vec, wspec, vec, vec,
        ],
        out_specs=pl.BlockSpec((1, ts, c), lambda i, n: (i, n, 0)),
        out_shape=jax.ShapeDtypeStruct((b, s, c), F32),
        scratch_shapes=[pltpu.VMEM((ts + SUBLANES, c), F32), pltpu.VMEM((SUBLANES, c), F32)],
        compiler_params=_params(("parallel", "arbitrary")),
        name="rglru",
    )(proj, proj, conv_w, conv_b, wa, ba, wx, bx, lam)


def _outproj_kernel(x_ref, ya_ref, oc_ref, os_ref, ow_ref, yc_ref, gn_ref, w_ref, o_ref):
    ya = _rms(ya_ref[...], gn_ref[:, 0:D_ATT]).astype(BF16)
    yb = oc_ref[...] + os_ref[...] + ow_ref[...]
    yb = _rms(yb, gn_ref[:, D_ATT:2 * D_ATT]).astype(BF16)
    yc = _rms(yc_ref[...], gn_ref[:, 2 * D_ATT:]).astype(BF16)
    y = jnp.dot(ya, w_ref[0:D_ATT, :], preferred_element_type=F32)
    y = y + jnp.dot(yb, w_ref[D_ATT:2 * D_ATT, :], preferred_element_type=F32)
    y = y + jnp.dot(yc, w_ref[2 * D_ATT:, :], preferred_element_type=F32)
    o_ref[...] = x_ref[...] + y


def _outproj(x, ya, oc, osl, ow, yc, gn, w, layer, *, tm=256):
    t, d = x.shape
    att = pl.BlockSpec((tm, D_ATT), lambda i: (i, 0))
    return pl.pallas_call(
        _outproj_kernel,
        grid=(t // tm,),
        in_specs=[
            pl.BlockSpec((tm, d), lambda i: (i, 0)),
            att, att, att, att,
            pl.BlockSpec((tm, RNN_WIDTH), lambda i: (i, 0)),
            pl.BlockSpec((1, gn.shape[1]), lambda i: (0, 0)),
            pl.BlockSpec((None,) + w.shape[1:], lambda i: (layer, 0, 0)),
        ],
        out_specs=pl.BlockSpec((tm, d), lambda i: (i, 0)),
        out_shape=jax.ShapeDtypeStruct((t, d), F32),
        compiler_params=_params(("parallel",)),
        name="outproj",
    )(x, ya, oc, osl, ow, yc, gn, w)


def _block_diag(w, per):
    *lead, n, r, c = w.shape
    w = w.reshape(*lead, n // per, per, r, c)
    eye = jnp.eye(per, dtype=w.dtype)
    out = jnp.einsum('...prc,pq->...prqc', w, eye)
    return out.reshape(*lead, n // per, per * r, per * c)


def _compress_weights(cmp_pos, w1, b1, w2):
    pos = jnp.tile(cmp_pos, (1, 1, N_KV))
    w1 = w1.reshape(2, CMP_LEN, 1, HEAD_DIM, CMP_HIDDEN)
    w1 = _block_diag(jnp.broadcast_to(w1, (2, CMP_LEN, N_KV, HEAD_DIM, CMP_HIDDEN)), N_KV)
    w1 = w1.reshape(2, CMP_LEN, N_KV * HEAD_DIM, N_KV * CMP_HIDDEN)
    w2 = _block_diag(jnp.broadcast_to(w2[:, None], (2, N_KV, CMP_HIDDEN, HEAD_DIM)), N_KV)
    w2 = w2.reshape(2, N_KV * CMP_HIDDEN, N_KV * HEAD_DIM)
    b1 = jnp.tile(b1[:, None, :], (1, 1, N_KV))
    return pos, w1.astype(BF16), b1, w2.astype(BF16)


def _overlap_matrix(seq, n_cmp_padded):
    n_c = (seq - CMP_LEN) // CMP_STRIDE + 1
    n_sel = seq // SEL_BLOCK
    cs = np.arange(n_c)[:, None] * CMP_STRIDE
    ss = np.arange(n_sel)[None, :] * SEL_BLOCK
    ov = np.clip(np.minimum(cs + CMP_LEN, ss + SEL_BLOCK) - np.maximum(cs, ss), 0, None)
    out = np.zeros((n_cmp_padded, n_sel), np.float32)
    out[:n_c] = ov / CMP_LEN
    return jnp.asarray(out.T)


def kernel(x, ffn1_norm, ffn1_w_gate, ffn1_w_up, ffn1_w_down, mix_norm, w_in, swa_sinks, cmp_pos, cmp_w1, cmp_b1, cmp_w2, conv_w, conv_b, lru_wa, lru_ba, lru_wx, lru_bx, lru_lambda, group_norm, w_out, ffn2_norm, ffn2_w_gate, ffn2_w_up, ffn2_w_down, final_norm):
    b, s, d = x.shape
    depth = w_in.shape[0]
    t = b * s
    overlap = _overlap_matrix(s, s // CMP_STRIDE)
    gate_tile = MXU_DIM // RNN_BLOCK_WIDTH
    xt = x.reshape(t, d)
    ffn1_f32 = (ffn1_w_gate, ffn1_w_up, ffn1_w_down)
    ffn2_f32 = (ffn2_w_gate, ffn2_w_up, ffn2_w_down)
    ffn1_w = [_to_bf16(ffn1_w_gate, rows=256, layer=0), _to_bf16(ffn1_w_up, rows=256, layer=0),
              _to_bf16(ffn1_w_down, rows=512, layer=0)]
    w_in_b = _stage_w_in(w_in)
    w_out_b = _to_bf16(w_out, rows=512)
    for l in range(depth):
        xt = _ffn(xt, ffn1_norm[l][None], *ffn1_w, 0)

        proj = _inproj(xt, mix_norm[l][None], w_in_b, l)
        proj = proj.reshape(b, s, D_PROJ)

        ya = _window_attention(proj, COL_QA, COL_KA, COL_VA, SWA_WINDOW, sinks=swa_sinks[l])

        kc, vc = _compress(proj, *_compress_weights(cmp_pos[l], cmp_w1[l], cmp_b1[l], cmp_w2[l]))
        o_cmp, bias = _cmp_attention(proj, kc, vc, overlap)
        jobs = [(w, l) for w in ffn2_f32]
        if l + 1 < depth:
            jobs += [(w, l + 1) for w in ffn1_f32]
        o_slc, staged = _sel_attention(proj, bias, jobs)
        ffn2_w, ffn1_w = staged[:3], staged[3:]
        o_win = _window_attention(proj, COL_QB, COL_KW, COL_VW, NSA_WINDOW,
                                  gate_row=2 * N_HEADS, q_blocks=2)

        yc = _rglru(proj, conv_w[l], conv_b[l][None],
                    _block_diag(lru_wa[l], gate_tile).astype(BF16), lru_ba[l][None],
                    _block_diag(lru_wx[l], gate_tile).astype(BF16), lru_bx[l][None],
                    lru_lambda[l][None])

        xt = _outproj(xt, ya.reshape(t, D_ATT), o_cmp.reshape(t, D_ATT),
                      o_slc.reshape(t, D_ATT), o_win.reshape(t, D_ATT),
                      yc.reshape(t, RNN_WIDTH), group_norm[l][None], w_out_b, l)

        last = l == depth - 1
        xt = _ffn(xt, ffn2_norm[l][None], *ffn2_w, 0,
                  final_g=final_norm[None] if last else None)
    return xt.reshape(b, s, d)
```

```python
import functools

import jax
import jax.numpy as jnp
import numpy as np
from jax import lax
from jax.experimental import pallas as pl
from jax.experimental.pallas import tpu as pltpu

F32 = jnp.float32
BF16 = jnp.bfloat16

HEAD_DIM = 64
N_HEADS = 8
N_KV = 2
GROUP = N_HEADS // N_KV
D_ATT = N_HEADS * HEAD_DIM
D_KV = N_KV * HEAD_DIM
SWA_WINDOW = 128
NSA_WINDOW = 512
ATTN_BLOCK = 128
CMP_LEN = 32
CMP_STRIDE = 16
CMP_HIDDEN = 256
SEL_BLOCK = 64
TOP_N = 16
RNN_WIDTH = 1024
RNN_BLOCK_WIDTH = 64
CONV_WIDTH = 4
LRU_C = 8.0
RMS_EPS = 1e-6
NEG_INF = -1e30
FORCED_SCORE = 1e4
SCALE = HEAD_DIM ** -0.5
LOG2E = 1.4426950408889634
SCALE_LOG2 = SCALE * LOG2E

LANES = 128
SUBLANES = 8
MXU_DIM = 256
VMEM_BYTES = 64 * 1024 * 1024
VMEM_LIMIT = VMEM_BYTES - 8 * 1024 * 1024
VMEM_LIMIT_BIG = VMEM_BYTES - 4 * 1024 * 1024
DOWN_CHUNK = 512
SCORE_LEAD = 2

COL_XR = 0
COL_XG = 1024
COL_QA = 2048
COL_QB = 2560
COL_KA = 3072
COL_VA = 3200
COL_KC = 3328
COL_VC = 3456
COL_KS = 3584
COL_VS = 3712
COL_KW = 3840
COL_VW = 3968
COL_GL = 4096
D_PROJ = 4224

_REF_SPLITS = (D_ATT, D_KV, D_KV, D_ATT, D_KV, D_KV, D_KV, D_KV, D_KV, D_KV, 3 * N_HEADS,
               RNN_WIDTH, RNN_WIDTH)


def _params(sem, vmem_limit=VMEM_LIMIT):
    return pltpu.CompilerParams(dimension_semantics=sem, vmem_limit_bytes=vmem_limit)


def _rms(x, g):
    return x * lax.rsqrt(jnp.mean(x * x, axis=-1, keepdims=True) + RMS_EPS) * g


def _cast_kernel(w_ref, o_ref):
    o_ref[...] = w_ref[...].astype(BF16)


def _to_bf16(w, *, rows, layer=None):
    dep, k, n = w.shape
    first, count = (0, dep) if layer is None else (layer, 1)
    return pl.pallas_call(
        _cast_kernel,
        grid=(count, k // rows),
        in_specs=[pl.BlockSpec((1, rows, n), lambda l, i: (first + l, i, 0))],
        out_specs=pl.BlockSpec((1, rows, n), lambda l, i: (l, i, 0)),
        out_shape=jax.ShapeDtypeStruct((count, k, n), BF16),
        compiler_params=_params(("parallel", "parallel")),
        name="cast_bf16",
    )(w)


def _w_in_kernel(w_ref, o_ref):
    wt = w_ref[0]
    offs = np.concatenate([[0], np.cumsum(_REF_SPLITS)])
    qa, ka, va, qb, kc, vc, ks, vs, kw, vw, gl, xr, xg = [
        wt[int(offs[i]):int(offs[i + 1]), :] for i in range(len(_REF_SPLITS))]
    gl = jnp.concatenate([gl, jnp.zeros((LANES - gl.shape[0], wt.shape[1]), F32)], axis=0)
    col = 0
    for piece in (xr, xg, qa, qb, ka, va, kc, vc, ks, vs, kw, vw, gl):
        o_ref[0, :, col:col + piece.shape[0]] = piece.T.astype(BF16)
        col += piece.shape[0]


def _stage_w_in(w, *, rows=256):
    dep, k, n = w.shape
    return pl.pallas_call(
        _w_in_kernel,
        grid=(dep, k // rows),
        in_specs=[pl.BlockSpec((1, n, rows), lambda l, i: (l, 0, i))],
        out_specs=pl.BlockSpec((1, rows, D_PROJ), lambda l, i: (l, i, 0)),
        out_shape=jax.ShapeDtypeStruct((dep, k, D_PROJ), BF16),
        compiler_params=_params(("parallel", "parallel")),
        name="stage_w_in",
    )(jnp.swapaxes(w, 1, 2))


def _ffn_kernel(x_ref, g_ref, wg_ref, wu_ref, wd_ref, *rest, final_norm):
    if final_norm:
        fg_ref, o_ref, h_ref = rest
    else:
        o_ref, h_ref = rest
    j = pl.program_id(1)

    @pl.when(j == 0)
    def _():
        x = x_ref[...]
        h_ref[...] = _rms(x, g_ref[...]).astype(BF16)
        o_ref[...] = x

    h = h_ref[...]
    gate = jnp.dot(h, wg_ref[...], preferred_element_type=F32)
    up = jnp.dot(h, wu_ref[...], preferred_element_type=F32)
    act = (0.5 * gate * jax.nn.sigmoid(gate) * up).astype(BF16)
    for c in range(0, o_ref.shape[1], DOWN_CHUNK):
        o_ref[:, c:c + DOWN_CHUNK] += jnp.dot(act, wd_ref[:, c:c + DOWN_CHUNK],
                                              preferred_element_type=F32)

    if final_norm:
        @pl.when(j == pl.num_programs(1) - 1)
        def _():
            o_ref[...] = _rms(o_ref[...], fg_ref[...])


def _ffn(x, g, wg, wu, wd, layer, final_g=None, *, tm=1024, tf=512):
    t, d = x.shape
    f = wg.shape[2]
    final_norm = final_g is not None
    in_specs = [
        pl.BlockSpec((tm, d), lambda i, j: (i, 0)),
        pl.BlockSpec((1, d), lambda i, j: (0, 0)),
        pl.BlockSpec((None, d, tf), lambda i, j: (layer, 0, j)),
        pl.BlockSpec((None, d, tf), lambda i, j: (layer, 0, j)),
        pl.BlockSpec((None, tf, d), lambda i, j: (layer, j, 0)),
    ]
    args = [x, g, wg, wu, wd]
    if final_norm:
        in_specs.append(pl.BlockSpec((1, d), lambda i, j: (0, 0)))
        args.append(final_g)
    return pl.pallas_call(
        functools.partial(_ffn_kernel, final_norm=final_norm),
        grid=(t // tm, f // tf),
        in_specs=in_specs,
        out_specs=pl.BlockSpec((tm, d), lambda i, j: (i, 0)),
        out_shape=jax.ShapeDtypeStruct((t, d), F32),
        scratch_shapes=[pltpu.VMEM((tm, d), BF16)],
        compiler_params=_params(("parallel", "arbitrary"), VMEM_LIMIT_BIG),
        name="ffn",
    )(*args)


def _inproj_kernel(x_ref, g_ref, w_ref, o_ref, h_ref):
    @pl.when(pl.program_id(1) == 0)
    def _():
        h_ref[...] = _rms(x_ref[...], g_ref[...]).astype(BF16)

    o_ref[...] = jnp.dot(h_ref[...], w_ref[...], preferred_element_type=F32)


def _inproj(x, g, w, layer, *, tm=1024, tn=1408):
    t, d = x.shape
    n = w.shape[2]
    return pl.pallas_call(
        _inproj_kernel,
        grid=(t // tm, n // tn),
        in_specs=[
            pl.BlockSpec((tm, d), lambda i, j: (i, 0)),
            pl.BlockSpec((1, d), lambda i, j: (0, 0)),
            pl.BlockSpec((None, d, tn), lambda i, j: (layer, 0, j)),
        ],
        out_specs=pl.BlockSpec((tm, tn), lambda i, j: (i, j)),
        out_shape=jax.ShapeDtypeStruct((t, n), F32),
        scratch_shapes=[pltpu.VMEM((tm, d), BF16)],
        compiler_params=_params(("parallel", "arbitrary")),
        name="inproj",
    )(x, g, w)


def _queries_t(q_ref, tok):
    return q_ref[0, tok, :].T * SCALE_LOG2


def _group_t(qt, kv, extra=()):
    cols = []
    for g in range(GROUP):
        h = kv * GROUP + g
        parts = [qt[h * HEAD_DIM:(h + 1) * HEAD_DIM, :], *extra]
        cols.append(parts[0] if len(parts) == 1 else jnp.concatenate(parts, axis=0))
    return jnp.concatenate(cols, axis=1).astype(BF16)


def _store_heads(o_ref, tok, outs_t, gates_t, gate_row, tq):
    pieces = []
    for kv in range(N_KV):
        for g in range(GROUP):
            h = kv * GROUP + g
            piece = outs_t[kv][:, g * tq:(g + 1) * tq]
            if gates_t is not None:
                piece = piece * gates_t[gate_row + h:gate_row + h + 1, :]
            pieces.append(piece)
    o_ref[0, tok, :] = jnp.concatenate(pieces, axis=0).T


def _gates_t(gl_ref, tok):
    return jax.nn.sigmoid(gl_ref[0, tok, :].T)


def _window_kernel(*refs, n_prev, q_blocks, use_sink, gate_row):
    refs = list(refs)
    q_ref, k_ref, v_ref = refs[:3]
    rest = refs[3:]
    sink_ref = rest.pop(0) if use_sink else None
    gl_ref = rest.pop(0) if gate_row is not None else None
    o_ref, kb_ref, vt_ref = rest

    blk = ATTN_BLOCK
    cols = GROUP * blk
    span = (n_prev + 1) * blk
    pad = n_prev * blk
    s_len = k_ref.shape[1]
    n = pl.program_id(1)

    @pl.when(n == 0)
    def _():
        vt = v_ref[0].T
        lane = lax.broadcasted_iota(jnp.int32, (pad, LANES), 1)
        pad_keys = jnp.where(lane == HEAD_DIM, NEG_INF, 0.0).astype(BF16)
        for kv in range(N_KV):
            k = k_ref[0, :, kv * HEAD_DIM:(kv + 1) * HEAD_DIM]
            kb_ref[kv, 0:pad, :] = pad_keys
            kb_ref[kv, pad:pad + s_len, :] = jnp.concatenate(
                [k, jnp.zeros((s_len, LANES - HEAD_DIM), F32)], axis=1).astype(BF16)
            vt_ref[kv, :, 0:pad] = jnp.zeros((HEAD_DIM, pad), BF16)
            vt_ref[kv, :, pad:pad + s_len] = vt[kv * HEAD_DIM:(kv + 1) * HEAD_DIM, :].astype(BF16)

    krow = lax.broadcasted_iota(jnp.int32, (blk, cols), 0)
    qcol = lax.broadcasted_iota(jnp.int32, (blk, cols), 1) & (blk - 1)
    in_window = krow > qcol
    causal = krow <= qcol
    one_row = lax.broadcasted_iota(jnp.int32, (LANES - HEAD_DIM, blk), 0) == 0
    pad_hit = jnp.where(one_row, 1.0, 0.0)
    starts, scores = [], []
    for j in range(q_blocks):
        start = pl.multiple_of((n * q_blocks + j) * blk, blk)
        qt = _queries_t(q_ref, slice(j * blk, (j + 1) * blk))
        starts.append(start)
        for kv in range(N_KV):
            s = jnp.dot(kb_ref[kv, pl.ds(start, span), :], _group_t(qt, kv, extra=(pad_hit,)),
                        preferred_element_type=F32)
            bands = [jnp.where(in_window, s[0:blk], NEG_INF)]
            if n_prev > 1:
                bands.append(s[blk:span - blk])
            bands.append(jnp.where(causal, s[span - blk:span], NEG_INF))
            scores.append(jnp.concatenate(bands, axis=0))
    probs = []
    for c, s in enumerate(scores):
        kv = c % N_KV
        m = jnp.max(s, axis=0, keepdims=True)
        if use_sink:
            sk = jnp.concatenate([jnp.full((1, blk), sink_ref[kv * GROUP + g] * LOG2E, F32)
                                  for g in range(GROUP)], axis=1)
            m = jnp.maximum(m, sk)
            p = jnp.exp2(s - m)
            denom = jnp.sum(p, axis=0, keepdims=True) + jnp.exp2(sk - m)
        else:
            p = jnp.exp2(s - m)
            denom = jnp.sum(p, axis=0, keepdims=True)
        probs.append((p.astype(BF16), denom))
    for j in range(q_blocks):
        tok = slice(j * blk, (j + 1) * blk)
        outs = []
        for kv in range(N_KV):
            p, denom = probs[j * N_KV + kv]
            o = jnp.dot(vt_ref[kv, :, pl.ds(starts[j], span)], p,
                        preferred_element_type=F32)
            outs.append(o * (1.0 / denom))
        gates = _gates_t(gl_ref, tok) if gate_row is not None else None
        _store_heads(o_ref, tok, outs, gates, gate_row, blk)


def _window_attention(proj, q_col, k_col, v_col, window, sinks=None, gate_row=None, *,
                      q_blocks=4):
    b, s, _ = proj.shape
    n_prev = window // ATTN_BLOCK
    assert window == n_prev * ATTN_BLOCK, "band masks assume a window of whole blocks"
    pad = n_prev * ATTN_BLOCK
    blk = q_blocks * ATTN_BLOCK
    in_specs = [
        pl.BlockSpec((1, blk, D_ATT), lambda i, n: (i, n, q_col // D_ATT)),
        pl.BlockSpec((1, s, D_KV), lambda i, n: (i, 0, k_col // D_KV)),
        pl.BlockSpec((1, s, D_KV), lambda i, n: (i, 0, v_col // D_KV)),
    ]
    args = [proj, proj, proj]
    if sinks is not None:
        in_specs.append(pl.BlockSpec(memory_space=pltpu.SMEM))
        args.append(sinks)
    if gate_row is not None:
        in_specs.append(pl.BlockSpec((1, blk, LANES), lambda i, n: (i, n, COL_GL // LANES)))
        args.append(proj)
    return pl.pallas_call(
        functools.partial(_window_kernel, n_prev=n_prev, q_blocks=q_blocks,
                          use_sink=sinks is not None, gate_row=gate_row),
        grid=(b, s // blk),
        in_specs=in_specs,
        out_specs=pl.BlockSpec((1, blk, D_ATT), lambda i, n: (i, n, 0)),
        out_shape=jax.ShapeDtypeStruct((b, s, D_ATT), F32),
        scratch_shapes=[pltpu.VMEM((N_KV, pad + s, LANES), BF16),
                        pltpu.VMEM((N_KV, HEAD_DIM, pad + s), BF16)],
        compiler_params=_params(("parallel", "arbitrary")),
        name=f"window{window}",
    )(*args)


def _compress_kernel(k_ref, v_ref, pos_ref, w1_ref, b1_ref, w2_ref, kc_ref, vc_ref):
    n_rows = kc_ref.shape[1]
    for i, (src_ref, dst_ref) in enumerate(((k_ref, kc_ref), (v_ref, vc_ref))):
        lo = jnp.zeros((n_rows, N_KV * CMP_HIDDEN), F32)
        hi = jnp.zeros((n_rows, N_KV * CMP_HIDDEN), F32)
        for l in range(CMP_STRIDE):
            x = src_ref[0, pl.ds(l, n_rows, stride=CMP_STRIDE), :]
            lo = lo + jnp.dot((x + pos_ref[i, l:l + 1, :]).astype(BF16), w1_ref[i, l],
                              preferred_element_type=F32)
            hi = hi + jnp.dot((x + pos_ref[i, CMP_STRIDE + l:CMP_STRIDE + l + 1, :]).astype(BF16),
                              w1_ref[i, CMP_STRIDE + l], preferred_element_type=F32)
        hid = lo + pltpu.roll(hi, n_rows - 1, 0) + b1_ref[i]
        act = jax.nn.gelu(hid, approximate=True).astype(BF16)
        dst_ref[0] = jnp.dot(act, w2_ref[i], preferred_element_type=F32)


def _compress(proj, pos, w1, b1, w2):
    b, s, _ = proj.shape
    n_rows = s // CMP_STRIDE
    out_spec = pl.BlockSpec((1, n_rows, D_KV), lambda i: (i, 0, 0))
    out_shape = jax.ShapeDtypeStruct((b, n_rows, D_KV), F32)
    return pl.pallas_call(
        _compress_kernel,
        grid=(b,),
        in_specs=[
            pl.BlockSpec((1, s, D_KV), lambda i: (i, 0, COL_KC // D_KV)),
            pl.BlockSpec((1, s, D_KV), lambda i: (i, 0, COL_VC // D_KV)),
            pl.BlockSpec(pos.shape, lambda i: (0, 0, 0)),
            pl.BlockSpec(w1.shape, lambda i: (0, 0, 0, 0)),
            pl.BlockSpec(b1.shape, lambda i: (0, 0, 0)),
            pl.BlockSpec(w2.shape, lambda i: (0, 0, 0)),
        ],
        out_specs=[out_spec, out_spec],
        out_shape=[out_shape, out_shape],
        compiler_params=_params(("parallel",)),
        name="compress",
    )(proj, proj, pos, w1, b1, w2)


def _cmp_attn_kernel(q_ref, kc_ref, vc_ref, gl_ref, ov_ref, o_ref, bias_ref, *, tq, n_sel):
    n = pl.program_id(1)
    ncp = kc_ref.shape[1]
    cols = GROUP * tq
    pos_c = n * tq + (lax.broadcasted_iota(jnp.int32, (ncp, cols), 1) & (tq - 1))
    cend = lax.broadcasted_iota(jnp.int32, (ncp, cols), 0) * CMP_STRIDE + (CMP_LEN - 1)
    cmask = cend <= pos_c

    pos_s = n * tq + lax.broadcasted_iota(jnp.int32, (n_sel, tq), 1)
    bid = lax.broadcasted_iota(jnp.int32, (n_sel, tq), 0)
    cur = pos_s // SEL_BLOCK
    forced = (bid == 0) | (bid == cur) | (bid == cur - 1)
    valid = bid * SEL_BLOCK <= pos_s

    tok = slice(None)
    qt = _queries_t(q_ref, tok)
    vct = vc_ref[0].T
    outs = []
    for kv in range(N_KV):
        kc = kc_ref[0, :, kv * HEAD_DIM:(kv + 1) * HEAD_DIM].astype(BF16)
        s = jnp.dot(kc, _group_t(qt, kv), preferred_element_type=F32)
        s = jnp.where(cmask, s, NEG_INF)
        m = jnp.max(s, axis=0, keepdims=True)
        p = jnp.exp2(s - m)
        p = p * (1.0 / jnp.sum(p, axis=0, keepdims=True))
        p = jnp.where(cmask, p, 0.0)
        outs.append(jnp.dot(vct[kv * HEAD_DIM:(kv + 1) * HEAD_DIM, :].astype(BF16),
                            p.astype(BF16), preferred_element_type=F32))
        psum = p[:, 0:tq]
        for g in range(1, GROUP):
            psum = psum + p[:, g * tq:(g + 1) * tq]

        imp = jnp.dot(ov_ref[...], psum, preferred_element_type=F32,
                      precision=lax.Precision.HIGHEST)
        imp = jnp.where(forced, FORCED_SCORE, imp)
        imp = jnp.where(valid, imp, NEG_INF)
        rank = jnp.zeros((n_sel, tq), jnp.int32)
        for i in range(n_sel):
            row = imp[i:i + 1, :]
            ahead = (row > imp) | ((row == imp) & (bid > i))
            rank = rank + ahead.astype(jnp.int32)
        bias_ref[0, kv * n_sel:(kv + 1) * n_sel, :] = jnp.where(rank < TOP_N, 0.0, NEG_INF)
    _store_heads(o_ref, tok, outs, _gates_t(gl_ref, tok), 0, tq)


def _cmp_attention(proj, kc, vc, overlap, *, tq=256):
    b, s, _ = proj.shape
    n_sel = overlap.shape[0]
    ncp = kc.shape[1]
    return pl.pallas_call(
        functools.partial(_cmp_attn_kernel, tq=tq, n_sel=n_sel),
        grid=(b, s // tq),
        in_specs=[
            pl.BlockSpec((1, tq, D_ATT), lambda i, n: (i, n, COL_QB // D_ATT)),
            pl.BlockSpec((1, ncp, D_KV), lambda i, n: (i, 0, 0)),
            pl.BlockSpec((1, ncp, D_KV), lambda i, n: (i, 0, 0)),
            pl.BlockSpec((1, tq, LANES), lambda i, n: (i, n, COL_GL // LANES)),
            pl.BlockSpec(overlap.shape, lambda i, n: (0, 0)),
        ],
        out_specs=[
            pl.BlockSpec((1, tq, D_ATT), lambda i, n: (i, n, 0)),
            pl.BlockSpec((1, N_KV * n_sel, tq), lambda i, n: (i, 0, n)),
        ],
        out_shape=[
            jax.ShapeDtypeStruct((b, s, D_ATT), F32),
            jax.ShapeDtypeStruct((b, N_KV * n_sel, s), F32),
        ],
        compiler_params=_params(("parallel", "arbitrary")),
        name="cmp_attn",
    )(proj, kc, vc, proj, overlap)


def _sel_attn_kernel(q_ref, k_ref, v_ref, bias_ref, gl_ref, *rest,
                     tq, tk, chains, n_sel, gate_row, n_cast):
    cast_src = rest[:n_cast]
    o_ref = rest[n_cast]
    cast_dst = rest[n_cast + 1:2 * n_cast + 1]
    qa_ref, ka_ref, vt_ref = rest[2 * n_cast + 1:]
    for src, dst in zip(cast_src, cast_dst):
        dst[...] = src[...].astype(BF16)

    n = pl.program_id(1)
    s_len = k_ref.shape[1]
    cols = GROUP * tq
    n_pad = LANES - HEAD_DIM - n_sel

    @pl.when(n == 0)
    def _():
        kpos = lax.broadcasted_iota(jnp.int32, (s_len, n_sel), 0)
        blk = lax.broadcasted_iota(jnp.int32, (s_len, n_sel), 1)
        onehot = (kpos // SEL_BLOCK == blk).astype(F32)
        zeros = jnp.zeros((s_len, n_pad), F32)
        vt = v_ref[0].T
        for kv in range(N_KV):
            k = k_ref[0, :, kv * HEAD_DIM:(kv + 1) * HEAD_DIM]
            ka_ref[kv] = jnp.concatenate([k, onehot, zeros], axis=1).astype(BF16)
            for t in range(s_len // tk):
                vt_ref[kv, t] = vt[kv * HEAD_DIM:(kv + 1) * HEAD_DIM,
                                   t * tk:(t + 1) * tk].astype(BF16)

    tok = slice(None)
    qt = _queries_t(q_ref, tok)
    zq = jnp.zeros((n_pad, tq), F32)
    for kv in range(N_KV):
        bias = bias_ref[0, kv * n_sel:(kv + 1) * n_sel, :]
        qa_ref[kv] = _group_t(qt, kv, extra=(bias, zq))
    ccols = cols // chains
    qpos = n * tq + (lax.broadcasted_iota(jnp.int32, (tk, ccols), 1) & (tq - 1))
    krow = lax.broadcasted_iota(jnp.int32, (tk, ccols), 0)

    def step(kt, state, causal):
        k0 = pl.multiple_of(kt * tk, tk)
        n_chain = len(state)
        scores, probs, new_state = {}, {}, {}

        def score(c):
            kv, part = divmod(c, chains)
            cs = slice(part * ccols, (part + 1) * ccols)
            s = jnp.dot(ka_ref[kv, pl.ds(k0, tk), :], qa_ref[kv, :, cs],
                        preferred_element_type=F32)
            scores[c] = jnp.where(k0 + krow <= qpos, s, NEG_INF) if causal else s

        def softmax(c):
            m_old, l_old, _ = state[c]
            s = scores.pop(c)
            m_new = jnp.maximum(m_old, jnp.max(s, axis=0, keepdims=True))
            alpha = jnp.exp2(m_old - m_new)
            p = jnp.exp2(s - m_new)
            l_new = alpha * l_old + jnp.sum(p, axis=0, keepdims=True)
            probs[c] = (m_new, l_new, alpha, p.astype(BF16))

        def weigh(c):
            m_new, l_new, alpha, p = probs.pop(c)
            acc_new = alpha * state[c][2] + jnp.dot(vt_ref[c // chains, kt], p,
                                                    preferred_element_type=F32)
            new_state[c] = (m_new, l_new, acc_new)

        for t in range(n_chain + SCORE_LEAD + 1):
            if t < n_chain:
                score(t)
            if 0 <= t - SCORE_LEAD < n_chain:
                softmax(t - SCORE_LEAD)
            if 0 <= t - SCORE_LEAD - 1 < n_chain:
                weigh(t - SCORE_LEAD - 1)
        return tuple(new_state[c] for c in range(n_chain))

    init = tuple((jnp.full((1, ccols), NEG_INF, F32), jnp.zeros((1, ccols), F32),
                  jnp.zeros((HEAD_DIM, ccols), F32)) for _ in range(N_KV * chains))
    n_full = (n * tq) // tk
    state = lax.fori_loop(0, n_full, functools.partial(step, causal=False), init)
    state = step(n_full, state, causal=True)

    outs = []
    for kv in range(N_KV):
        parts = [acc * (1.0 / l) for _, l, acc in state[kv * chains:(kv + 1) * chains]]
        outs.append(jnp.concatenate(parts, axis=1))
    _store_heads(o_ref, tok, outs, _gates_t(gl_ref, tok), gate_row, tq)


def _sel_attention(proj, bias, cast_jobs=(), *, tq=512, tk=512, chains=4):
    b, s, _ = proj.shape
    n_sel = bias.shape[1] // N_KV
    cols = GROUP * tq
    nq = s // tq
    steps = b * nq
    cast_in, cast_out, cast_shapes = [], [], []
    for w, layer in cast_jobs:
        _, k, nn = w.shape
        rows = k // steps
        cast_in.append(pl.BlockSpec((None, rows, nn),
                                    lambda i, n, layer=layer: (layer, i * nq + n, 0)))
        cast_out.append(pl.BlockSpec((None, rows, nn), lambda i, n: (0, i * nq + n, 0)))
        cast_shapes.append(jax.ShapeDtypeStruct((1, k, nn), BF16))
    outs = pl.pallas_call(
        functools.partial(_sel_attn_kernel, tq=tq, tk=tk, chains=chains, n_sel=n_sel,
                          gate_row=N_HEADS, n_cast=len(cast_jobs)),
        grid=(b, nq),
        in_specs=[
            pl.BlockSpec((1, tq, D_ATT), lambda i, n: (i, n, COL_QB // D_ATT)),
            pl.BlockSpec((1, s, D_KV), lambda i, n: (i, 0, COL_KS // D_KV)),
            pl.BlockSpec((1, s, D_KV), lambda i, n: (i, 0, COL_VS // D_KV)),
            pl.BlockSpec((1, N_KV * n_sel, tq), lambda i, n: (i, 0, n)),
            pl.BlockSpec((1, tq, LANES), lambda i, n: (i, n, COL_GL // LANES)),
            *cast_in,
        ],
        out_specs=[pl.BlockSpec((1, tq, D_ATT), lambda i, n: (i, n, 0)), *cast_out],
        out_shape=[jax.ShapeDtypeStruct((b, s, D_ATT), F32), *cast_shapes],
        scratch_shapes=[
            pltpu.VMEM((N_KV, LANES, cols), BF16),
            pltpu.VMEM((N_KV, s, LANES), BF16),
            pltpu.VMEM((N_KV, s // tk, HEAD_DIM, tk), BF16),
        ],
        compiler_params=_params(("parallel", "arbitrary"), VMEM_LIMIT_BIG),
        name="sel_attn",
    )(proj, proj, proj, bias, proj, *[w for w, _ in cast_jobs])
    return outs[0], list(outs[1:])


def _rglru_kernel(xr_ref, xg_ref, cw_ref, cb_ref, wa_ref, ba_ref, wx_ref, bx_ref, lam_ref,
                  o_ref, xbuf_ref, h_ref, *, ts):
    pad = SUBLANES

    @pl.when(pl.program_id(1) == 0)
    def _():
        xbuf_ref[0:pad, :] = jnp.zeros((pad, RNN_WIDTH), F32)
        h_ref[...] = jnp.zeros_like(h_ref)

    xr = xr_ref[0]
    xbuf_ref[pad:pad + ts, :] = xr
    xc = cb_ref[...] + cw_ref[CONV_WIDTH - 1:CONV_WIDTH, :] * xr
    for w in range(CONV_WIDTH - 1):
        shift = CONV_WIDTH - 1 - w
        xc = xc + cw_ref[w:w + 1, :] * xbuf_ref[pl.ds(pad - shift, ts), :]
    xbuf_ref[0:pad, :] = xr[ts - pad:ts, :]

    ra, rx = [], []
    for blk in range(RNN_WIDTH // MXU_DIM):
        xs = xc[:, blk * MXU_DIM:(blk + 1) * MXU_DIM].astype(BF16)
        ra.append(jnp.dot(xs, wa_ref[blk], preferred_element_type=F32))
        rx.append(jnp.dot(xs, wx_ref[blk], preferred_element_type=F32))
    r = 0.5 * (jnp.tanh(0.5 * (jnp.concatenate(ra, axis=1) + ba_ref[...])) + 1.0)
    gi = 0.5 * (jnp.tanh(0.5 * (jnp.concatenate(rx, axis=1) + bx_ref[...])) + 1.0)
    nl = -lam_ref[...]
    softplus = jnp.maximum(nl, 0.0) + jnp.log1p(jnp.exp(-jnp.abs(nl)))
    log_a = -LRU_C * r * softplus
    a = jnp.exp(log_a)
    u = jnp.sqrt(-jnp.tanh(log_a) * (a * a + 1.0)) * (gi * xc)

    n_slab = ts // SUBLANES
    a = a.reshape(n_slab, SUBLANES, RNN_WIDTH)
    u = u.reshape(n_slab, SUBLANES, RNN_WIDTH)
    sub = lax.broadcasted_iota(jnp.int32, a.shape, 1)
    d = 1
    while d < SUBLANES:
        keep = sub >= d
        a_prev = jnp.where(keep, pltpu.roll(a, d, 1), 1.0)
        u_prev = jnp.where(keep, pltpu.roll(u, d, 1), 0.0)
        u = a * u_prev + u
        a = a * a_prev
        d *= 2
    h_last = h_ref[0:1, :]
    slabs = []
    for r in range(n_slab):
        h = a[r] * h_last + u[r]
        slabs.append(h)
        h_last = h[SUBLANES - 1:SUBLANES, :]
    h_ref[0:1, :] = h_last
    o_ref[0] = jax.nn.gelu(xg_ref[0], approximate=True) * jnp.concatenate(slabs, axis=0)


def _rglru(proj, conv_w, conv_b, wa, ba, wx, bx, lam, *, ts=256):
    b, s, _ = proj.shape
    c = RNN_WIDTH
    vec = pl.BlockSpec((1, c), lambda i, n: (0, 0))
    wspec = pl.BlockSpec(wa.shape, lambda i, n: (0, 0, 0))
    return pl.pallas_call(
        functools.partial(_rglru_kernel, ts=ts),
        grid=(b, s // ts),
        in_specs=[
            pl.BlockSpec((1, ts, c), lambda i, n: (i, n, COL_XR // c)),
            pl.BlockSpec((1, ts, c), lambda i, n: (i, n, COL_XG // c)),
            pl.BlockSpec((CONV_WIDTH, c), lambda i, n: (0, 0)),
            vec, wspec, vec, wspec, vec, vec,
        ],
        out_specs=pl.BlockSpec((1, ts, c), lambda i, n: (i, n, 0)),
        out_shape=jax.ShapeDtypeStruct((b, s, c), F32),
        scratch_shapes=[pltpu.VMEM((ts + SUBLANES, c), F32), pltpu.VMEM((SUBLANES, c), F32)],
        compiler_params=_params(("parallel", "arbitrary")),
        name="rglru",
    )(proj, proj, conv_w, conv_b, wa, ba, wx, bx, lam)


def _outproj_kernel(x_ref, ya_ref, oc_ref, os_ref, ow_ref, yc_ref, gn_ref, w_ref, o_ref):
    ya = _rms(ya_ref[...], gn_ref[:, 0:D_ATT]).astype(BF16)
    yb = oc_ref[...] + os_ref[...] + ow_ref[...]
    yb = _rms(yb, gn_ref[:, D_ATT:2 * D_ATT]).astype(BF16)
    yc = _rms(yc_ref[...], gn_ref[:, 2 * D_ATT:]).astype(BF16)
    y = jnp.dot(ya, w_ref[0:D_ATT, :], preferred_element_type=F32)
    y = y + jnp.dot(yb, w_ref[D_ATT:2 * D_ATT, :], preferred_element_type=F32)
    y = y + jnp.dot(yc, w_ref[2 * D_ATT:, :], preferred_element_type=F32)
    o_ref[...] = x_ref[...] + y


def _outproj(x, ya, oc, osl, ow, yc, gn, w, layer, *, tm=256):
    t, d = x.shape
    att = pl.BlockSpec((tm, D_ATT), lambda i: (i, 0))
    return pl.pallas_call(
        _outproj_kernel,
        grid=(t // tm,),
        in_specs=[
            pl.BlockSpec((tm, d), lambda i: (i, 0)),
            att, att, att, att,
            pl.BlockSpec((tm, RNN_WIDTH), lambda i: (i, 0)),
            pl.BlockSpec((1, gn.shape[1]), lambda i: (0, 0)),
            pl.BlockSpec((None,) + w.shape[1:], lambda i: (layer, 0, 0)),
        ],
        out_specs=pl.BlockSpec((tm, d), lambda i: (i, 0)),
        out_shape=jax.ShapeDtypeStruct((t, d), F32),
        compiler_params=_params(("parallel",)),
        name="outproj",
    )(x, ya, oc, osl, ow, yc, gn, w)


def _block_diag(w, per):
    *lead, n, r, c = w.shape
    w = w.reshape(*lead, n // per, per, r, c)
    eye = jnp.eye(per, dtype=w.dtype)
    out = jnp.einsum('...prc,pq->...prqc', w, eye)
    return out.reshape(*lead, n // per, per * r, per * c)


def _compress_weights(cmp_pos, w1, b1, w2):
    pos = jnp.tile(cmp_pos, (1, 1, N_KV))
    w1 = w1.reshape(2, CMP_LEN, 1, HEAD_DIM, CMP_HIDDEN)
    w1 = _block_diag(jnp.broadcast_to(w1, (2, CMP_LEN, N_KV, HEAD_DIM, CMP_HIDDEN)), N_KV)
    w1 = w1.reshape(2, CMP_LEN, N_KV * HEAD_DIM, N_KV * CMP_HIDDEN)
    w2 = _block_diag(jnp.broadcast_to(w2[:, None], (2, N_KV, CMP_HIDDEN, HEAD_DIM)), N_KV)
    w2 = w2.reshape(2, N_KV * CMP_HIDDEN, N_KV * HEAD_DIM)
    b1 = jnp.tile(b1[:, None, :], (1, 1, N_KV))
    return pos, w1.astype(BF16), b1, w2.astype(BF16)


def _overlap_matrix(seq, n_cmp_padded):
    n_c = (seq - CMP_LEN) // CMP_STRIDE + 1
    n_sel = seq // SEL_BLOCK
    cs = np.arange(n_c)[:, None] * CMP_STRIDE
    ss = np.arange(n_sel)[None, :] * SEL_BLOCK
    ov = np.clip(np.minimum(cs + CMP_LEN, ss + SEL_BLOCK) - np.maximum(cs, ss), 0, None)
    out = np.zeros((n_cmp_padded, n_sel), np.float32)
    out[:n_c] = ov / CMP_LEN
    return jnp.asarray(out.T)


def kernel(x, ffn1_norm, ffn1_w_gate, ffn1_w_up, ffn1_w_down, mix_norm, w_in, swa_sinks, cmp_pos, cmp_w1, cmp_b1, cmp_w2, conv_w, conv_b, lru_wa, lru_ba, lru_wx, lru_bx, lru_lambda, group_norm, w_out, ffn2_norm, ffn2_w_gate, ffn2_w_up, ffn2_w_down, final_norm):
    b, s, d = x.shape
    depth = w_in.shape[0]
    t = b * s
    overlap = _overlap_matrix(s, s // CMP_STRIDE)
    gate_tile = MXU_DIM // RNN_BLOCK_WIDTH
    xt = x.reshape(t, d)
    ffn1_f32 = (ffn1_w_gate, ffn1_w_up, ffn1_w_down)
    ffn2_f32 = (ffn2_w_gate, ffn2_w_up, ffn2_w_down)
    ffn1_w = [_to_bf16(ffn1_w_gate, rows=256, layer=0), _to_bf16(ffn1_w_up, rows=256, layer=0),
              _to_bf16(ffn1_w_down, rows=512, layer=0)]
    w_in_b = _stage_w_in(w_in)
    w_out_b = _to_bf16(w_out, rows=512)
    for l in range(depth):
        xt = _ffn(xt, ffn1_norm[l][None], *ffn1_w, 0)

        proj = _inproj(xt, mix_norm[l][None], w_in_b, l)
        proj = proj.reshape(b, s, D_PROJ)

        ya = _window_attention(proj, COL_QA, COL_KA, COL_VA, SWA_WINDOW, sinks=swa_sinks[l])

        kc, vc = _compress(proj, *_compress_weights(cmp_pos[l], cmp_w1[l], cmp_b1[l], cmp_w2[l]))
        o_cmp, bias = _cmp_attention(proj, kc, vc, overlap)
        jobs = [(w, l) for w in ffn2_f32]
        if l + 1 < depth:
            jobs += [(w, l + 1) for w in ffn1_f32]
        o_slc, staged = _sel_attention(proj, bias, jobs)
        ffn2_w, ffn1_w = staged[:3], staged[3:]
        o_win = _window_attention(proj, COL_QB, COL_KW, COL_VW, NSA_WINDOW,
                                  gate_row=2 * N_HEADS, q_blocks=2)

        yc = _rglru(proj, conv_w[l], conv_b[l][None],
                    _block_diag(lru_wa[l], gate_tile).astype(BF16), lru_ba[l][None],
                    _block_diag(lru_wx[l], gate_tile).astype(BF16), lru_bx[l][None],
                    lru_lambda[l][None])

        xt = _outproj(xt, ya.reshape(t, D_ATT), o_cmp.reshape(t, D_ATT),
                      o_slc.reshape(t, D_ATT), o_win.reshape(t, D_ATT),
                      yc.reshape(t, RNN_WIDTH), group_norm[l][None], w_out_b, l)

        last = l == depth - 1
        xt = _ffn(xt, ffn2_norm[l][None], *ffn2_w, 0,
                  final_g=final_norm[None] if last else None)
    return xt.reshape(b, s, d)
```

```python
import functools

import jax
import jax.numpy as jnp
import numpy as np
from jax import lax
from jax.experimental import pallas as pl
from jax.experimental.pallas import tpu as pltpu

F32 = jnp.float32
BF16 = jnp.bfloat16

HEAD_DIM = 64
N_HEADS = 8
N_KV = 2
GROUP = N_HEADS // N_KV
D_ATT = N_HEADS * HEAD_DIM
D_KV = N_KV * HEAD_DIM
SWA_WINDOW = 128
NSA_WINDOW = 512
ATTN_BLOCK = 128
CMP_LEN = 32
CMP_STRIDE = 16
CMP_HIDDEN = 256
SEL_BLOCK = 64
TOP_N = 16
RNN_WIDTH = 1024
RNN_BLOCK_WIDTH = 64
CONV_WIDTH = 4
LRU_C = 8.0
RMS_EPS = 1e-6
NEG_INF = -1e30
FORCED_SCORE = 1e4
SCALE = HEAD_DIM ** -0.5
LOG2E = 1.4426950408889634
SCALE_LOG2 = SCALE * LOG2E

LANES = 128
SUBLANES = 8
MXU_DIM = 256
VMEM_BYTES = 64 * 1024 * 1024
VMEM_LIMIT = VMEM_BYTES - 8 * 1024 * 1024
VMEM_LIMIT_BIG = VMEM_BYTES - 4 * 1024 * 1024
DOWN_CHUNK = 512
INPROJ_CHUNK = 4 * MXU_DIM
SCORE_LEAD = 2

COL_XR = 0
COL_XG = 1024
COL_QA = 2048
COL_QB = 2560
COL_KA = 3072
COL_VA = 3200
COL_KC = 3328
COL_VC = 3456
COL_KS = 3584
COL_VS = 3712
COL_KW = 3840
COL_VW = 3968
COL_GL = 4096
D_PROJ = 4224

_REF_SPLITS = (D_ATT, D_KV, D_KV, D_ATT, D_KV, D_KV, D_KV, D_KV, D_KV, D_KV, 3 * N_HEADS,
               RNN_WIDTH, RNN_WIDTH)


def _params(sem, vmem_limit=VMEM_LIMIT):
    return pltpu.CompilerParams(dimension_semantics=sem, vmem_limit_bytes=vmem_limit)


def _rms(x, g):
    return x * lax.rsqrt(jnp.mean(x * x, axis=-1, keepdims=True) + RMS_EPS) * g


def _cast_kernel(w_ref, o_ref):
    o_ref[...] = w_ref[...].astype(BF16)


def _to_bf16(w, *, rows, layer=None):
    dep, k, n = w.shape
    first, count = (0, dep) if layer is None else (layer, 1)
    return pl.pallas_call(
        _cast_kernel,
        grid=(count, k // rows),
        in_specs=[pl.BlockSpec((1, rows, n), lambda l, i: (first + l, i, 0))],
        out_specs=pl.BlockSpec((1, rows, n), lambda l, i: (l, i, 0)),
        out_shape=jax.ShapeDtypeStruct((count, k, n), BF16),
        compiler_params=_params(("parallel", "parallel")),
        name="cast_bf16",
    )(w)


def _w_in_kernel(w_ref, o_ref):
    wt = w_ref[0]
    offs = np.concatenate([[0], np.cumsum(_REF_SPLITS)])
    qa, ka, va, qb, kc, vc, ks, vs, kw, vw, gl, xr, xg = [
        wt[int(offs[i]):int(offs[i + 1]), :] for i in range(len(_REF_SPLITS))]
    gl = jnp.concatenate([gl, jnp.zeros((LANES - gl.shape[0], wt.shape[1]), F32)], axis=0)
    col = 0
    for piece in (xr, xg, qa, qb, ka, va, kc, vc, ks, vs, kw, vw, gl):
        o_ref[0, :, col:col + piece.shape[0]] = piece.T.astype(BF16)
        col += piece.shape[0]


def _stage_w_in(w, *, rows=256):
    dep, k, n = w.shape
    return pl.pallas_call(
        _w_in_kernel,
        grid=(dep, k // rows),
        in_specs=[pl.BlockSpec((1, n, rows), lambda l, i: (l, 0, i))],
        out_specs=pl.BlockSpec((1, rows, D_PROJ), lambda l, i: (l, i, 0)),
        out_shape=jax.ShapeDtypeStruct((dep, k, D_PROJ), BF16),
        compiler_params=_params(("parallel", "parallel")),
        name="stage_w_in",
    )(jnp.swapaxes(w, 1, 2))


def _ffn_kernel(x_ref, g_ref, wg_ref, wu_ref, wd_ref, *rest, final_norm):
    if final_norm:
        fg_ref, o_ref, h_ref = rest
    else:
        o_ref, h_ref = rest
    j = pl.program_id(1)

    @pl.when(j == 0)
    def _():
        x = x_ref[...]
        h_ref[...] = _rms(x, g_ref[...]).astype(BF16)
        o_ref[...] = x

    h = h_ref[...]
    gate = jnp.dot(h, wg_ref[...], preferred_element_type=F32)
    up = jnp.dot(h, wu_ref[...], preferred_element_type=F32)
    act = (0.5 * gate * jax.nn.sigmoid(gate) * up).astype(BF16)
    for c in range(0, o_ref.shape[1], DOWN_CHUNK):
        o_ref[:, c:c + DOWN_CHUNK] += jnp.dot(act, wd_ref[:, c:c + DOWN_CHUNK],
                                              preferred_element_type=F32)

    if final_norm:
        @pl.when(j == pl.num_programs(1) - 1)
        def _():
            o_ref[...] = _rms(o_ref[...], fg_ref[...])


def _ffn(x, g, wg, wu, wd, layer, final_g=None, *, tm=1024, tf=512):
    t, d = x.shape
    f = wg.shape[2]
    final_norm = final_g is not None
    in_specs = [
        pl.BlockSpec((tm, d), lambda i, j: (i, 0)),
        pl.BlockSpec((1, d), lambda i, j: (0, 0)),
        pl.BlockSpec((None, d, tf), lambda i, j: (layer, 0, j)),
        pl.BlockSpec((None, d, tf), lambda i, j: (layer, 0, j)),
        pl.BlockSpec((None, tf, d), lambda i, j: (layer, j, 0)),
    ]
    args = [x, g, wg, wu, wd]
    if final_norm:
        in_specs.append(pl.BlockSpec((1, d), lambda i, j: (0, 0)))
        args.append(final_g)
    return pl.pallas_call(
        functools.partial(_ffn_kernel, final_norm=final_norm),
        grid=(t // tm, f // tf),
        in_specs=in_specs,
        out_specs=pl.BlockSpec((tm, d), lambda i, j: (i, 0)),
        out_shape=jax.ShapeDtypeStruct((t, d), F32),
        scratch_shapes=[pltpu.VMEM((tm, d), BF16)],
        compiler_params=_params(("parallel", "arbitrary"), VMEM_LIMIT_BIG),
        name="ffn",
    )(*args)


def _inproj_kernel(x_ref, g_ref, w_ref, o_ref):
    h = _rms(x_ref[...], g_ref[...]).astype(BF16)
    n = o_ref.shape[1]
    for c in range(0, n, INPROJ_CHUNK):
        hi = min(c + INPROJ_CHUNK, n)
        o_ref[:, c:hi] = jnp.dot(h, w_ref[:, c:hi], preferred_element_type=F32)


def _inproj(x, g, w, layer, *, tm=512):
    t, d = x.shape
    n = w.shape[2]
    return pl.pallas_call(
        _inproj_kernel,
        grid=(t // tm,),
        in_specs=[
            pl.BlockSpec((tm, d), lambda i: (i, 0)),
            pl.BlockSpec((1, d), lambda i: (0, 0)),
            pl.BlockSpec((None, d, n), lambda i: (layer, 0, 0), pipeline_mode=pl.Buffered(1)),
        ],
        out_specs=pl.BlockSpec((tm, n), lambda i: (i, 0)),
        out_shape=jax.ShapeDtypeStruct((t, n), F32),
        compiler_params=_params(("parallel",)),
        name="inproj",
    )(x, g, w)


def _queries_t(q_ref, tok):
    return q_ref[0, tok, :].T * SCALE_LOG2


def _group_t(qt, kv, extra=()):
    cols = []
    for g in range(GROUP):
        h = kv * GROUP + g
        parts = [qt[h * HEAD_DIM:(h + 1) * HEAD_DIM, :], *extra]
        cols.append(parts[0] if len(parts) == 1 else jnp.concatenate(parts, axis=0))
    return jnp.concatenate(cols, axis=1).astype(BF16)


def _store_heads(o_ref, tok, outs_t, gates_t, gate_row, tq):
    pieces = []
    for kv in range(N_KV):
        for g in range(GROUP):
            h = kv * GROUP + g
            piece = outs_t[kv][:, g * tq:(g + 1) * tq]
            if gates_t is not None:
                piece = piece * gates_t[gate_row + h:gate_row + h + 1, :]
            pieces.append(piece)
    o_ref[0, tok, :] = jnp.concatenate(pieces, axis=0).T


def _gates_t(gl_ref, tok):
    return jax.nn.sigmoid(gl_ref[0, tok, :].T)


def _window_kernel(*refs, n_prev, q_blocks, use_sink, gate_row):
    refs = list(refs)
    q_ref, k_ref, v_ref = refs[:3]
    rest = refs[3:]
    sink_ref = rest.pop(0) if use_sink else None
    gl_ref = rest.pop(0) if gate_row is not None else None
    o_ref, kb_ref, vt_ref = rest

    blk = ATTN_BLOCK
    cols = GROUP * blk
    span = (n_prev + 1) * blk
    pad = n_prev * blk
    s_len = k_ref.shape[1]
    n = pl.program_id(1)

    @pl.when(n == 0)
    def _():
        vt = v_ref[0].T
        lane = lax.broadcasted_iota(jnp.int32, (pad, LANES), 1)
        pad_keys = jnp.where(lane == HEAD_DIM, NEG_INF, 0.0).astype(BF16)
        for kv in range(N_KV):
            k = k_ref[0, :, kv * HEAD_DIM:(kv + 1) * HEAD_DIM]
            kb_ref[kv, 0:pad, :] = pad_keys
            kb_ref[kv, pad:pad + s_len, :] = jnp.concatenate(
                [k, jnp.zeros((s_len, LANES - HEAD_DIM), F32)], axis=1).astype(BF16)
            vt_ref[kv, :, 0:pad] = jnp.zeros((HEAD_DIM, pad), BF16)
            vt_ref[kv, :, pad:pad + s_len] = vt[kv * HEAD_DIM:(kv + 1) * HEAD_DIM, :].astype(BF16)

    krow = lax.broadcasted_iota(jnp.int32, (blk, cols), 0)
    qcol = lax.broadcasted_iota(jnp.int32, (blk, cols), 1) & (blk - 1)
    in_window = krow > qcol
    causal = krow <= qcol
    one_row = lax.broadcasted_iota(jnp.int32, (LANES - HEAD_DIM, blk), 0) == 0
    pad_hit = jnp.where(one_row, 1.0, 0.0)
    starts, scores = [], []
    for j in range(q_blocks):
        start = pl.multiple_of((n * q_blocks + j) * blk, blk)
        qt = _queries_t(q_ref, slice(j * blk, (j + 1) * blk))
        starts.append(start)
        for kv in range(N_KV):
            s = jnp.dot(kb_ref[kv, pl.ds(start, span), :], _group_t(qt, kv, extra=(pad_hit,)),
                        preferred_element_type=F32)
            bands = [jnp.where(in_window, s[0:blk], NEG_INF)]
            if n_prev > 1:
                bands.append(s[blk:span - blk])
            bands.append(jnp.where(causal, s[span - blk:span], NEG_INF))
            scores.append(jnp.concatenate(bands, axis=0))
    probs = []
    for c, s in enumerate(scores):
        kv = c % N_KV
        m = jnp.max(s, axis=0, keepdims=True)
        if use_sink:
            sk = jnp.concatenate([jnp.full((1, blk), sink_ref[kv * GROUP + g] * LOG2E, F32)
                                  for g in range(GROUP)], axis=1)
            m = jnp.maximum(m, sk)
            p = jnp.exp2(s - m)
            denom = jnp.sum(p, axis=0, keepdims=True) + jnp.exp2(sk - m)
        else:
            p = jnp.exp2(s - m)
            denom = jnp.sum(p, axis=0, keepdims=True)
        probs.append((p.astype(BF16), denom))
    for j in range(q_blocks):
        tok = slice(j * blk, (j + 1) * blk)
        outs = []
        for kv in range(N_KV):
            p, denom = probs[j * N_KV + kv]
            o = jnp.dot(vt_ref[kv, :, pl.ds(starts[j], span)], p,
                        preferred_element_type=F32)
            outs.append(o * (1.0 / denom))
        gates = _gates_t(gl_ref, tok) if gate_row is not None else None
        _store_heads(o_ref, tok, outs, gates, gate_row, blk)


def _window_attention(proj, q_col, k_col, v_col, window, sinks=None, gate_row=None, *,
                      q_blocks=4):
    b, s, _ = proj.shape
    n_prev = window // ATTN_BLOCK
    assert window == n_prev * ATTN_BLOCK, "band masks assume a window of whole blocks"
    pad = n_prev * ATTN_BLOCK
    blk = q_blocks * ATTN_BLOCK
    in_specs = [
        pl.BlockSpec((1, blk, D_ATT), lambda i, n: (i, n, q_col // D_ATT)),
        pl.BlockSpec((1, s, D_KV), lambda i, n: (i, 0, k_col // D_KV)),
        pl.BlockSpec((1, s, D_KV), lambda i, n: (i, 0, v_col // D_KV)),
    ]
    args = [proj, proj, proj]
    if sinks is not None:
        in_specs.append(pl.BlockSpec(memory_space=pltpu.SMEM))
        args.append(sinks)
    if gate_row is not None:
        in_specs.append(pl.BlockSpec((1, blk, LANES), lambda i, n: (i, n, COL_GL // LANES)))
        args.append(proj)
    return pl.pallas_call(
        functools.partial(_window_kernel, n_prev=n_prev, q_blocks=q_blocks,
                          use_sink=sinks is not None, gate_row=gate_row),
        grid=(b, s // blk),
        in_specs=in_specs,
        out_specs=pl.BlockSpec((1, blk, D_ATT), lambda i, n: (i, n, 0)),
        out_shape=jax.ShapeDtypeStruct((b, s, D_ATT), F32),
        scratch_shapes=[pltpu.VMEM((N_KV, pad + s, LANES), BF16),
                        pltpu.VMEM((N_KV, HEAD_DIM, pad + s), BF16)],
        compiler_params=_params(("parallel", "arbitrary")),
        name=f"window{window}",
    )(*args)


def _compress_kernel(k_ref, v_ref, pos_ref, w1_ref, b1_ref, w2_ref, kc_ref, vc_ref):
    n_rows = kc_ref.shape[1]
    for i, (src_ref, dst_ref) in enumerate(((k_ref, kc_ref), (v_ref, vc_ref))):
        lo = jnp.zeros((n_rows, N_KV * CMP_HIDDEN), F32)
        hi = jnp.zeros((n_rows, N_KV * CMP_HIDDEN), F32)
        for l in range(CMP_STRIDE):
            x = src_ref[0, pl.ds(l, n_rows, stride=CMP_STRIDE), :]
            lo = lo + jnp.dot((x + pos_ref[i, l:l + 1, :]).astype(BF16), w1_ref[i, l],
                              preferred_element_type=F32)
            hi = hi + jnp.dot((x + pos_ref[i, CMP_STRIDE + l:CMP_STRIDE + l + 1, :]).astype(BF16),
                              w1_ref[i, CMP_STRIDE + l], preferred_element_type=F32)
        hid = lo + pltpu.roll(hi, n_rows - 1, 0) + b1_ref[i]
        act = jax.nn.gelu(hid, approximate=True).astype(BF16)
        dst_ref[0] = jnp.dot(act, w2_ref[i], preferred_element_type=F32)


def _compress(proj, pos, w1, b1, w2):
    b, s, _ = proj.shape
    n_rows = s // CMP_STRIDE
    out_spec = pl.BlockSpec((1, n_rows, D_KV), lambda i: (i, 0, 0))
    out_shape = jax.ShapeDtypeStruct((b, n_rows, D_KV), F32)
    return pl.pallas_call(
        _compress_kernel,
        grid=(b,),
        in_specs=[
            pl.BlockSpec((1, s, D_KV), lambda i: (i, 0, COL_KC // D_KV)),
            pl.BlockSpec((1, s, D_KV), lambda i: (i, 0, COL_VC // D_KV)),
            pl.BlockSpec(pos.shape, lambda i: (0, 0, 0)),
            pl.BlockSpec(w1.shape, lambda i: (0, 0, 0, 0)),
            pl.BlockSpec(b1.shape, lambda i: (0, 0, 0)),
            pl.BlockSpec(w2.shape, lambda i: (0, 0, 0)),
        ],
        out_specs=[out_spec, out_spec],
        out_shape=[out_shape, out_shape],
        compiler_params=_params(("parallel",)),
        name="compress",
    )(proj, proj, pos, w1, b1, w2)


def _cmp_attn_kernel(q_ref, kc_ref, vc_ref, gl_ref, ov_ref, o_ref, bias_ref, *, tq, n_sel):
    n = pl.program_id(1)
    ncp = kc_ref.shape[1]
    cols = GROUP * tq
    pos_c = n * tq + (lax.broadcasted_iota(jnp.int32, (ncp, cols), 1) & (tq - 1))
    cend = lax.broadcasted_iota(jnp.int32, (ncp, cols), 0) * CMP_STRIDE + (CMP_LEN - 1)
    cmask = cend <= pos_c

    pos_s = n * tq + lax.broadcasted_iota(jnp.int32, (n_sel, tq), 1)
    bid = lax.broadcasted_iota(jnp.int32, (n_sel, tq), 0)
    cur = pos_s // SEL_BLOCK
    forced = (bid == 0) | (bid == cur) | (bid == cur - 1)
    valid = bid * SEL_BLOCK <= pos_s

    tok = slice(None)
    qt = _queries_t(q_ref, tok)
    vct = vc_ref[0].T
    outs = []
    for kv in range(N_KV):
        kc = kc_ref[0, :, kv * HEAD_DIM:(kv + 1) * HEAD_DIM].astype(BF16)
        s = jnp.dot(kc, _group_t(qt, kv), preferred_element_type=F32)
        s = jnp.where(cmask, s, NEG_INF)
        m = jnp.max(s, axis=0, keepdims=True)
        p = jnp.exp2(s - m)
        p = p * (1.0 / jnp.sum(p, axis=0, keepdims=True))
        p = jnp.where(cmask, p, 0.0)
        outs.append(jnp.dot(vct[kv * HEAD_DIM:(kv + 1) * HEAD_DIM, :].astype(BF16),
                            p.astype(BF16), preferred_element_type=F32))
        psum = p[:, 0:tq]
        for g in range(1, GROUP):
            psum = psum + p[:, g * tq:(g + 1) * tq]

        imp = jnp.dot(ov_ref[...], psum, preferred_element_type=F32,
                      precision=lax.Precision.HIGHEST)
        imp = jnp.where(forced, FORCED_SCORE, imp)
        imp = jnp.where(valid, imp, NEG_INF)
        rank = jnp.zeros((n_sel, tq), jnp.int32)
        for i in range(n_sel):
            row = imp[i:i + 1, :]
            ahead = (row > imp) | ((row == imp) & (bid > i))
            rank = rank + ahead.astype(jnp.int32)
        bias_ref[0, kv * n_sel:(kv + 1) * n_sel, :] = jnp.where(rank < TOP_N, 0.0, NEG_INF)
    _store_heads(o_ref, tok, outs, _gates_t(gl_ref, tok), 0, tq)


def _cmp_attention(proj, kc, vc, overlap, *, tq=256):
    b, s, _ = proj.shape
    n_sel = overlap.shape[0]
    ncp = kc.shape[1]
    return pl.pallas_call(
        functools.partial(_cmp_attn_kernel, tq=tq, n_sel=n_sel),
        grid=(b, s // tq),
        in_specs=[
            pl.BlockSpec((1, tq, D_ATT), lambda i, n: (i, n, COL_QB // D_ATT)),
            pl.BlockSpec((1, ncp, D_KV), lambda i, n: (i, 0, 0)),
            pl.BlockSpec((1, ncp, D_KV), lambda i, n: (i, 0, 0)),
            pl.BlockSpec((1, tq, LANES), lambda i, n: (i, n, COL_GL // LANES)),
            pl.BlockSpec(overlap.shape, lambda i, n: (0, 0)),
        ],
        out_specs=[
            pl.BlockSpec((1, tq, D_ATT), lambda i, n: (i, n, 0)),
            pl.BlockSpec((1, N_KV * n_sel, tq), lambda i, n: (i, 0, n)),
        ],
        out_shape=[
            jax.ShapeDtypeStruct((b, s, D_ATT), F32),
            jax.ShapeDtypeStruct((b, N_KV * n_sel, s), F32),
        ],
        compiler_params=_params(("parallel", "arbitrary")),
        name="cmp_attn",
    )(proj, kc, vc, proj, overlap)


def _sel_attn_kernel(q_ref, k_ref, v_ref, bias_ref, gl_ref, *rest,
                     tq, tk, chains, n_sel, gate_row, n_cast):
    cast_src = rest[:n_cast]
    o_ref = rest[n_cast]
    cast_dst = rest[n_cast + 1:2 * n_cast + 1]
    qa_ref, ka_ref, vt_ref = rest[2 * n_cast + 1:]
    for src, dst in zip(cast_src, cast_dst):
        dst[...] = src[...].astype(BF16)

    n = pl.program_id(1)
    s_len = k_ref.shape[1]
    cols = GROUP * tq
    n_pad = LANES - HEAD_DIM - n_sel

    @pl.when(n == 0)
    def _():
        kpos = lax.broadcasted_iota(jnp.int32, (s_len, n_sel), 0)
        blk = lax.broadcasted_iota(jnp.int32, (s_len, n_sel), 1)
        onehot = (kpos // SEL_BLOCK == blk).astype(F32)
        zeros = jnp.zeros((s_len, n_pad), F32)
        vt = v_ref[0].T
        for kv in range(N_KV):
            k = k_ref[0, :, kv * HEAD_DIM:(kv + 1) * HEAD_DIM]
            ka_ref[kv] = jnp.concatenate([k, onehot, zeros], axis=1).astype(BF16)
            for t in range(s_len // tk):
                vt_ref[kv, t] = vt[kv * HEAD_DIM:(kv + 1) * HEAD_DIM,
                                   t * tk:(t + 1) * tk].astype(BF16)

    tok = slice(None)
    qt = _queries_t(q_ref, tok)
    zq = jnp.zeros((n_pad, tq), F32)
    for kv in range(N_KV):
        bias = bias_ref[0, kv * n_sel:(kv + 1) * n_sel, :]
        qa_ref[kv] = _group_t(qt, kv, extra=(bias, zq))
    ccols = cols // chains
    qpos = n * tq + (lax.broadcasted_iota(jnp.int32, (tk, ccols), 1) & (tq - 1))
    krow = lax.broadcasted_iota(jnp.int32, (tk, ccols), 0)

    def step(kt, state, causal):
        k0 = pl.multiple_of(kt * tk, tk)
        n_chain = len(state)
        scores, probs, new_state = {}, {}, {}

        def score(c):
            kv, part = divmod(c, chains)
            cs = slice(part * ccols, (part + 1) * ccols)
            s = jnp.dot(ka_ref[kv, pl.ds(k0, tk), :], qa_ref[kv, :, cs],
                        preferred_element_type=F32)
            scores[c] = jnp.where(k0 + krow <= qpos, s, NEG_INF) if causal else s

        def softmax(c):
            m_old, l_old, _ = state[c]
            s = scores.pop(c)
            m_new = jnp.maximum(m_old, jnp.max(s, axis=0, keepdims=True))
            alpha = jnp.exp2(m_old - m_new)
            p = jnp.exp2(s - m_new)
            l_new = alpha * l_old + jnp.sum(p, axis=0, keepdims=True)
            probs[c] = (m_new, l_new, alpha, p.astype(BF16))

        def weigh(c):
            m_new, l_new, alpha, p = probs.pop(c)
            acc_new = alpha * state[c][2] + jnp.dot(vt_ref[c // chains, kt], p,
                                                    preferred_element_type=F32)
            new_state[c] = (m_new, l_new, acc_new)

        for t in range(n_chain + SCORE_LEAD + 1):
            if t < n_chain:
                score(t)
            if 0 <= t - SCORE_LEAD < n_chain:
                softmax(t - SCORE_LEAD)
            if 0 <= t - SCORE_LEAD - 1 < n_chain:
                weigh(t - SCORE_LEAD - 1)
        return tuple(new_state[c] for c in range(n_chain))

    init = tuple((jnp.full((1, ccols), NEG_INF, F32), jnp.zeros((1, ccols), F32),
                  jnp.zeros((HEAD_DIM, ccols), F32)) for _ in range(N_KV * chains))
    n_full = (n * tq) // tk
    state = lax.fori_loop(0, n_full, functools.partial(step, causal=False), init)
    state = step(n_full, state, causal=True)

    outs = []
    for kv in range(N_KV):
        parts = [acc * (1.0 / l) for _, l, acc in state[kv * chains:(kv + 1) * chains]]
        outs.append(jnp.concatenate(parts, axis=1))
    _store_heads(o_ref, tok, outs, _gates_t(gl_ref, tok), gate_row, tq)


def _sel_attention(proj, bias, cast_jobs=(), *, tq=512, tk=512, chains=4):
    b, s, _ = proj.shape
    n_sel = bias.shape[1] // N_KV
    cols = GROUP * tq
    nq = s // tq
    steps = b * nq
    cast_in, cast_out, cast_shapes = [], [], []
    for w, layer in cast_jobs:
        _, k, nn = w.shape
        rows = k // steps
        cast_in.append(pl.BlockSpec((None, rows, nn),
                                    lambda i, n, layer=layer: (layer, i * nq + n, 0)))
        cast_out.append(pl.BlockSpec((None, rows, nn), lambda i, n: (0, i * nq + n, 0)))
        cast_shapes.append(jax.ShapeDtypeStruct((1, k, nn), BF16))
    outs = pl.pallas_call(
        functools.partial(_sel_attn_kernel, tq=tq, tk=tk, chains=chains, n_sel=n_sel,
                          gate_row=N_HEADS, n_cast=len(cast_jobs)),
        grid=(b, nq),
        in_specs=[
            pl.BlockSpec((1, tq, D_ATT), lambda i, n: (i, n, COL_QB // D_ATT)),
            pl.BlockSpec((1, s, D_KV), lambda i, n: (i, 0, COL_KS // D_KV)),
            pl.BlockSpec((1, s, D_KV), lambda i, n: (i, 0, COL_VS // D_KV)),
            pl.BlockSpec((1, N_KV * n_sel, tq), lambda i, n: (i, 0, n)),
            pl.BlockSpec((1, tq, LANES), lambda i, n: (i, n, COL_GL // LANES)),
            *cast_in,
        ],
        out_specs=[pl.BlockSpec((1, tq, D_ATT), lambda i, n: (i, n, 0)), *cast_out],
        out_shape=[jax.ShapeDtypeStruct((b, s, D_ATT), F32), *cast_shapes],
        scratch_shapes=[
            pltpu.VMEM((N_KV, LANES, cols), BF16),
            pltpu.VMEM((N_KV, s, LANES), BF16),
            pltpu.VMEM((N_KV, s // tk, HEAD_DIM, tk), BF16),
        ],
        compiler_params=_params(("parallel", "arbitrary"), VMEM_LIMIT_BIG),
        name="sel_attn",
    )(proj, proj, proj, bias, proj, *[w for w, _ in cast_jobs])
    return outs[0], list(outs[1:])


def _rglru_kernel(xr_ref, xg_ref, cw_ref, cb_ref, wa_ref, ba_ref, wx_ref, bx_ref, lam_ref,
                  o_ref, xbuf_ref, h_ref, *, ts):
    pad = SUBLANES

    @pl.when(pl.program_id(1) == 0)
    def _():
        xbuf_ref[0:pad, :] = jnp.zeros((pad, RNN_WIDTH), F32)
        h_ref[...] = jnp.zeros_like(h_ref)

    xr = xr_ref[0]
    xbuf_ref[pad:pad + ts, :] = xr
    xc = cb_ref[...] + cw_ref[CONV_WIDTH - 1:CONV_WIDTH, :] * xr
    for w in range(CONV_WIDTH - 1):
        shift = CONV_WIDTH - 1 - w
        xc = xc + cw_ref[w:w + 1, :] * xbuf_ref[pl.ds(pad - shift, ts), :]
    xbuf_ref[0:pad, :] = xr[ts - pad:ts, :]

    ra, rx = [], []
    for blk in range(RNN_WIDTH // MXU_DIM):
        xs = xc[:, blk * MXU_DIM:(blk + 1) * MXU_DIM].astype(BF16)
        ra.append(jnp.dot(xs, wa_ref[blk], preferred_element_type=F32))
        rx.append(jnp.dot(xs, wx_ref[blk], preferred_element_type=F32))
    r = 0.5 * (jnp.tanh(0.5 * (jnp.concatenate(ra, axis=1) + ba_ref[...])) + 1.0)
    gi = 0.5 * (jnp.tanh(0.5 * (jnp.concatenate(rx, axis=1) + bx_ref[...])) + 1.0)
    nl = -lam_ref[...]
    softplus = jnp.maximum(nl, 0.0) + jnp.log1p(jnp.exp(-jnp.abs(nl)))
    log_a = -LRU_C * r * softplus
    a = jnp.exp(log_a)
    u = jnp.sqrt(-jnp.tanh(log_a) * (a * a + 1.0)) * (gi * xc)

    n_slab = ts // SUBLANES
    a = a.reshape(n_slab, SUBLANES, RNN_WIDTH)
    u = u.reshape(n_slab, SUBLANES, RNN_WIDTH)
    sub = lax.broadcasted_iota(jnp.int32, a.shape, 1)
    d = 1
    while d < SUBLANES:
        keep = sub >= d
        a_prev = jnp.where(keep, pltpu.roll(a, d, 1), 1.0)
        u_prev = jnp.where(keep, pltpu.roll(u, d, 1), 0.0)
        u = a * u_prev + u
        a = a * a_prev
        d *= 2
    h_last = h_ref[0:1, :]
    slabs = []
    for r in range(n_slab):
        h = a[r] * h_last + u[r]
        slabs.append(h)
        h_last = h[SUBLANES - 1:SUBLANES, :]
    h_ref[0:1, :] = h_last
    o_ref[0] = jax.nn.gelu(xg_ref[0], approximate=True) * jnp.concatenate(slabs, axis=0)


def _rglru(proj, conv_w, conv_b, wa, ba, wx, bx, lam, *, ts=512):
    b, s, _ = proj.shape
    c = RNN_WIDTH
    vec = pl.BlockSpec((1, c), lambda i, n: (0, 0))
    wspec = pl.BlockSpec(wa.shape, lambda i, n: (0, 0, 0))
    return pl.pallas_call(
        functools.partial(_rglru_kernel, ts=ts),
        grid=(b, s // ts),
        in_specs=[
            pl.BlockSpec((1, ts, c), lambda i, n: (i, n, COL_XR // c)),
            pl.BlockSpec((1, ts, c), lambda i, n: (i, n, COL_XG // c)),
            pl.BlockSpec((CONV_WIDTH, c), lambda i, n: (0, 0)),
            vec, wspec, vec, wspec, vec, vec,
        ],
        out_specs=pl.BlockSpec((1, ts, c), lambda i, n: (i, n, 0)),
        out_shape=jax.ShapeDtypeStruct((b, s, c), F32),
        scratch_shapes=[pltpu.VMEM((ts + SUBLANES, c), F32), pltpu.VMEM((SUBLANES, c), F32)],
        compiler_params=_params(("parallel", "arbitrary")),
        name="rglru",
    )(proj, proj, conv_w, conv_b, wa, ba, wx, bx, lam)


def _outproj_kernel(x_ref, ya_ref, oc_ref, os_ref, ow_ref, yc_ref, gn_ref, w_ref, o_ref):
    ya = _rms(ya_ref[...], gn_ref[:, 0:D_ATT]).astype(BF16)
    yb = oc_ref[...] + os_ref[...] + ow_ref[...]
    yb = _rms(yb, gn_ref[:, D_ATT:2 * D_ATT]).astype(BF16)
    yc = _rms(yc_ref[...], gn_ref[:, 2 * D_ATT:]).astype(BF16)
    y = jnp.dot(ya, w_ref[0:D_ATT, :], preferred_element_type=F32)
    y = y + jnp.dot(yb, w_ref[D_ATT:2 * D_ATT, :], preferred_element_type=F32)
    y = y + jnp.dot(yc, w_ref[2 * D_ATT:, :], preferred_element_type=F32)
    o_ref[...] = x_ref[...] + y


def _outproj(x, ya, oc, osl, ow, yc, gn, w, layer, *, tm=512):
    t, d = x.shape
    att = pl.BlockSpec((tm, D_ATT), lambda i: (i, 0))
    return pl.pallas_call(
        _outproj_kernel,
        grid=(t // tm,),
        in_specs=[
            pl.BlockSpec((tm, d), lambda i: (i, 0)),
            att, att, att, att,
            pl.BlockSpec((tm, RNN_WIDTH), lambda i: (i, 0)),
            pl.BlockSpec((1, gn.shape[1]), lambda i: (0, 0)),
            pl.BlockSpec((None,) + w.shape[1:], lambda i: (layer, 0, 0)),
        ],
        out_specs=pl.BlockSpec((tm, d), lambda i: (i, 0)),
        out_shape=jax.ShapeDtypeStruct((t, d), F32),
        compiler_params=_params(("parallel",)),
        name="outproj",
    )(x, ya, oc, osl, ow, yc, gn, w)


def _block_diag(w, per):
    *lead, n, r, c = w.shape
    w = w.reshape(*lead, n // per, per, r, c)
    eye = jnp.eye(per, dtype=w.dtype)
    out = jnp.einsum('...prc,pq->...prqc', w, eye)
    return out.reshape(*lead, n // per, per * r, per * c)


def _compress_weights(cmp_pos, w1, b1, w2):
    pos = jnp.tile(cmp_pos, (1, 1, N_KV))
    w1 = w1.reshape(2, CMP_LEN, 1, HEAD_DIM, CMP_HIDDEN)
    w1 = _block_diag(jnp.broadcast_to(w1, (2, CMP_LEN, N_KV, HEAD_DIM, CMP_HIDDEN)), N_KV)
    w1 = w1.reshape(2, CMP_LEN, N_KV * HEAD_DIM, N_KV * CMP_HIDDEN)
    w2 = _block_diag(jnp.broadcast_to(w2[:, None], (2, N_KV, CMP_HIDDEN, HEAD_DIM)), N_KV)
    w2 = w2.reshape(2, N_KV * CMP_HIDDEN, N_KV * HEAD_DIM)
    b1 = jnp.tile(b1[:, None, :], (1, 1, N_KV))
    return pos, w1.astype(BF16), b1, w2.astype(BF16)


def _overlap_matrix(seq, n_cmp_padded):
    n_c = (seq - CMP_LEN) // CMP_STRIDE + 1
    n_sel = seq // SEL_BLOCK
    cs = np.arange(n_c)[:, None] * CMP_STRIDE
    ss = np.arange(n_sel)[None, :] * SEL_BLOCK
    ov = np.clip(np.minimum(cs + CMP_LEN, ss + SEL_BLOCK) - np.maximum(cs, ss), 0, None)
    out = np.zeros((n_cmp_padded, n_sel), np.float32)
    out[:n_c] = ov / CMP_LEN
    return jnp.asarray(out.T)


def kernel(x, ffn1_norm, ffn1_w_gate, ffn1_w_up, ffn1_w_down, mix_norm, w_in, swa_sinks, cmp_pos, cmp_w1, cmp_b1, cmp_w2, conv_w, conv_b, lru_wa, lru_ba, lru_wx, lru_bx, lru_lambda, group_norm, w_out, ffn2_norm, ffn2_w_gate, ffn2_w_up, ffn2_w_down, final_norm):
    b, s, d = x.shape
    depth = w_in.shape[0]
    t = b * s
    overlap = _overlap_matrix(s, s // CMP_STRIDE)
    gate_tile = MXU_DIM // RNN_BLOCK_WIDTH
    xt = x.reshape(t, d)
    ffn1_f32 = (ffn1_w_gate, ffn1_w_up, ffn1_w_down)
    ffn2_f32 = (ffn2_w_gate, ffn2_w_up, ffn2_w_down)
    ffn1_w = [_to_bf16(ffn1_w_gate, rows=256, layer=0), _to_bf16(ffn1_w_up, rows=256, layer=0),
              _to_bf16(ffn1_w_down, rows=512, layer=0)]
    w_in_b = _stage_w_in(w_in)
    w_out_b = _to_bf16(w_out, rows=512)
    for l in range(depth):
        xt = _ffn(xt, ffn1_norm[l][None], *ffn1_w, 0)

        proj = _inproj(xt, mix_norm[l][None], w_in_b, l)
        proj = proj.reshape(b, s, D_PROJ)

        ya = _window_attention(proj, COL_QA, COL_KA, COL_VA, SWA_WINDOW, sinks=swa_sinks[l])

        kc, vc = _compress(proj, *_compress_weights(cmp_pos[l], cmp_w1[l], cmp_b1[l], cmp_w2[l]))
        o_cmp, bias = _cmp_attention(proj, kc, vc, overlap)
        jobs = [(w, l) for w in ffn2_f32]
        if l + 1 < depth:
            jobs += [(w, l + 1) for w in ffn1_f32]
        o_slc, staged = _sel_attention(proj, bias, jobs)
        ffn2_w, ffn1_w = staged[:3], staged[3:]
        o_win = _window_attention(proj, COL_QB, COL_KW, COL_VW, NSA_WINDOW,
                                  gate_row=2 * N_HEADS, q_blocks=2)

        yc = _rglru(proj, conv_w[l], conv_b[l][None],
                    _block_diag(lru_wa[l], gate_tile).astype(BF16), lru_ba[l][None],
                    _block_diag(lru_wx[l], gate_tile).astype(BF16), lru_bx[l][None],
                    lru_lambda[l][None])

        xt = _outproj(xt, ya.reshape(t, D_ATT), o_cmp.reshape(t, D_ATT),
                      o_slc.reshape(t, D_ATT), o_win.reshape(t, D_ATT),
                      yc.reshape(t, RNN_WIDTH), group_norm[l][None], w_out_b, l)

        last = l == depth - 1
        xt = _ffn(xt, ffn2_norm[l][None], *ffn2_w, 0,
                  final_g=final_norm[None] if last else None)
    return xt.reshape(b, s, d)
```

```python
import functools

import jax
import jax.numpy as jnp
import numpy as np
from jax import lax
from jax.experimental import pallas as pl
from jax.experimental.pallas import tpu as pltpu

F32 = jnp.float32
BF16 = jnp.bfloat16

HEAD_DIM = 64
N_HEADS = 8
N_KV = 2
GROUP = N_HEADS // N_KV
D_ATT = N_HEADS * HEAD_DIM
D_KV = N_KV * HEAD_DIM
SWA_WINDOW = 128
NSA_WINDOW = 512
ATTN_BLOCK = 128
CMP_LEN = 32
CMP_STRIDE = 16
CMP_HIDDEN = 256
SEL_BLOCK = 64
TOP_N = 16
RNN_WIDTH = 1024
RNN_BLOCK_WIDTH = 64
CONV_WIDTH = 4
LRU_C = 8.0
RMS_EPS = 1e-6
NEG_INF = -1e30
FORCED_SCORE = 1e4
SCALE = HEAD_DIM ** -0.5
LOG2E = 1.4426950408889634
SCALE_LOG2 = SCALE * LOG2E

LANES = 128
SUBLANES = 8
MXU_DIM = 256
VMEM_BYTES = 64 * 1024 * 1024
VMEM_LIMIT = VMEM_BYTES - 8 * 1024 * 1024
VMEM_LIMIT_BIG = VMEM_BYTES - 4 * 1024 * 1024
DOWN_CHUNK = 512
INPROJ_CHUNK = 4 * MXU_DIM
SCORE_LEAD = 2

COL_XR = 0
COL_XG = 1024
COL_QA = 2048
COL_QB = 2560
COL_KA = 3072
COL_VA = 3200
COL_KC = 3328
COL_VC = 3456
COL_KS = 3584
COL_VS = 3712
COL_KW = 3840
COL_VW = 3968
COL_GL = 4096
D_PROJ = 4224

_REF_SPLITS = (D_ATT, D_KV, D_KV, D_ATT, D_KV, D_KV, D_KV, D_KV, D_KV, D_KV, 3 * N_HEADS,
               RNN_WIDTH, RNN_WIDTH)


def _params(sem, vmem_limit=VMEM_LIMIT):
    return pltpu.CompilerParams(dimension_semantics=sem, vmem_limit_bytes=vmem_limit)


def _rms(x, g):
    return x * lax.rsqrt(jnp.mean(x * x, axis=-1, keepdims=True) + RMS_EPS) * g


def _cast_kernel(w_ref, o_ref):
    o_ref[...] = w_ref[...].astype(BF16)


def _to_bf16(w, *, rows, layer=None):
    dep, k, n = w.shape
    first, count = (0, dep) if layer is None else (layer, 1)
    return pl.pallas_call(
        _cast_kernel,
        grid=(count, k // rows),
        in_specs=[pl.BlockSpec((1, rows, n), lambda l, i: (first + l, i, 0))],
        out_specs=pl.BlockSpec((1, rows, n), lambda l, i: (l, i, 0)),
        out_shape=jax.ShapeDtypeStruct((count, k, n), BF16),
        compiler_params=_params(("parallel", "parallel")),
        name="cast_bf16",
    )(w)


def _w_in_kernel(w_ref, o_ref):
    wt = w_ref[0]
    offs = np.concatenate([[0], np.cumsum(_REF_SPLITS)])
    qa, ka, va, qb, kc, vc, ks, vs, kw, vw, gl, xr, xg = [
        wt[int(offs[i]):int(offs[i + 1]), :] for i in range(len(_REF_SPLITS))]
    gl = jnp.concatenate([gl, jnp.zeros((LANES - gl.shape[0], wt.shape[1]), F32)], axis=0)
    col = 0
    for piece in (xr, xg, qa, qb, ka, va, kc, vc, ks, vs, kw, vw, gl):
        o_ref[0, :, col:col + piece.shape[0]] = piece.T.astype(BF16)
        col += piece.shape[0]


def _stage_w_in(w, *, rows=256):
    dep, k, n = w.shape
    return pl.pallas_call(
        _w_in_kernel,
        grid=(dep, k // rows),
        in_specs=[pl.BlockSpec((1, n, rows), lambda l, i: (l, 0, i))],
        out_specs=pl.BlockSpec((1, rows, D_PROJ), lambda l, i: (l, i, 0)),
        out_shape=jax.ShapeDtypeStruct((dep, k, D_PROJ), BF16),
        compiler_params=_params(("parallel", "parallel")),
        name="stage_w_in",
    )(jnp.swapaxes(w, 1, 2))


def _ffn_kernel(x_ref, g_ref, wg_ref, wu_ref, wd_ref, *rest, final_norm):
    if final_norm:
        fg_ref, o_ref, h_ref = rest
    else:
        o_ref, h_ref = rest
    j = pl.program_id(1)

    @pl.when(j == 0)
    def _():
        x = x_ref[...]
        h_ref[...] = _rms(x, g_ref[...]).astype(BF16)
        o_ref[...] = x

    h = h_ref[...]
    gate = jnp.dot(h, wg_ref[...], preferred_element_type=F32)
    up = jnp.dot(h, wu_ref[...], preferred_element_type=F32)
    act = (0.5 * gate * jax.nn.sigmoid(gate) * up).astype(BF16)
    for c in range(0, o_ref.shape[1], DOWN_CHUNK):
        o_ref[:, c:c + DOWN_CHUNK] += jnp.dot(act, wd_ref[:, c:c + DOWN_CHUNK],
                                              preferred_element_type=F32)

    if final_norm:
        @pl.when(j == pl.num_programs(1) - 1)
        def _():
            o_ref[...] = _rms(o_ref[...], fg_ref[...])


def _ffn(x, g, wg, wu, wd, layer, final_g=None, *, tm=1024, tf=512):
    t, d = x.shape
    f = wg.shape[2]
    final_norm = final_g is not None
    in_specs = [
        pl.BlockSpec((tm, d), lambda i, j: (i, 0)),
        pl.BlockSpec((1, d), lambda i, j: (0, 0)),
        pl.BlockSpec((None, d, tf), lambda i, j: (layer, 0, j)),
        pl.BlockSpec((None, d, tf), lambda i, j: (layer, 0, j)),
        pl.BlockSpec((None, tf, d), lambda i, j: (layer, j, 0)),
    ]
    args = [x, g, wg, wu, wd]
    if final_norm:
        in_specs.append(pl.BlockSpec((1, d), lambda i, j: (0, 0)))
        args.append(final_g)
    return pl.pallas_call(
        functools.partial(_ffn_kernel, final_norm=final_norm),
        grid=(t // tm, f // tf),
        in_specs=in_specs,
        out_specs=pl.BlockSpec((tm, d), lambda i, j: (i, 0)),
        out_shape=jax.ShapeDtypeStruct((t, d), F32),
        scratch_shapes=[pltpu.VMEM((tm, d), BF16)],
        compiler_params=_params(("parallel", "arbitrary"), VMEM_LIMIT_BIG),
        name="ffn",
    )(*args)


def _inproj_kernel(x_ref, g_ref, w_ref, o_ref):
    h = _rms(x_ref[...], g_ref[...]).astype(BF16)
    n = o_ref.shape[1]
    for c in range(0, n, INPROJ_CHUNK):
        hi = min(c + INPROJ_CHUNK, n)
        o_ref[:, c:hi] = jnp.dot(h, w_ref[:, c:hi], preferred_element_type=F32)


def _inproj(x, g, w, layer, *, tm=512):
    t, d = x.shape
    n = w.shape[2]
    return pl.pallas_call(
        _inproj_kernel,
        grid=(t // tm,),
        in_specs=[
            pl.BlockSpec((tm, d), lambda i: (i, 0)),
            pl.BlockSpec((1, d), lambda i: (0, 0)),
            pl.BlockSpec((None, d, n), lambda i: (layer, 0, 0), pipeline_mode=pl.Buffered(1)),
        ],
        out_specs=pl.BlockSpec((tm, n), lambda i: (i, 0)),
        out_shape=jax.ShapeDtypeStruct((t, n), F32),
        compiler_params=_params(("parallel",)),
        name="inproj",
    )(x, g, w)


def _queries_t(q_ref, tok):
    return q_ref[0, tok, :].T * SCALE_LOG2


def _group_t(qt, kv, extra=()):
    cols = []
    for g in range(GROUP):
        h = kv * GROUP + g
        parts = [qt[h * HEAD_DIM:(h + 1) * HEAD_DIM, :], *extra]
        cols.append(parts[0] if len(parts) == 1 else jnp.concatenate(parts, axis=0))
    return jnp.concatenate(cols, axis=1).astype(BF16)


def _store_heads(o_ref, tok, outs_t, gates_t, gate_row, tq):
    pieces = []
    for kv in range(N_KV):
        for g in range(GROUP):
            h = kv * GROUP + g
            piece = outs_t[kv][:, g * tq:(g + 1) * tq]
            if gates_t is not None:
                piece = piece * gates_t[gate_row + h:gate_row + h + 1, :]
            pieces.append(piece)
    o_ref[0, tok, :] = jnp.concatenate(pieces, axis=0).T


def _gates_t(gl_ref, tok):
    return jax.nn.sigmoid(gl_ref[0, tok, :].T)


def _window_kernel(*refs, n_prev, q_blocks, use_sink, gate_row):
    refs = list(refs)
    q_ref, k_ref, v_ref = refs[:3]
    rest = refs[3:]
    sink_ref = rest.pop(0) if use_sink else None
    gl_ref = rest.pop(0) if gate_row is not None else None
    o_ref, kb_ref, vt_ref = rest

    blk = ATTN_BLOCK
    cols = GROUP * blk
    span = (n_prev + 1) * blk
    pad = n_prev * blk
    s_len = k_ref.shape[1]
    n = pl.program_id(1)

    @pl.when(n == 0)
    def _():
        vt = v_ref[0].T
        lane = lax.broadcasted_iota(jnp.int32, (pad, LANES), 1)
        pad_keys = jnp.where(lane == HEAD_DIM, NEG_INF, 0.0).astype(BF16)
        for kv in range(N_KV):
            k = k_ref[0, :, kv * HEAD_DIM:(kv + 1) * HEAD_DIM]
            kb_ref[kv, 0:pad, :] = pad_keys
            kb_ref[kv, pad:pad + s_len, :] = jnp.concatenate(
                [k, jnp.zeros((s_len, LANES - HEAD_DIM), F32)], axis=1).astype(BF16)
            vt_ref[kv, :, 0:pad] = jnp.zeros((HEAD_DIM, pad), BF16)
            vt_ref[kv, :, pad:pad + s_len] = vt[kv * HEAD_DIM:(kv + 1) * HEAD_DIM, :].astype(BF16)

    krow = lax.broadcasted_iota(jnp.int32, (blk, cols), 0)
    qcol = lax.broadcasted_iota(jnp.int32, (blk, cols), 1) & (blk - 1)
    in_window = krow > qcol
    causal = krow <= qcol
    one_row = lax.broadcasted_iota(jnp.int32, (LANES - HEAD_DIM, blk), 0) == 0
    pad_hit = jnp.where(one_row, 1.0, 0.0)
    starts, scores = [], []
    for j in range(q_blocks):
        start = pl.multiple_of((n * q_blocks + j) * blk, blk)
        qt = _queries_t(q_ref, slice(j * blk, (j + 1) * blk))
        starts.append(start)
        for kv in range(N_KV):
            s = jnp.dot(kb_ref[kv, pl.ds(start, span), :], _group_t(qt, kv, extra=(pad_hit,)),
                        preferred_element_type=F32)
            bands = [jnp.where(in_window, s[0:blk], NEG_INF)]
            if n_prev > 1:
                bands.append(s[blk:span - blk])
            bands.append(jnp.where(causal, s[span - blk:span], NEG_INF))
            scores.append(jnp.concatenate(bands, axis=0))
    probs = []
    for c, s in enumerate(scores):
        kv = c % N_KV
        m = jnp.max(s, axis=0, keepdims=True)
        if use_sink:
            sk = jnp.concatenate([jnp.full((1, blk), sink_ref[kv * GROUP + g] * LOG2E, F32)
                                  for g in range(GROUP)], axis=1)
            m = jnp.maximum(m, sk)
            p = jnp.exp2(s - m)
            denom = jnp.sum(p, axis=0, keepdims=True) + jnp.exp2(sk - m)
        else:
            p = jnp.exp2(s - m)
            denom = jnp.sum(p, axis=0, keepdims=True)
        probs.append((p.astype(BF16), denom))
    for j in range(q_blocks):
        tok = slice(j * blk, (j + 1) * blk)
        outs = []
        for kv in range(N_KV):
            p, denom = probs[j * N_KV + kv]
            o = jnp.dot(vt_ref[kv, :, pl.ds(starts[j], span)], p,
                        preferred_element_type=F32)
            outs.append(o * (1.0 / denom))
        gates = _gates_t(gl_ref, tok) if gate_row is not None else None
        _store_heads(o_ref, tok, outs, gates, gate_row, blk)


def _window_attention(proj, q_col, k_col, v_col, window, sinks=None, gate_row=None, *,
                      q_blocks=8):
    b, s, _ = proj.shape
    n_prev = window // ATTN_BLOCK
    assert window == n_prev * ATTN_BLOCK, "band masks assume a window of whole blocks"
    pad = n_prev * ATTN_BLOCK
    blk = q_blocks * ATTN_BLOCK
    in_specs = [
        pl.BlockSpec((1, blk, D_ATT), lambda i, n: (i, n, q_col // D_ATT)),
        pl.BlockSpec((1, s, D_KV), lambda i, n: (i, 0, k_col // D_KV)),
        pl.BlockSpec((1, s, D_KV), lambda i, n: (i, 0, v_col // D_KV)),
    ]
    args = [proj, proj, proj]
    if sinks is not None:
        in_specs.append(pl.BlockSpec(memory_space=pltpu.SMEM))
        args.append(sinks)
    if gate_row is not None:
        in_specs.append(pl.BlockSpec((1, blk, LANES), lambda i, n: (i, n, COL_GL // LANES)))
        args.append(proj)
    return pl.pallas_call(
        functools.partial(_window_kernel, n_prev=n_prev, q_blocks=q_blocks,
                          use_sink=sinks is not None, gate_row=gate_row),
        grid=(b, s // blk),
        in_specs=in_specs,
        out_specs=pl.BlockSpec((1, blk, D_ATT), lambda i, n: (i, n, 0)),
        out_shape=jax.ShapeDtypeStruct((b, s, D_ATT), F32),
        scratch_shapes=[pltpu.VMEM((N_KV, pad + s, LANES), BF16),
                        pltpu.VMEM((N_KV, HEAD_DIM, pad + s), BF16)],
        compiler_params=_params(("parallel", "arbitrary")),
        name=f"window{window}",
    )(*args)


def _compress_kernel(k_ref, v_ref, pos_ref, w1_ref, b1_ref, w2_ref, kc_ref, vc_ref):
    n_rows = kc_ref.shape[1]
    for i, (src_ref, dst_ref) in enumerate(((k_ref, kc_ref), (v_ref, vc_ref))):
        lo = jnp.zeros((n_rows, N_KV * CMP_HIDDEN), F32)
        hi = jnp.zeros((n_rows, N_KV * CMP_HIDDEN), F32)
        for l in range(CMP_STRIDE):
            x = src_ref[0, pl.ds(l, n_rows, stride=CMP_STRIDE), :]
            lo = lo + jnp.dot((x + pos_ref[i, l:l + 1, :]).astype(BF16), w1_ref[i, l],
                              preferred_element_type=F32)
            hi = hi + jnp.dot((x + pos_ref[i, CMP_STRIDE + l:CMP_STRIDE + l + 1, :]).astype(BF16),
                              w1_ref[i, CMP_STRIDE + l], preferred_element_type=F32)
        hid = lo + pltpu.roll(hi, n_rows - 1, 0) + b1_ref[i]
        act = jax.nn.gelu(hid, approximate=True).astype(BF16)
        dst_ref[0] = jnp.dot(act, w2_ref[i], preferred_element_type=F32)


def _compress(proj, pos, w1, b1, w2):
    b, s, _ = proj.shape
    n_rows = s // CMP_STRIDE
    out_spec = pl.BlockSpec((1, n_rows, D_KV), lambda i: (i, 0, 0))
    out_shape = jax.ShapeDtypeStruct((b, n_rows, D_KV), F32)
    return pl.pallas_call(
        _compress_kernel,
        grid=(b,),
        in_specs=[
            pl.BlockSpec((1, s, D_KV), lambda i: (i, 0, COL_KC // D_KV)),
            pl.BlockSpec((1, s, D_KV), lambda i: (i, 0, COL_VC // D_KV)),
            pl.BlockSpec(pos.shape, lambda i: (0, 0, 0)),
            pl.BlockSpec(w1.shape, lambda i: (0, 0, 0, 0)),
            pl.BlockSpec(b1.shape, lambda i: (0, 0, 0)),
            pl.BlockSpec(w2.shape, lambda i: (0, 0, 0)),
        ],
        out_specs=[out_spec, out_spec],
        out_shape=[out_shape, out_shape],
        compiler_params=_params(("parallel",)),
        name="compress",
    )(proj, proj, pos, w1, b1, w2)


def _cmp_attn_kernel(q_ref, kc_ref, vc_ref, gl_ref, ov_ref, o_ref, bias_ref, *, tq, n_sel):
    n = pl.program_id(1)
    ncp = kc_ref.shape[1]
    cols = GROUP * tq
    pos_c = n * tq + (lax.broadcasted_iota(jnp.int32, (ncp, cols), 1) & (tq - 1))
    cend = lax.broadcasted_iota(jnp.int32, (ncp, cols), 0) * CMP_STRIDE + (CMP_LEN - 1)
    cmask = cend <= pos_c

    pos_s = n * tq + lax.broadcasted_iota(jnp.int32, (n_sel, tq), 1)
    bid = lax.broadcasted_iota(jnp.int32, (n_sel, tq), 0)
    cur = pos_s // SEL_BLOCK
    forced = (bid == 0) | (bid == cur) | (bid == cur - 1)
    valid = bid * SEL_BLOCK <= pos_s

    tok = slice(None)
    qt = _queries_t(q_ref, tok)
    vct = vc_ref[0].T
    outs = []
    for kv in range(N_KV):
        kc = kc_ref[0, :, kv * HEAD_DIM:(kv + 1) * HEAD_DIM].astype(BF16)
        s = jnp.dot(kc, _group_t(qt, kv), preferred_element_type=F32)
        s = jnp.where(cmask, s, NEG_INF)
        m = jnp.max(s, axis=0, keepdims=True)
        p = jnp.exp2(s - m)
        p = p * (1.0 / jnp.sum(p, axis=0, keepdims=True))
        p = jnp.where(cmask, p, 0.0)
        outs.append(jnp.dot(vct[kv * HEAD_DIM:(kv + 1) * HEAD_DIM, :].astype(BF16),
                            p.astype(BF16), preferred_element_type=F32))
        psum = p[:, 0:tq]
        for g in range(1, GROUP):
            psum = psum + p[:, g * tq:(g + 1) * tq]

        imp = jnp.dot(ov_ref[...], psum, preferred_element_type=F32,
                      precision=lax.Precision.HIGHEST)
        imp = jnp.where(forced, FORCED_SCORE, imp)
        imp = jnp.where(valid, imp, NEG_INF)
        rank = jnp.zeros((n_sel, tq), jnp.int32)
        for i in range(n_sel):
            row = imp[i:i + 1, :]
            ahead = (row > imp) | ((row == imp) & (bid > i))
            rank = rank + ahead.astype(jnp.int32)
        bias_ref[0, kv * n_sel:(kv + 1) * n_sel, :] = jnp.where(rank < TOP_N, 0.0, NEG_INF)
    _store_heads(o_ref, tok, outs, _gates_t(gl_ref, tok), 0, tq)


def _cmp_attention(proj, kc, vc, overlap, *, tq=512):
    b, s, _ = proj.shape
    n_sel = overlap.shape[0]
    ncp = kc.shape[1]
    return pl.pallas_call(
        functools.partial(_cmp_attn_kernel, tq=tq, n_sel=n_sel),
        grid=(b, s // tq),
        in_specs=[
            pl.BlockSpec((1, tq, D_ATT), lambda i, n: (i, n, COL_QB // D_ATT)),
            pl.BlockSpec((1, ncp, D_KV), lambda i, n: (i, 0, 0)),
            pl.BlockSpec((1, ncp, D_KV), lambda i, n: (i, 0, 0)),
            pl.BlockSpec((1, tq, LANES), lambda i, n: (i, n, COL_GL // LANES)),
            pl.BlockSpec(overlap.shape, lambda i, n: (0, 0)),
        ],
        out_specs=[
            pl.BlockSpec((1, tq, D_ATT), lambda i, n: (i, n, 0)),
            pl.BlockSpec((1, N_KV * n_sel, tq), lambda i, n: (i, 0, n)),
        ],
        out_shape=[
            jax.ShapeDtypeStruct((b, s, D_ATT), F32),
            jax.ShapeDtypeStruct((b, N_KV * n_sel, s), F32),
        ],
        compiler_params=_params(("parallel", "arbitrary")),
        name="cmp_attn",
    )(proj, kc, vc, proj, overlap)


def _sel_attn_kernel(q_ref, k_ref, v_ref, bias_ref, gl_ref, *rest,
                     tq, tk, chains, n_sel, gate_row, n_cast):
    cast_src = rest[:n_cast]
    o_ref = rest[n_cast]
    cast_dst = rest[n_cast + 1:2 * n_cast + 1]
    qa_ref, ka_ref, vt_ref = rest[2 * n_cast + 1:]
    for src, dst in zip(cast_src, cast_dst):
        dst[...] = src[...].astype(BF16)

    n = pl.program_id(1)
    s_len = k_ref.shape[1]
    cols = GROUP * tq
    n_pad = LANES - HEAD_DIM - n_sel

    @pl.when(n == 0)
    def _():
        kpos = lax.broadcasted_iota(jnp.int32, (s_len, n_sel), 0)
        blk = lax.broadcasted_iota(jnp.int32, (s_len, n_sel), 1)
        onehot = (kpos // SEL_BLOCK == blk).astype(F32)
        zeros = jnp.zeros((s_len, n_pad), F32)
        vt = v_ref[0].T
        for kv in range(N_KV):
            k = k_ref[0, :, kv * HEAD_DIM:(kv + 1) * HEAD_DIM]
            ka_ref[kv] = jnp.concatenate([k, onehot, zeros], axis=1).astype(BF16)
            for t in range(s_len // tk):
                vt_ref[kv, t] = vt[kv * HEAD_DIM:(kv + 1) * HEAD_DIM,
                                   t * tk:(t + 1) * tk].astype(BF16)

    tok = slice(None)
    qt = _queries_t(q_ref, tok)
    zq = jnp.zeros((n_pad, tq), F32)
    for kv in range(N_KV):
        bias = bias_ref[0, kv * n_sel:(kv + 1) * n_sel, :]
        qa_ref[kv] = _group_t(qt, kv, extra=(bias, zq))
    ccols = cols // chains
    qpos = n * tq + (lax.broadcasted_iota(jnp.int32, (tk, ccols), 1) & (tq - 1))
    krow = lax.broadcasted_iota(jnp.int32, (tk, ccols), 0)

    def step(kt, state, causal):
        k0 = pl.multiple_of(kt * tk, tk)
        n_chain = len(state)
        scores, probs, new_state = {}, {}, {}

        def score(c):
            kv, part = divmod(c, chains)
            cs = slice(part * ccols, (part + 1) * ccols)
            s = jnp.dot(ka_ref[kv, pl.ds(k0, tk), :], qa_ref[kv, :, cs],
                        preferred_element_type=F32)
            scores[c] = jnp.where(k0 + krow <= qpos, s, NEG_INF) if causal else s

        def softmax(c):
            m_old, l_old, _ = state[c]
            s = scores.pop(c)
            m_new = jnp.maximum(m_old, jnp.max(s, axis=0, keepdims=True))
            alpha = jnp.exp2(m_old - m_new)
            p = jnp.exp2(s - m_new)
            l_new = alpha * l_old + jnp.sum(p, axis=0, keepdims=True)
            probs[c] = (m_new, l_new, alpha, p.astype(BF16))

        def weigh(c):
            m_new, l_new, alpha, p = probs.pop(c)
            acc_new = alpha * state[c][2] + jnp.dot(vt_ref[c // chains, kt], p,
                                                    preferred_element_type=F32)
            new_state[c] = (m_new, l_new, acc_new)

        for t in range(n_chain + SCORE_LEAD + 1):
            if t < n_chain:
                score(t)
            if 0 <= t - SCORE_LEAD < n_chain:
                softmax(t - SCORE_LEAD)
            if 0 <= t - SCORE_LEAD - 1 < n_chain:
                weigh(t - SCORE_LEAD - 1)
        return tuple(new_state[c] for c in range(n_chain))

    init = tuple((jnp.full((1, ccols), NEG_INF, F32), jnp.zeros((1, ccols), F32),
                  jnp.zeros((HEAD_DIM, ccols), F32)) for _ in range(N_KV * chains))
    n_full = (n * tq) // tk
    state = lax.fori_loop(0, n_full, functools.partial(step, causal=False), init)
    state = step(n_full, state, causal=True)

    outs = []
    for kv in range(N_KV):
        parts = [acc * (1.0 / l) for _, l, acc in state[kv * chains:(kv + 1) * chains]]
        outs.append(jnp.concatenate(parts, axis=1))
    _store_heads(o_ref, tok, outs, _gates_t(gl_ref, tok), gate_row, tq)


def _sel_attention(proj, bias, cast_jobs=(), *, tq=512, tk=512, chains=4):
    b, s, _ = proj.shape
    n_sel = bias.shape[1] // N_KV
    cols = GROUP * tq
    nq = s // tq
    steps = b * nq
    cast_in, cast_out, cast_shapes = [], [], []
    for w, layer in cast_jobs:
        _, k, nn = w.shape
        rows = k // steps
        cast_in.append(pl.BlockSpec((None, rows, nn),
                                    lambda i, n, layer=layer: (layer, i * nq + n, 0)))
        cast_out.append(pl.BlockSpec((None, rows, nn), lambda i, n: (0, i * nq + n, 0)))
        cast_shapes.append(jax.ShapeDtypeStruct((1, k, nn), BF16))
    outs = pl.pallas_call(
        functools.partial(_sel_attn_kernel, tq=tq, tk=tk, chains=chains, n_sel=n_sel,
                          gate_row=N_HEADS, n_cast=len(cast_jobs)),
        grid=(b, nq),
        in_specs=[
            pl.BlockSpec((1, tq, D_ATT), lambda i, n: (i, n, COL_QB // D_ATT)),
            pl.BlockSpec((1, s, D_KV), lambda i, n: (i, 0, COL_KS // D_KV)),
            pl.BlockSpec((1, s, D_KV), lambda i, n: (i, 0, COL_VS // D_KV)),
            pl.BlockSpec((1, N_KV * n_sel, tq), lambda i, n: (i, 0, n)),
            pl.BlockSpec((1, tq, LANES), lambda i, n: (i, n, COL_GL // LANES)),
            *cast_in,
        ],
        out_specs=[pl.BlockSpec((1, tq, D_ATT), lambda i, n: (i, n, 0)), *cast_out],
        out_shape=[jax.ShapeDtypeStruct((b, s, D_ATT), F32), *cast_shapes],
        scratch_shapes=[
            pltpu.VMEM((N_KV, LANES, cols), BF16),
            pltpu.VMEM((N_KV, s, LANES), BF16),
            pltpu.VMEM((N_KV, s // tk, HEAD_DIM, tk), BF16),
        ],
        compiler_params=_params(("parallel", "arbitrary"), VMEM_LIMIT_BIG),
        name="sel_attn",
    )(proj, proj, proj, bias, proj, *[w for w, _ in cast_jobs])
    return outs[0], list(outs[1:])


def _rglru_kernel(xr_ref, xg_ref, cw_ref, cb_ref, wa_ref, ba_ref, wx_ref, bx_ref, lam_ref,
                  o_ref, xbuf_ref, h_ref, *, ts):
    pad = SUBLANES

    @pl.when(pl.program_id(1) == 0)
    def _():
        xbuf_ref[0:pad, :] = jnp.zeros((pad, RNN_WIDTH), F32)
        h_ref[...] = jnp.zeros_like(h_ref)

    xr = xr_ref[0]
    xbuf_ref[pad:pad + ts, :] = xr
    xc = cb_ref[...] + cw_ref[CONV_WIDTH - 1:CONV_WIDTH, :] * xr
    for w in range(CONV_WIDTH - 1):
        shift = CONV_WIDTH - 1 - w
        xc = xc + cw_ref[w:w + 1, :] * xbuf_ref[pl.ds(pad - shift, ts), :]
    xbuf_ref[0:pad, :] = xr[ts - pad:ts, :]

    ra, rx = [], []
    for blk in range(RNN_WIDTH // MXU_DIM):
        xs = xc[:, blk * MXU_DIM:(blk + 1) * MXU_DIM].astype(BF16)
        ra.append(jnp.dot(xs, wa_ref[blk], preferred_element_type=F32))
        rx.append(jnp.dot(xs, wx_ref[blk], preferred_element_type=F32))
    r = 0.5 * (jnp.tanh(0.5 * (jnp.concatenate(ra, axis=1) + ba_ref[...])) + 1.0)
    gi = 0.5 * (jnp.tanh(0.5 * (jnp.concatenate(rx, axis=1) + bx_ref[...])) + 1.0)
    nl = -lam_ref[...]
    softplus = jnp.maximum(nl, 0.0) + jnp.log1p(jnp.exp(-jnp.abs(nl)))
    log_a = -LRU_C * r * softplus
    a = jnp.exp(log_a)
    u = jnp.sqrt(-jnp.tanh(log_a) * (a * a + 1.0)) * (gi * xc)

    n_slab = ts // SUBLANES
    a = a.reshape(n_slab, SUBLANES, RNN_WIDTH)
    u = u.reshape(n_slab, SUBLANES, RNN_WIDTH)
    sub = lax.broadcasted_iota(jnp.int32, a.shape, 1)
    d = 1
    while d < SUBLANES:
        keep = sub >= d
        a_prev = jnp.where(keep, pltpu.roll(a, d, 1), 1.0)
        u_prev = jnp.where(keep, pltpu.roll(u, d, 1), 0.0)
        u = a * u_prev + u
        a = a * a_prev
        d *= 2
    h_last = h_ref[0:1, :]
    slabs = []
    for r in range(n_slab):
        h = a[r] * h_last + u[r]
        slabs.append(h)
        h_last = h[SUBLANES - 1:SUBLANES, :]
    h_ref[0:1, :] = h_last
    o_ref[0] = jax.nn.gelu(xg_ref[0], approximate=True) * jnp.concatenate(slabs, axis=0)


def _rglru(proj, conv_w, conv_b, wa, ba, wx, bx, lam, *, ts=512):
    b, s, _ = proj.shape
    c = RNN_WIDTH
    vec = pl.BlockSpec((1, c), lambda i, n: (0, 0))
    wspec = pl.BlockSpec(wa.shape, lambda i, n: (0, 0, 0))
    return pl.pallas_call(
        functools.partial(_rglru_kernel, ts=ts),
        grid=(b, s // ts),
        in_specs=[
            pl.BlockSpec((1, ts, c), lambda i, n: (i, n, COL_XR // c)),
            pl.BlockSpec((1, ts, c), lambda i, n: (i, n, COL_XG // c)),
            pl.BlockSpec((CONV_WIDTH, c), lambda i, n: (0, 0)),
            vec, wspec, vec, wspec, vec, vec,
        ],
        out_specs=pl.BlockSpec((1, ts, c), lambda i, n: (i, n, 0)),
        out_shape=jax.ShapeDtypeStruct((b, s, c), F32),
        scratch_shapes=[pltpu.VMEM((ts + SUBLANES, c), F32), pltpu.VMEM((SUBLANES, c), F32)],
        compiler_params=_params(("parallel", "arbitrary")),
        name="rglru",
    )(proj, proj, conv_w, conv_b, wa, ba, wx, bx, lam)


def _outproj_kernel(x_ref, ya_ref, oc_ref, os_ref, ow_ref, yc_ref, gn_ref, w_ref, o_ref):
    ya = _rms(ya_ref[...], gn_ref[:, 0:D_ATT]).astype(BF16)
    yb = oc_ref[...] + os_ref[...] + ow_ref[...]
    yb = _rms(yb, gn_ref[:, D_ATT:2 * D_ATT]).astype(BF16)
    yc = _rms(yc_ref[...], gn_ref[:, 2 * D_ATT:]).astype(BF16)
    y = jnp.dot(ya, w_ref[0:D_ATT, :], preferred_element_type=F32)
    y = y + jnp.dot(yb, w_ref[D_ATT:2 * D_ATT, :], preferred_element_type=F32)
    y = y + jnp.dot(yc, w_ref[2 * D_ATT:, :], preferred_element_type=F32)
    o_ref[...] = x_ref[...] + y


def _outproj(x, ya, oc, osl, ow, yc, gn, w, layer, *, tm=512):
    t, d = x.shape
    att = pl.BlockSpec((tm, D_ATT), lambda i: (i, 0))
    return pl.pallas_call(
        _outproj_kernel,
        grid=(t // tm,),
        in_specs=[
            pl.BlockSpec((tm, d), lambda i: (i, 0)),
            att, att, att, att,
            pl.BlockSpec((tm, RNN_WIDTH), lambda i: (i, 0)),
            pl.BlockSpec((1, gn.shape[1]), lambda i: (0, 0)),
            pl.BlockSpec((None,) + w.shape[1:], lambda i: (layer, 0, 0)),
        ],
        out_specs=pl.BlockSpec((tm, d), lambda i: (i, 0)),
        out_shape=jax.ShapeDtypeStruct((t, d), F32),
        compiler_params=_params(("parallel",)),
        name="outproj",
    )(x, ya, oc, osl, ow, yc, gn, w)


def _block_diag(w, per):
    *lead, n, r, c = w.shape
    w = w.reshape(*lead, n // per, per, r, c)
    eye = jnp.eye(per, dtype=w.dtype)
    out = jnp.einsum('...prc,pq->...prqc', w, eye)
    return out.reshape(*lead, n // per, per * r, per * c)


def _compress_weights(cmp_pos, w1, b1, w2):
    pos = jnp.tile(cmp_pos, (1, 1, N_KV))
    w1 = w1.reshape(2, CMP_LEN, 1, HEAD_DIM, CMP_HIDDEN)
    w1 = _block_diag(jnp.broadcast_to(w1, (2, CMP_LEN, N_KV, HEAD_DIM, CMP_HIDDEN)), N_KV)
    w1 = w1.reshape(2, CMP_LEN, N_KV * HEAD_DIM, N_KV * CMP_HIDDEN)
    w2 = _block_diag(jnp.broadcast_to(w2[:, None], (2, N_KV, CMP_HIDDEN, HEAD_DIM)), N_KV)
    w2 = w2.reshape(2, N_KV * CMP_HIDDEN, N_KV * HEAD_DIM)
    b1 = jnp.tile(b1[:, None, :], (1, 1, N_KV))
    return pos, w1.astype(BF16), b1, w2.astype(BF16)


def _overlap_matrix(seq, n_cmp_padded):
    n_c = (seq - CMP_LEN) // CMP_STRIDE + 1
    n_sel = seq // SEL_BLOCK
    cs = np.arange(n_c)[:, None] * CMP_STRIDE
    ss = np.arange(n_sel)[None, :] * SEL_BLOCK
    ov = np.clip(np.minimum(cs + CMP_LEN, ss + SEL_BLOCK) - np.maximum(cs, ss), 0, None)
    out = np.zeros((n_cmp_padded, n_sel), np.float32)
    out[:n_c] = ov / CMP_LEN
    return jnp.asarray(out.T)


def kernel(x, ffn1_norm, ffn1_w_gate, ffn1_w_up, ffn1_w_down, mix_norm, w_in, swa_sinks, cmp_pos, cmp_w1, cmp_b1, cmp_w2, conv_w, conv_b, lru_wa, lru_ba, lru_wx, lru_bx, lru_lambda, group_norm, w_out, ffn2_norm, ffn2_w_gate, ffn2_w_up, ffn2_w_down, final_norm):
    b, s, d = x.shape
    depth = w_in.shape[0]
    t = b * s
    overlap = _overlap_matrix(s, s // CMP_STRIDE)
    gate_tile = MXU_DIM // RNN_BLOCK_WIDTH
    xt = x.reshape(t, d)
    ffn1_f32 = (ffn1_w_gate, ffn1_w_up, ffn1_w_down)
    ffn2_f32 = (ffn2_w_gate, ffn2_w_up, ffn2_w_down)
    ffn1_w = [_to_bf16(ffn1_w_gate, rows=256, layer=0), _to_bf16(ffn1_w_up, rows=256, layer=0),
              _to_bf16(ffn1_w_down, rows=512, layer=0)]
    w_in_b = _stage_w_in(w_in)
    w_out_b = _to_bf16(w_out, rows=512)
    for l in range(depth):
        xt = _ffn(xt, ffn1_norm[l][None], *ffn1_w, 0)

        proj = _inproj(xt, mix_norm[l][None], w_in_b, l)
        proj = proj.reshape(b, s, D_PROJ)

        ya = _window_attention(proj, COL_QA, COL_KA, COL_VA, SWA_WINDOW, sinks=swa_sinks[l])

        kc, vc = _compress(proj, *_compress_weights(cmp_pos[l], cmp_w1[l], cmp_b1[l], cmp_w2[l]))
        o_cmp, bias = _cmp_attention(proj, kc, vc, overlap)
        jobs = [(w, l) for w in ffn2_f32]
        if l + 1 < depth:
            jobs += [(w, l + 1) for w in ffn1_f32]
        o_slc, staged = _sel_attention(proj, bias, jobs)
        ffn2_w, ffn1_w = staged[:3], staged[3:]
        o_win = _window_attention(proj, COL_QB, COL_KW, COL_VW, NSA_WINDOW,
                                  gate_row=2 * N_HEADS, q_blocks=4)

        yc = _rglru(proj, conv_w[l], conv_b[l][None],
                    _block_diag(lru_wa[l], gate_tile).astype(BF16), lru_ba[l][None],
                    _block_diag(lru_wx[l], gate_tile).astype(BF16), lru_bx[l][None],
                    lru_lambda[l][None])

        xt = _outproj(xt, ya.reshape(t, D_ATT), o_cmp.reshape(t, D_ATT),
                      o_slc.reshape(t, D_ATT), o_win.reshape(t, D_ATT),
                      yc.reshape(t, RNN_WIDTH), group_norm[l][None], w_out_b, l)

        last = l == depth - 1
        xt = _ffn(xt, ffn2_norm[l][None], *ffn2_w, 0,
                  final_g=final_norm[None] if last else None)
    return xt.reshape(b, s, d)
```
